```python
import math
import jax
import jax.numpy as jnp
from jax import lax
import numpy as np

D_MODEL = 4096
BATCH = 4
SEQ = 2048
DEPTH = 2
DEC_BATCH = 8
DEC_SEQ = 4
PAST_LEN = 16384
PAGE_SIZE = 128

HEAD_DIM = 128
ROPE_THETA = 10000.0
H_DIFF = 4
KV_DIFF = 2
H_NSA = 12
KV_NSA = 2
H_FOX = 12
KV_FOX = 4
W_DIFF = H_DIFF * 2 * HEAD_DIM
W_NSA = H_NSA * HEAD_DIM
W_FOX = H_FOX * HEAD_DIM
CMP_BLOCK = 32
CMP_STRIDE = 16
SEL_BLOCK = 64
SEL_TOPK = 16
WINDOW = 512
N_EXPERTS = 32
TOP_K = 4
D_EXPERT = D_MODEL // 2
SWIGLU_ALPHA = 1.702
SWIGLU_LIMIT = 7.0
Q_BLOCK = 128
SEL_Q_BLOCK = 64
MOE_BLOCK = 128
LN_EPS = 1e-5
DEEPNORM_ALPHA = (2 * DEPTH) ** 0.25
DEEPNORM_BETA = (8 * DEPTH) ** -0.25
FORGET_BIAS_INIT = 4.0
ATTN_SCALE = HEAD_DIM ** -0.5
IN_SIZES = (
    H_DIFF * 2 * HEAD_DIM,
    2 * KV_DIFF * 2 * HEAD_DIM,
    H_NSA * HEAD_DIM,
    6 * KV_NSA * HEAD_DIM,
    3 * H_NSA,
    H_FOX * HEAD_DIM,
    2 * KV_FOX * HEAD_DIM,
    H_FOX,
    3 * D_MODEL,
)
N_IN = sum(IN_SIZES)

kernel_name = 'hybrid_diff_nsa_fox_moe_deepnorm_step'


def layer_norm(x, g, b):
    xf = x.astype(jnp.float32)
    mu = jnp.mean(xf, -1, keepdims=True)
    var = jnp.mean(jnp.square(xf - mu), -1, keepdims=True)
    return ((xf - mu) * lax.rsqrt(var + LN_EPS) * g + b).astype(x.dtype)


def rope(x, pos):
    half = HEAD_DIM // 2
    inv = ROPE_THETA ** (-jnp.arange(half, dtype=jnp.float32) / half)
    ang = pos.astype(jnp.float32)[:, None] * inv[None, :]
    shp = (1, pos.shape[0]) + (1,) * (x.ndim - 3) + (half,)
    cos = jnp.cos(ang).reshape(shp)
    sin = jnp.sin(ang).reshape(shp)
    xf = x.astype(jnp.float32)
    x1, x2 = xf[..., :half], xf[..., half:]
    return jnp.concatenate([x1 * cos - x2 * sin, x2 * cos + x1 * sin], -1).astype(x.dtype)


def masked_softmax(s, mask):
    s = jnp.where(mask, s, -jnp.inf)
    m = jnp.max(s, -1, keepdims=True)
    m = jnp.where(jnp.isfinite(m), m, 0.0)
    p = jnp.exp(s - m)
    d = jnp.sum(p, -1, keepdims=True)
    return p / jnp.where(d > 0, d, 1.0)


def gqa_attend(qg, k, v, mask, bias=None):
    s = jnp.einsum('bqghd,bsgd->bghqs', qg, k).astype(jnp.float32) * ATTN_SCALE
    if bias is not None:
        s = s + bias
    p = masked_softmax(s, mask)
    return jnp.einsum('bghqs,bsgd->bqghd', p.astype(v.dtype), v)


def map_query_blocks(fn, arrays, q_pos, block):
    n_q = q_pos.shape[0]
    blk = block if n_q % block == 0 else n_q
    nb = n_q // blk
    if nb == 1:
        return fn(*arrays, q_pos)
    xs = tuple(jnp.moveaxis(a.reshape((a.shape[0], nb, blk) + a.shape[2:]), 1, 0) for a in arrays)
    out = lax.map(lambda t: fn(*t[0], t[1]), (xs, q_pos.reshape(nb, blk)))
    out = jnp.moveaxis(out, 0, 1)
    return out.reshape((out.shape[0], n_q) + out.shape[3:])


def gather_pages(pool, page_table):
    g = pool[page_table]
    return g.reshape((page_table.shape[0], page_table.shape[1] * pool.shape[1]) + pool.shape[2:])


def diff_attention(q, k, v, q_pos, lam, lam_init, subln_g):
    B, T = q.shape[:2]
    n_k = k.shape[1]
    qg = q.reshape(B, T, KV_DIFF, H_DIFF // KV_DIFF, 2, HEAD_DIM)
    k_pos = jnp.arange(n_k)

    def block(qb, pb):
        s = jnp.einsum('bqghcd,bsgcd->cbghqs', qb, k).astype(jnp.float32) * ATTN_SCALE
        p = masked_softmax(s, k_pos[None, :] <= pb[:, None])
        a = p[0] - lam * p[1]
        return jnp.einsum('bghqs,bsgd->bqghd', a.astype(v.dtype), v)

    o = map_query_blocks(block, (qg,), q_pos, Q_BLOCK).astype(jnp.float32)
    o = o * lax.rsqrt(jnp.mean(jnp.square(o), -1, keepdims=True) + LN_EPS) * subln_g * (1.0 - lam_init)
    return o.reshape(B, T, W_DIFF)


def fox_attention(q, k, v, cq, ck, q_pos):
    B, T = q.shape[:2]
    n_k = k.shape[1]
    G, Hg = KV_FOX, H_FOX // KV_FOX
    qg = q.reshape(B, T, G, Hg, HEAD_DIM)
    cqg = cq.reshape(B, T, G, Hg)
    ck_t = ck.reshape(B, n_k, G, Hg).transpose(0, 2, 3, 1)[:, :, :, None, :]
    k_pos = jnp.arange(n_k)

    def block(qb, cb, pb):
        bias = cb.transpose(0, 2, 3, 1)[..., None] - ck_t
        return gqa_attend(qb, k, v, k_pos[None, :] <= pb[:, None], bias)

    o = map_query_blocks(block, (qg, cqg), q_pos, Q_BLOCK)
    return o.reshape(B, T, W_FOX)


def nsa_compress(x, pe, w1, w2):
    B, n_k, G, d = x.shape
    n_cmp = (n_k - CMP_BLOCK) // CMP_STRIDE + 1
    idx = np.arange(n_cmp)[:, None] * CMP_STRIDE + np.arange(CMP_BLOCK)[None, :]
    blocks = x[:, idx] + pe[None, None, :, None, :]
    flat = blocks.transpose(0, 1, 3, 2, 4).reshape(B, n_cmp, G, CMP_BLOCK * d)
    return jax.nn.gelu(flat @ w1) @ w2


def nsa_compressed_and_select(qg, q_pos, kc, vc, pe, w1, w2):
    n_k = kc.shape[1]
    k_cmp = nsa_compress(kc, pe[0], w1[0], w2[0])
    v_cmp = nsa_compress(vc, pe[1], w1[1], w2[1])
    n_cmp = k_cmp.shape[1]
    ends = np.arange(n_cmp) * CMP_STRIDE + CMP_BLOCK - 1
    s = jnp.einsum('bqghd,bngd->bghqn', qg, k_cmp).astype(jnp.float32) * ATTN_SCALE
    p = masked_softmax(s, jnp.asarray(ends)[None, :] <= q_pos[:, None])
    o_cmp = jnp.einsum('bghqn,bngd->bqghd', p.astype(v_cmp.dtype), v_cmp)
    n_sel = -(-n_k // SEL_BLOCK)
    cs = np.arange(n_cmp) * CMP_STRIDE
    ss = np.arange(n_sel) * SEL_BLOCK
    cover = ((cs[:, None] < ss[None, :] + SEL_BLOCK) & (cs[:, None] + CMP_BLOCK > ss[None, :])).astype(np.float32)
    p_slc = jnp.einsum('bghqn,nj->bqgj', p, jnp.asarray(cover))
    cur = (q_pos // SEL_BLOCK)[:, None]
    jj = jnp.arange(n_sel)[None, :]
    valid = jj <= cur
    forced = (jj == 0) | (jj == cur) | (jj == cur - 1)
    score = jnp.where(valid[None, :, None, :], jnp.where(forced[None, :, None, :], jnp.inf, p_slc), -jnp.inf)
    vals, sel = lax.top_k(score, min(SEL_TOPK, n_sel))
    return o_cmp, sel, vals > -jnp.inf


def nsa_selected(qg, q_pos, sel, sel_valid, gather):
    def block(qb, sb, vb, pb):
        Bq, nq, G = qb.shape[:3]
        tok = sb[..., None] * SEL_BLOCK + jnp.arange(SEL_BLOCK)
        kk, vv = gather(tok)
        n_s = tok.shape[3] * SEL_BLOCK
        kk = kk.reshape(Bq, nq, G, n_s, HEAD_DIM)
        vv = vv.reshape(Bq, nq, G, n_s, HEAD_DIM)
        mask = ((tok <= pb[None, :, None, None, None]) & vb[..., None]).reshape(Bq, nq, G, n_s)
        mask = mask.transpose(0, 2, 1, 3)[:, :, None]
        s = jnp.einsum('bqghd,bqgsd->bghqs', qb, kk).astype(jnp.float32) * ATTN_SCALE
        p = masked_softmax(s, mask)
        return jnp.einsum('bghqs,bqgsd->bqghd', p.astype(vv.dtype), vv)

    return map_query_blocks(block, (qg, sel, sel_valid), q_pos, SEL_Q_BLOCK)


def window_core(qg, q_pos, k, v, k_pos):
    dlt = q_pos[:, None] - k_pos[None, :]
    mask = (dlt >= 0) & (dlt <= WINDOW) & (k_pos[None, :] >= 0)
    return gqa_attend(qg, k, v, mask)


def nsa_window_prompt(qg, q_pos, k, v):
    k_pad = jnp.pad(k, ((0, 0), (WINDOW, 0), (0, 0), (0, 0)))
    v_pad = jnp.pad(v, ((0, 0), (WINDOW, 0), (0, 0), (0, 0)))

    def block(qb, pb):
        n_keys = WINDOW + pb.shape[0]
        s0 = pb[0]
        kb = lax.dynamic_slice_in_dim(k_pad, s0, n_keys, axis=1)
        vb = lax.dynamic_slice_in_dim(v_pad, s0, n_keys, axis=1)
        return window_core(qb, pb, kb, vb, s0 - WINDOW + jnp.arange(n_keys))

    return map_query_blocks(block, (qg,), q_pos, Q_BLOCK)


def token_mixers(x, P, l, past):
    f32 = jnp.float32
    B, T, _ = x.shape
    past_len = 0 if past is None else past['page_table'].shape[1] * PAGE_SIZE
    q_pos = past_len + jnp.arange(T, dtype=jnp.int32)
    z = x @ P['w_in'][l] + P['b_in'][l]
    offs = np.cumsum((0,) + IN_SIZES)
    zq_d, zkv_d, zq_n, zkv_n, zg_n, zq_f, zkv_f, zf_f, zg_m = [z[..., offs[i]:offs[i + 1]] for i in range(len(IN_SIZES))]

    q_d = rope(zq_d.reshape(B, T, H_DIFF, 2, HEAD_DIM), q_pos)
    kv_d = zkv_d.reshape(B, T, 2, KV_DIFF, 2 * HEAD_DIM)
    k_d = rope(kv_d[:, :, 0].reshape(B, T, KV_DIFF, 2, HEAD_DIM), q_pos).reshape(B, T, KV_DIFF, 2 * HEAD_DIM)
    new_d = jnp.stack([k_d, kv_d[:, :, 1]], axis=2)
    if past is None:
        full_d = new_d
    else:
        full_d = jnp.concatenate([gather_pages(past['diff'], past['page_table']), new_d], axis=1)
    n_k = full_d.shape[1]
    lam_init = 0.8 - 0.6 * math.exp(-0.3 * l)
    dl = P['diff_lambda'][l].astype(f32)
    lam = jnp.exp(jnp.sum(dl[0] * dl[1])) - jnp.exp(jnp.sum(dl[2] * dl[3])) + lam_init
    o_d = diff_attention(q_d, full_d[:, :, 0].reshape(B, n_k, KV_DIFF, 2, HEAD_DIM), full_d[:, :, 1],
                         q_pos, lam, lam_init, P['diff_subln'][l]).astype(x.dtype)

    G, Hg = KV_NSA, H_NSA // KV_NSA
    q_n = rope(zq_n.reshape(B, T, H_NSA, HEAD_DIM), q_pos).reshape(B, T, G, Hg, HEAD_DIM)
    kv_n = zkv_n.reshape(B, T, 6, KV_NSA, HEAD_DIM)
    k_n = rope(kv_n[:, :, 0::2], q_pos)
    v_n = kv_n[:, :, 1::2]
    new_n = jnp.stack([k_n[:, :, 0], v_n[:, :, 0], k_n[:, :, 1], v_n[:, :, 1]], axis=2)
    new_w = jnp.stack([k_n[:, :, 2], v_n[:, :, 2]], axis=2)
    if past is None:
        full_c = new_n[:, :, :2]
    else:
        full_c = jnp.concatenate([gather_pages(past['nsa'][:, :, :2], past['page_table']), new_n[:, :, :2]], axis=1)
    o_cmp, sel, sel_valid = nsa_compressed_and_select(q_n, q_pos, full_c[:, :, 0], full_c[:, :, 1],
                                                      P['nsa_cmp_pe'][l], P['nsa_cmp_w1'][l], P['nsa_cmp_w2'][l])
    bi = jnp.arange(B).reshape(B, 1, 1, 1, 1)
    gi = jnp.arange(KV_NSA).reshape(1, 1, KV_NSA, 1, 1)
    k_s_new, v_s_new = new_n[:, :, 2], new_n[:, :, 3]
    if past is None:
        def gather_slc(tok):
            t = jnp.clip(tok, 0, T - 1)
            return k_s_new[bi, t, gi], v_s_new[bi, t, gi]
    else:
        pt = past['page_table']
        pool = past['nsa'].reshape((-1,) + past['nsa'].shape[2:])

        def gather_slc(tok):
            in_past = (tok < past_len)[..., None]
            tp = jnp.clip(tok, 0, past_len - 1)
            phys = pt[bi, tp // PAGE_SIZE] * PAGE_SIZE + tp % PAGE_SIZE
            tn = jnp.clip(tok - past_len, 0, T - 1)
            kk = jnp.where(in_past, pool[phys, 2, gi], k_s_new[bi, tn, gi])
            vv = jnp.where(in_past, pool[phys, 3, gi], v_s_new[bi, tn, gi])
            return kk, vv
    o_slc = nsa_selected(q_n, q_pos, sel, sel_valid, gather_slc)
    if past is None:
        o_win = nsa_window_prompt(q_n, q_pos, new_w[:, :, 0], new_w[:, :, 1])
        win_state = new_w[:, -min(WINDOW, T):]
    else:
        n_buf = past['win'].shape[1]
        band = jnp.concatenate([past['win'], new_w], axis=1)
        k_pos = past_len - n_buf + jnp.arange(band.shape[1])
        o_win = window_core(q_n, q_pos, band[:, :, 0], band[:, :, 1], k_pos)
        win_state = band[:, -n_buf:]
    g_n = jax.nn.sigmoid(zg_n.reshape(B, T, G, Hg, 3))
    o_n = (g_n[..., 0:1] * o_cmp + g_n[..., 1:2] * o_slc + g_n[..., 2:3] * o_win).reshape(B, T, W_NSA).astype(x.dtype)

    q_f = zq_f.reshape(B, T, H_FOX, HEAD_DIM)
    kv_f = zkv_f.reshape(B, T, 2, KV_FOX, HEAD_DIM)
    logf = jax.nn.log_sigmoid(zf_f.astype(f32))
    c_new = jnp.cumsum(logf, axis=1)
    if past is None:
        full_f = kv_f
        ck = c_new
    else:
        full_f = jnp.concatenate([gather_pages(past['fox'], past['page_table']), kv_f], axis=1)
        lf_past = gather_pages(past['logf'], past['page_table']).astype(f32)
        suffix = jnp.flip(jnp.cumsum(jnp.flip(lf_past, 1), axis=1), 1)
        ck = jnp.concatenate([-suffix[:, 1:], jnp.zeros_like(suffix[:, :1]), c_new], axis=1)
    o_f = fox_attention(q_f, full_f[:, :, 0], full_f[:, :, 1], c_new, ck, q_pos).astype(x.dtype)

    g_m = jax.nn.sigmoid(zg_m.reshape(B, T, 3, D_MODEL))
    merged = (g_m[:, :, 0] * (o_d @ P['w_branch_diff'][l])
              + g_m[:, :, 1] * (o_n @ P['w_branch_nsa'][l])
              + g_m[:, :, 2] * (o_f @ P['w_branch_fox'][l]))
    y = merged @ P['w_out'][l]
    return y, (new_d, new_n, win_state, kv_f, logf.astype(x.dtype))


def moe_ffn(x, w_router, b_router, w_gate_up, b_gate_up, w_down, b_down):
    B, T, D = x.shape
    n_tok = B * T
    xf = x.reshape(n_tok, D)
    logits = (xf @ w_router).astype(jnp.float32) + b_router.astype(jnp.float32)
    top_v, top_i = lax.top_k(logits, TOP_K)
    gates = jax.nn.softmax(top_v, axis=-1)
    n_assign = n_tok * TOP_K
    e_flat = top_i.reshape(-1)
    tok_flat = jnp.arange(n_assign, dtype=jnp.int32) // TOP_K
    order = jnp.argsort(e_flat)
    e_sorted = e_flat[order]
    blk = max(8, min(MOE_BLOCK, n_assign // N_EXPERTS))
    counts = jnp.bincount(e_flat, length=N_EXPERTS)
    padded = (counts + blk - 1) // blk * blk
    pad_end = jnp.cumsum(padded)
    start = jnp.cumsum(counts) - counts
    dest = (pad_end - padded)[e_sorted] + jnp.arange(n_assign) - start[e_sorted]
    n_blocks = -(-(n_assign + N_EXPERTS * (blk - 1)) // blk)
    row_tok = jnp.full((n_blocks * blk,), n_tok, jnp.int32).at[dest].set(tok_flat[order])
    row_gate = jnp.zeros((n_blocks * blk,), jnp.float32).at[dest].set(gates.reshape(-1)[order])
    blk_expert = jnp.minimum(jnp.searchsorted(pad_end, jnp.arange(n_blocks) * blk, side='right'), N_EXPERTS - 1)
    x_pad = jnp.concatenate([xf, jnp.zeros((1, D), xf.dtype)], axis=0)

    def body(acc, b):
        rows = lax.dynamic_slice_in_dim(row_tok, b * blk, blk)
        gw = lax.dynamic_slice_in_dim(row_gate, b * blk, blk)
        e = blk_expert[b]
        hgu = x_pad[rows] @ w_gate_up[e] + b_gate_up[e]
        g = jnp.minimum(hgu[:, :D_EXPERT], SWIGLU_LIMIT)
        u = jnp.clip(hgu[:, D_EXPERT:], -SWIGLU_LIMIT, SWIGLU_LIMIT)
        out = ((u + 1.0) * (g * jax.nn.sigmoid(SWIGLU_ALPHA * g))) @ w_down[e] + b_down[e]
        return acc.at[rows].add(out.astype(jnp.float32) * gw[:, None]), None

    acc, _ = lax.scan(body, jnp.zeros((n_tok + 1, D), jnp.float32), jnp.arange(n_blocks))
    return acc[:n_tok].reshape(B, T, D).astype(x.dtype)


def trunk_layer(x, P, l, past):
    y, st = token_mixers(x, P, l, past)
    h = layer_norm(DEEPNORM_ALPHA * x + y, P['ln1_g'][l], P['ln1_b'][l])
    f = moe_ffn(h, P['w_router'][l], P['b_router'][l], P['w_gate_up'][l], P['b_gate_up'][l],
                P['w_down'][l], P['b_down'][l])
    return layer_norm(DEEPNORM_ALPHA * h + f, P['ln2_g'][l], P['ln2_b'][l]), st


def setup_inputs(seed: int = 0) -> dict:
    key = jax.random.key(seed)
    ks = jax.random.split(key, 32)
    f32 = jnp.float32

    def nrm(k, shape, s=1.0):
        return s * jax.random.normal(k, shape, f32)

    n_pages = PAST_LEN // PAGE_SIZE
    n_used = DEC_BATCH * n_pages
    n_pool = n_used + (n_used + 3) // 4
    win_buf = min(WINDOW, PAST_LEN)
    x_prompt = nrm(ks[0], (BATCH, SEQ, D_MODEL))
    x_sample = nrm(ks[1], (DEC_BATCH, DEC_SEQ, D_MODEL))
    cache_diff_kv = nrm(ks[2], (DEPTH, n_pool, PAGE_SIZE, 2, KV_DIFF, 2 * HEAD_DIM))
    cache_nsa_kv = nrm(ks[3], (DEPTH, n_pool, PAGE_SIZE, 4, KV_NSA, HEAD_DIM))
    cache_nsa_win = nrm(ks[4], (DEPTH, DEC_BATCH, win_buf, 2, KV_NSA, HEAD_DIM))
    cache_fox_kv = nrm(ks[5], (DEPTH, n_pool, PAGE_SIZE, 2, KV_FOX, HEAD_DIM))
    cache_fox_logf = jax.nn.log_sigmoid(FORGET_BIAS_INIT + nrm(ks[6], (DEPTH, n_pool, PAGE_SIZE, H_FOX)))
    page_table = jax.random.permutation(ks[7], n_pool)[:n_used].reshape(DEC_BATCH, n_pages).astype(jnp.int32)

    offs = np.cumsum((0,) + IN_SIZES)
    col_scale = np.ones((N_IN,), np.float32)
    col_scale[offs[1] + KV_DIFF * 2 * HEAD_DIM:offs[2]] = DEEPNORM_BETA
    nb = KV_NSA * HEAD_DIM
    for j in (1, 3, 5):
        col_scale[offs[3] + j * nb:offs[3] + (j + 1) * nb] = DEEPNORM_BETA
    col_scale[offs[6] + KV_FOX * HEAD_DIM:offs[7]] = DEEPNORM_BETA
    f_bias = np.zeros((N_IN,), np.float32)
    f_bias[offs[7]:offs[8]] = FORGET_BIAS_INIT
    w_in = nrm(ks[8], (DEPTH, D_MODEL, N_IN), D_MODEL ** -0.5) * jnp.asarray(col_scale)
    b_in = nrm(ks[9], (DEPTH, N_IN), 0.02) + jnp.asarray(f_bias)
    diff_lambda = nrm(ks[10], (DEPTH, 4, HEAD_DIM), 0.1)
    diff_subln = 1.0 + nrm(ks[11], (DEPTH, 2 * HEAD_DIM), 0.02)
    nsa_cmp_pe = nrm(ks[12], (DEPTH, 2, CMP_BLOCK, HEAD_DIM), 0.02)
    nsa_cmp_w1 = nrm(ks[13], (DEPTH, 2, CMP_BLOCK * HEAD_DIM, HEAD_DIM), (CMP_BLOCK * HEAD_DIM) ** -0.5)
    nsa_cmp_w2 = nrm(ks[14], (DEPTH, 2, HEAD_DIM, HEAD_DIM), HEAD_DIM ** -0.5)
    w_branch_diff = nrm(ks[15], (DEPTH, W_DIFF, D_MODEL), W_DIFF ** -0.5 * DEEPNORM_BETA)
    w_branch_nsa = nrm(ks[16], (DEPTH, W_NSA, D_MODEL), W_NSA ** -0.5 * DEEPNORM_BETA)
    w_branch_fox = nrm(ks[17], (DEPTH, W_FOX, D_MODEL), W_FOX ** -0.5 * DEEPNORM_BETA)
    w_out = nrm(ks[18], (DEPTH, D_MODEL, D_MODEL), D_MODEL ** -0.5 * DEEPNORM_BETA)
    ln1_g = 1.0 + nrm(ks[19], (DEPTH, D_MODEL), 0.02)
    ln1_b = nrm(ks[20], (DEPTH, D_MODEL), 0.02)
    w_router = nrm(ks[21], (DEPTH, D_MODEL, N_EXPERTS), D_MODEL ** -0.5)
    b_router = nrm(ks[22], (DEPTH, N_EXPERTS), 0.01)
    w_gate_up = nrm(ks[23], (DEPTH, N_EXPERTS, D_MODEL, 2 * D_EXPERT), D_MODEL ** -0.5 * DEEPNORM_BETA)
    b_gate_up = nrm(ks[24], (DEPTH, N_EXPERTS, 2 * D_EXPERT), 0.02)
    w_down = nrm(ks[25], (DEPTH, N_EXPERTS, D_EXPERT, D_MODEL), D_EXPERT ** -0.5 * DEEPNORM_BETA)
    b_down = nrm(ks[26], (DEPTH, N_EXPERTS, D_MODEL), 0.02)
    ln2_g = 1.0 + nrm(ks[27], (DEPTH, D_MODEL), 0.02)
    ln2_b = nrm(ks[28], (DEPTH, D_MODEL), 0.02)
    return {'x_prompt': x_prompt, 'x_sample': x_sample, 'cache_diff_kv': cache_diff_kv,
            'cache_nsa_kv': cache_nsa_kv, 'cache_nsa_win': cache_nsa_win, 'cache_fox_kv': cache_fox_kv,
            'cache_fox_logf': cache_fox_logf, 'page_table': page_table, 'w_in': w_in, 'b_in': b_in,
            'diff_lambda': diff_lambda, 'diff_subln': diff_subln, 'nsa_cmp_pe': nsa_cmp_pe,
            'nsa_cmp_w1': nsa_cmp_w1, 'nsa_cmp_w2': nsa_cmp_w2, 'w_branch_diff': w_branch_diff,
            'w_branch_nsa': w_branch_nsa, 'w_branch_fox': w_branch_fox, 'w_out': w_out,
            'ln1_g': ln1_g, 'ln1_b': ln1_b, 'w_router': w_router, 'b_router': b_router,
            'w_gate_up': w_gate_up, 'b_gate_up': b_gate_up, 'w_down': w_down, 'b_down': b_down,
            'ln2_g': ln2_g, 'ln2_b': ln2_b}


def reference(x_prompt, x_sample, cache_diff_kv, cache_nsa_kv, cache_nsa_win, cache_fox_kv, cache_fox_logf,
              page_table, w_in, b_in, diff_lambda, diff_subln, nsa_cmp_pe, nsa_cmp_w1, nsa_cmp_w2,
              w_branch_diff, w_branch_nsa, w_branch_fox, w_out, ln1_g, ln1_b, w_router, b_router,
              w_gate_up, b_gate_up, w_down, b_down, ln2_g, ln2_b):
    P = dict(w_in=w_in, b_in=b_in, diff_lambda=diff_lambda, diff_subln=diff_subln, nsa_cmp_pe=nsa_cmp_pe,
             nsa_cmp_w1=nsa_cmp_w1, nsa_cmp_w2=nsa_cmp_w2, w_branch_diff=w_branch_diff,
             w_branch_nsa=w_branch_nsa, w_branch_fox=w_branch_fox, w_out=w_out, ln1_g=ln1_g, ln1_b=ln1_b,
             w_router=w_router, b_router=b_router, w_gate_up=w_gate_up, b_gate_up=b_gate_up,
             w_down=w_down, b_down=b_down, ln2_g=ln2_g, ln2_b=ln2_b)
    x = x_prompt
    st_p = []
    for l in range(DEPTH):
        x, st = trunk_layer(x, P, l, None)
        st_p.append(st)
    y_prompt = x
    x = x_sample
    st_s = []
    for l in range(DEPTH):
        past = dict(page_table=page_table, diff=cache_diff_kv[l], nsa=cache_nsa_kv[l], win=cache_nsa_win[l],
                    fox=cache_fox_kv[l], logf=cache_fox_logf[l])
        x, st = trunk_layer(x, P, l, past)
        st_s.append(st)
    y_sample = x
    new_diff_kv_p = jnp.stack([s[0] for s in st_p], 0)
    new_nsa_kv_p = jnp.stack([s[1] for s in st_p], 0)
    new_nsa_win_p = jnp.stack([s[2] for s in st_p], 0)
    new_fox_kv_p = jnp.stack([s[3] for s in st_p], 0)
    new_fox_logf_p = jnp.stack([s[4] for s in st_p], 0)
    new_diff_kv_s = jnp.stack([s[0] for s in st_s], 0)
    new_nsa_kv_s = jnp.stack([s[1] for s in st_s], 0)
    new_nsa_win_s = jnp.stack([s[2] for s in st_s], 0)
    new_fox_kv_s = jnp.stack([s[3] for s in st_s], 0)
    new_fox_logf_s = jnp.stack([s[4] for s in st_s], 0)
    return (y_prompt, y_sample, new_diff_kv_p, new_nsa_kv_p, new_nsa_win_p, new_fox_kv_p, new_fox_logf_p,
            new_diff_kv_s, new_nsa_kv_s, new_nsa_win_s, new_fox_kv_s, new_fox_logf_s)
```

```python
import functools
import math

import numpy as np
import jax
import jax.numpy as jnp
from jax import lax
from jax.experimental import pallas as pl
from jax.experimental.pallas import tpu as pltpu

D_MODEL = 4096
BATCH = 4
SEQ = 2048
DEPTH = 2
DEC_BATCH = 8
DEC_SEQ = 4
PAST_LEN = 16384
PAGE_SIZE = 128

HEAD_DIM = 128
ROPE_THETA = 10000.0
H_DIFF = 4
KV_DIFF = 2
H_NSA = 12
KV_NSA = 2
H_FOX = 12
KV_FOX = 4
W_DIFF = H_DIFF * 2 * HEAD_DIM
W_NSA = H_NSA * HEAD_DIM
W_FOX = H_FOX * HEAD_DIM
CMP_BLOCK = 32
CMP_STRIDE = 16
SEL_BLOCK = 64
SEL_TOPK = 16
WINDOW = 512
N_EXPERTS = 32
TOP_K = 4
D_EXPERT = D_MODEL // 2
SWIGLU_ALPHA = 1.702
SWIGLU_LIMIT = 7.0
LN_EPS = 1e-5
DEEPNORM_ALPHA = (2 * DEPTH) ** 0.25
ATTN_SCALE = HEAD_DIM ** -0.5
IN_SIZES = (
    H_DIFF * 2 * HEAD_DIM,
    2 * KV_DIFF * 2 * HEAD_DIM,
    H_NSA * HEAD_DIM,
    6 * KV_NSA * HEAD_DIM,
    3 * H_NSA,
    H_FOX * HEAD_DIM,
    2 * KV_FOX * HEAD_DIM,
    H_FOX,
    3 * D_MODEL,
)
N_IN = sum(IN_SIZES)
IN_OFFS = tuple(int(v) for v in np.cumsum((0,) + IN_SIZES))

F32 = jnp.float32
BF16 = jnp.bfloat16
NEG_BIG = -1e30

V7X_VMEM_LIMIT_BYTES = 48 * 1024 * 1024
Q_TILE = 128
MOE_TILE = 256


def _cparams(n_axes):
    return pltpu.CompilerParams(dimension_semantics=("arbitrary",) * n_axes,
                                vmem_limit_bytes=V7X_VMEM_LIMIT_BYTES)


def _cast_weight_tile(w_ref, wbf_ref, rows_per_chunk=256):
    n_chunks = w_ref.shape[0] // rows_per_chunk

    def body(c, carry):
        r = pl.multiple_of(c * rows_per_chunk, rows_per_chunk)
        wbf_ref[pl.ds(r, rows_per_chunk), :] = w_ref[pl.ds(r, rows_per_chunk), :].astype(BF16)
        return carry

    lax.fori_loop(0, n_chunks, body, 0)


def _mm_body(x_ref, w_ref, b_ref, o_ref, wbf_ref):
    @pl.when(pl.program_id(1) == 0)
    def _():
        _cast_weight_tile(w_ref, wbf_ref)

    acc = jnp.dot(x_ref[...], wbf_ref[...], preferred_element_type=F32)
    o_ref[...] = (acc + b_ref[...]).astype(o_ref.dtype)


def matmul_bias(x, w, b, *, tm, tn, out_dtype=F32, name="matmul_bias"):
    M, K = x.shape
    N = w.shape[1]
    tn = min(tn, N)
    tm = min(tm, M)
    return pl.pallas_call(
        _mm_body,
        grid=(pl.cdiv(N, tn), pl.cdiv(M, tm)),
        in_specs=[pl.BlockSpec((tm, K), lambda j, i: (i, 0)),
                  pl.BlockSpec((K, tn), lambda j, i: (0, j)),
                  pl.BlockSpec((1, tn), lambda j, i: (0, j))],
        out_specs=pl.BlockSpec((tm, tn), lambda j, i: (i, j)),
        out_shape=jax.ShapeDtypeStruct((M, N), out_dtype),
        scratch_shapes=[pltpu.VMEM((K, tn), BF16)],
        compiler_params=_cparams(2),
        name=name,
    )(x, w, b)


def _merge_body(od_ref, on_ref, of_ref, g_ref, wd_ref, wn_ref, wf_ref, o_ref, wdb, wnb, wfb):
    @pl.when(pl.program_id(1) == 0)
    def _():
        _cast_weight_tile(wd_ref, wdb)
        _cast_weight_tile(wn_ref, wnb)
        _cast_weight_tile(wf_ref, wfb)

    acc = jax.nn.sigmoid(g_ref[0]) * jnp.dot(od_ref[...], wdb[...], preferred_element_type=F32)
    acc += jax.nn.sigmoid(g_ref[1]) * jnp.dot(on_ref[...], wnb[...], preferred_element_type=F32)
    acc += jax.nn.sigmoid(g_ref[2]) * jnp.dot(of_ref[...], wfb[...], preferred_element_type=F32)
    o_ref[...] = acc.astype(o_ref.dtype)


def merge_branches(o_d, o_n, o_f, gates, w_d, w_n, w_f, *, tm=512, tn=512):
    M = o_d.shape[0]
    D = w_d.shape[1]
    tn = min(tn, D)
    return pl.pallas_call(
        _merge_body,
        grid=(D // tn, pl.cdiv(M, tm)),
        in_specs=[pl.BlockSpec((tm, W_DIFF), lambda j, i: (i, 0)),
                  pl.BlockSpec((tm, W_NSA), lambda j, i: (i, 0)),
                  pl.BlockSpec((tm, W_FOX), lambda j, i: (i, 0)),
                  pl.BlockSpec((3, tm, tn), lambda j, i: (0, i, j)),
                  pl.BlockSpec((W_DIFF, tn), lambda j, i: (0, j)),
                  pl.BlockSpec((W_NSA, tn), lambda j, i: (0, j)),
                  pl.BlockSpec((W_FOX, tn), lambda j, i: (0, j))],
        out_specs=pl.BlockSpec((tm, tn), lambda j, i: (i, j)),
        out_shape=jax.ShapeDtypeStruct((M, D), BF16),
        scratch_shapes=[pltpu.VMEM((W_DIFF, tn), BF16), pltpu.VMEM((W_NSA, tn), BF16),
                        pltpu.VMEM((W_FOX, tn), BF16)],
        compiler_params=_cparams(2),
        name="merge_branches",
    )(o_d, o_n, o_f, gates, w_d, w_n, w_f)


def _ln_body(x_ref, y_ref, g_ref, b_ref, o_ref, obf_ref):
    v = DEEPNORM_ALPHA * x_ref[...] + y_ref[...]
    mu = jnp.mean(v, -1, keepdims=True)
    c = v - mu
    var = jnp.mean(c * c, -1, keepdims=True)
    out = c * lax.rsqrt(var + LN_EPS) * g_ref[...] + b_ref[...]
    o_ref[...] = out
    obf_ref[...] = out.astype(BF16)


def residual_layer_norm(x, y, g, b, *, tm=256):
    M, D = x.shape
    return pl.pallas_call(
        _ln_body,
        grid=(pl.cdiv(M, tm),),
        in_specs=[pl.BlockSpec((tm, D), lambda i: (i, 0)),
                  pl.BlockSpec((tm, D), lambda i: (i, 0)),
                  pl.BlockSpec((1, D), lambda i: (0, 0)),
                  pl.BlockSpec((1, D), lambda i: (0, 0))],
        out_specs=[pl.BlockSpec((tm, D), lambda i: (i, 0)),
                   pl.BlockSpec((tm, D), lambda i: (i, 0))],
        out_shape=[jax.ShapeDtypeStruct((M, D), F32), jax.ShapeDtypeStruct((M, D), BF16)],
        compiler_params=_cparams(1),
        name="residual_layer_norm",
    )(x, y, g, b)


def _softmax_rows(s, mask):
    s = jnp.where(mask, s, NEG_BIG)
    m = jnp.max(s, -1, keepdims=True)
    p = jnp.where(mask, jnp.exp(s - m), 0.0)
    d = jnp.sum(p, -1, keepdims=True)
    return p * (1.0 / jnp.where(d > 0, d, 1.0))


def _scores(q, k):
    return lax.dot_general(q, k, (((1,), (1,)), ((), ())), preferred_element_type=F32) * ATTN_SCALE


def _diff_prompt_body(sc_ref, q_ref, k_ref, v_ref, g_ref, o_ref, *, hg, tq):
    qi = pl.program_id(2)
    T = k_ref.shape[1]
    lam = sc_ref[0]
    post = sc_ref[1]
    qpos = qi * tq + lax.broadcasted_iota(jnp.int32, (tq, T), 0)
    kpos = lax.broadcasted_iota(jnp.int32, (tq, T), 1)
    mask = (kpos <= qpos)[None]

    def probs(c):
        s = _scores(q_ref[c].reshape(hg * tq, HEAD_DIM), k_ref[c]).reshape(hg, tq, T)
        return _softmax_rows(s, mask)

    a = probs(0) - lam * probs(1)
    o = jnp.dot(a.astype(BF16).reshape(hg * tq, T), v_ref[...], preferred_element_type=F32)
    o = o * lax.rsqrt(jnp.mean(o * o, -1, keepdims=True) + LN_EPS) * g_ref[...] * post
    for h in range(hg):
        o_ref[:, h * 2 * HEAD_DIM:(h + 1) * 2 * HEAD_DIM] = o[h * tq:(h + 1) * tq].astype(o_ref.dtype)


def diff_attention_prompt(q, k, v, scalars, subln_g, *, tq=Q_TILE):
    B, G, _, hg, T, d = q.shape
    kern = functools.partial(_diff_prompt_body, hg=hg, tq=tq)
    return pl.pallas_call(
        kern,
        grid=(B, G, T // tq),
        in_specs=[pl.BlockSpec(memory_space=pltpu.SMEM),
                  pl.BlockSpec((None, None, 2, hg, tq, d), lambda b, g, i: (b, g, 0, 0, i, 0)),
                  pl.BlockSpec((None, None, 2, T, d), lambda b, g, i: (b, g, 0, 0, 0)),
                  pl.BlockSpec((None, None, T, 2 * d), lambda b, g, i: (b, g, 0, 0)),
                  pl.BlockSpec((1, 2 * d), lambda b, g, i: (0, 0))],
        out_specs=pl.BlockSpec((None, tq, hg * 2 * d), lambda b, g, i: (b, i, g)),
        out_shape=jax.ShapeDtypeStruct((B, T, W_DIFF), BF16),
        compiler_params=_cparams(3),
        name="diff_attention_prompt",
    )(scalars, q, k, v, subln_g)


def _fox_prompt_body(q_ref, k_ref, v_ref, cq_ref, ck_ref, o_ref, *, hg, tq):
    qi = pl.program_id(2)
    T = k_ref.shape[0]
    qpos = qi * tq + lax.broadcasted_iota(jnp.int32, (tq, T), 0)
    kpos = lax.broadcasted_iota(jnp.int32, (tq, T), 1)
    mask = (kpos <= qpos)[None]
    s = _scores(q_ref[...].reshape(hg * tq, HEAD_DIM), k_ref[...]).reshape(hg, tq, T)
    s = s + (cq_ref[...] - ck_ref[...])
    p = _softmax_rows(s, mask)
    o = jnp.dot(p.astype(BF16).reshape(hg * tq, T), v_ref[...], preferred_element_type=F32)
    for h in range(hg):
        o_ref[:, h * HEAD_DIM:(h + 1) * HEAD_DIM] = o[h * tq:(h + 1) * tq].astype(o_ref.dtype)


def fox_attention_prompt(q, k, v, c_col, c_row, *, tq=Q_TILE):
    B, G, hg, T, d = q.shape
    kern = functools.partial(_fox_prompt_body, hg=hg, tq=tq)
    return pl.pallas_call(
        kern,
        grid=(B, G, T // tq),
        in_specs=[pl.BlockSpec((None, None, hg, tq, d), lambda b, g, i: (b, g, 0, i, 0)),
                  pl.BlockSpec((None, None, T, d), lambda b, g, i: (b, g, 0, 0)),
                  pl.BlockSpec((None, None, T, d), lambda b, g, i: (b, g, 0, 0)),
                  pl.BlockSpec((None, None, hg, tq, 1), lambda b, g, i: (b, g, 0, i, 0)),
                  pl.BlockSpec((None, None, hg, 1, T), lambda b, g, i: (b, g, 0, 0, 0))],
        out_specs=pl.BlockSpec((None, tq, hg * d), lambda b, g, i: (b, i, g)),
        out_shape=jax.ShapeDtypeStruct((B, T, W_FOX), BF16),
        compiler_params=_cparams(3),
        name="fox_attention_prompt",
    )(q, k, v, c_col, c_row)


def _select_blocks(score_t, cur, n_sel):
    jj = lax.broadcasted_iota(jnp.int32, score_t.shape, 0)
    valid = jj <= cur
    forced = (jj == 0) | (jj == cur) | (jj == cur - 1)
    key = jnp.where(valid, jnp.where(forced, jnp.inf, score_t), -jnp.inf)
    rank = jnp.zeros(score_t.shape, F32)
    for i in range(n_sel):
        ki = key[i:i + 1, :]
        beats = (ki > key) | ((ki == key) & (jj > i))
        rank = rank + jnp.where(beats, 1.0, 0.0)
    return jnp.where(valid & (rank < float(min(SEL_TOPK, n_sel))), 1.0, 0.0)


def _nsa_prompt_body(q_ref, kc_ref, vc_ref, ks_ref, vs_ref, kw_ref, vw_ref, gate_ref, cover_ref,
                     expand_ref, o_ref, *, hg, tq, n_win):
    qi = pl.program_id(2)
    T = ks_ref.shape[0]
    n_cmp = kc_ref.shape[0]
    n_sel = cover_ref.shape[0]
    q = q_ref[...].reshape(hg * tq, HEAD_DIM)
    q0 = qi * tq

    s = _scores(q, kc_ref[...]).reshape(hg, tq, n_cmp)
    qpos_c = q0 + lax.broadcasted_iota(jnp.int32, (tq, n_cmp), 0)
    ends = lax.broadcasted_iota(jnp.int32, (tq, n_cmp), 1) * CMP_STRIDE + (CMP_BLOCK - 1)
    p_cmp = _softmax_rows(s, (ends <= qpos_c)[None])
    o_cmp = jnp.dot(p_cmp.astype(BF16).reshape(hg * tq, n_cmp), vc_ref[...], preferred_element_type=F32)

    p_sum = jnp.sum(p_cmp, axis=0)
    score_t = lax.dot_general(cover_ref[...], p_sum, (((1,), (1,)), ((), ())),
                              preferred_element_type=F32, precision=lax.Precision.HIGHEST)
    cur = (q0 + lax.broadcasted_iota(jnp.int32, (n_sel, tq), 1)) // SEL_BLOCK
    sel_t = _select_blocks(score_t, cur, n_sel)
    sel_keys = jnp.dot(sel_t.T.astype(BF16), expand_ref[...], preferred_element_type=F32)

    qpos = q0 + lax.broadcasted_iota(jnp.int32, (tq, T), 0)
    kpos = lax.broadcasted_iota(jnp.int32, (tq, T), 1)
    mask_slc = ((sel_keys > 0.5) & (kpos <= qpos))[None]
    s = _scores(q, ks_ref[...]).reshape(hg, tq, T)
    p_slc = _softmax_rows(s, mask_slc)
    o_slc = jnp.dot(p_slc.astype(BF16).reshape(hg * tq, T), vs_ref[...], preferred_element_type=F32)

    start = pl.multiple_of(jnp.maximum(q0 + tq - n_win, 0), tq)
    kw = kw_ref[pl.ds(start, n_win), :]
    vw = vw_ref[pl.ds(start, n_win), :]
    dlt = (q0 + lax.broadcasted_iota(jnp.int32, (tq, n_win), 0)) - (
        start + lax.broadcasted_iota(jnp.int32, (tq, n_win), 1))
    mask_win = ((dlt >= 0) & (dlt <= WINDOW))[None]
    s = _scores(q, kw).reshape(hg, tq, n_win)
    p_win = _softmax_rows(s, mask_win)
    o_win = jnp.dot(p_win.astype(BF16).reshape(hg * tq, n_win), vw, preferred_element_type=F32)

    gate = jax.nn.sigmoid(gate_ref[...])
    for h in range(hg):
        rows = slice(h * tq, (h + 1) * tq)
        o = (gate[:, 3 * h:3 * h + 1] * o_cmp[rows] + gate[:, 3 * h + 1:3 * h + 2] * o_slc[rows]
             + gate[:, 3 * h + 2:3 * h + 3] * o_win[rows])
        o_ref[:, h * HEAD_DIM:(h + 1) * HEAD_DIM] = o.astype(o_ref.dtype)


def nsa_attention_prompt(q, k_cmp, v_cmp, k_slc, v_slc, k_win, v_win, gates, *, tq=Q_TILE):
    B, G, hg, T, d = q.shape
    n_cmp_pad = k_cmp.shape[2]
    n_sel = -(-T // SEL_BLOCK)
    cs = np.arange(n_cmp_pad) * CMP_STRIDE
    ss = np.arange(n_sel) * SEL_BLOCK
    cover_t = ((cs[None, :] < ss[:, None] + SEL_BLOCK) & (cs[None, :] + CMP_BLOCK > ss[:, None]))
    cover_t = jnp.asarray(cover_t.astype(np.float32))
    expand = jnp.asarray((np.arange(T)[None, :] // SEL_BLOCK == np.arange(n_sel)[:, None]).astype(np.float32),
                         dtype=BF16)
    n_win = min(WINDOW + tq, T)
    kern = functools.partial(_nsa_prompt_body, hg=hg, tq=tq, n_win=n_win)
    full = lambda b, g, i: (b, g, 0, 0)
    return pl.pallas_call(
        kern,
        grid=(B, G, T // tq),
        in_specs=[pl.BlockSpec((None, None, hg, tq, d), lambda b, g, i: (b, g, 0, i, 0)),
                  pl.BlockSpec((None, None, n_cmp_pad, d), full),
                  pl.BlockSpec((None, None, n_cmp_pad, d), full),
                  pl.BlockSpec((None, None, T, d), full),
                  pl.BlockSpec((None, None, T, d), full),
                  pl.BlockSpec((None, None, T, d), full),
                  pl.BlockSpec((None, None, T, d), full),
                  pl.BlockSpec((None, None, tq, hg * 3), lambda b, g, i: (b, g, i, 0)),
                  pl.BlockSpec((n_sel, n_cmp_pad), lambda b, g, i: (0, 0)),
                  pl.BlockSpec((n_sel, T), lambda b, g, i: (0, 0))],
        out_specs=pl.BlockSpec((None, tq, hg * d), lambda b, g, i: (b, i, g)),
        out_shape=jax.ShapeDtypeStruct((B, T, W_NSA), BF16),
        compiler_params=_cparams(3),
        name="nsa_attention_prompt",
    )(q, k_cmp, v_cmp, k_slc, v_slc, k_win, v_win, gates, cover_t, expand)


def _expert_changed(te_ref, i):
    prev = te_ref[jnp.maximum(i - 1, 0)]
    return (i == 0) | (te_ref[i] != prev)


def _moe_up_body(te_ref, nu_ref, x_ref, wg_ref, wu_ref, bg_ref, bu_ref, o_ref, wgb, wub):
    i = pl.program_id(1)

    @pl.when(_expert_changed(te_ref, i))
    def _():
        _cast_weight_tile(wg_ref, wgb)
        _cast_weight_tile(wu_ref, wub)

    @pl.when(i < nu_ref[0])
    def _():
        x = x_ref[...]
        g = jnp.dot(x, wgb[...], preferred_element_type=F32) + bg_ref[...]
        u = jnp.dot(x, wub[...], preferred_element_type=F32) + bu_ref[...]
        g = jnp.minimum(g, SWIGLU_LIMIT)
        u = jnp.clip(u, -SWIGLU_LIMIT, SWIGLU_LIMIT)
        o_ref[...] = ((u + 1.0) * (g * jax.nn.sigmoid(SWIGLU_ALPHA * g))).astype(o_ref.dtype)


def moe_up(tile_expert, n_used, x_sorted, w_gate_up, b_gate_up, *, tn=256):
    R, D = x_sorted.shape
    de = w_gate_up.shape[2] // 2
    tn = min(tn, de)
    nj = de // tn
    tm = MOE_TILE
    grid_spec = pltpu.PrefetchScalarGridSpec(
        num_scalar_prefetch=2,
        grid=(nj, R // tm),
        in_specs=[pl.BlockSpec((tm, D), lambda j, i, te, nu: (i, 0)),
                  pl.BlockSpec((None, D, tn), lambda j, i, te, nu: (te[i], 0, j)),
                  pl.BlockSpec((None, D, tn), lambda j, i, te, nu: (te[i], 0, j + nj)),
                  pl.BlockSpec((None, 1, tn), lambda j, i, te, nu: (te[i], 0, j)),
                  pl.BlockSpec((None, 1, tn), lambda j, i, te, nu: (te[i], 0, j + nj))],
        out_specs=pl.BlockSpec((tm, tn), lambda j, i, te, nu: (i, j)),
        scratch_shapes=[pltpu.VMEM((D, tn), BF16), pltpu.VMEM((D, tn), BF16)],
    )
    return pl.pallas_call(
        _moe_up_body,
        grid_spec=grid_spec,
        out_shape=jax.ShapeDtypeStruct((R, de), BF16),
        compiler_params=_cparams(2),
        name="moe_up",
    )(tile_expert, n_used, x_sorted, w_gate_up, w_gate_up, b_gate_up, b_gate_up)


def _moe_down_body(te_ref, nu_ref, a_ref, w_ref, b_ref, o_ref, wb):
    i = pl.program_id(1)

    @pl.when(_expert_changed(te_ref, i))
    def _():
        _cast_weight_tile(w_ref, wb)

    @pl.when(i < nu_ref[0])
    def _():
        o_ref[...] = jnp.dot(a_ref[...], wb[...], preferred_element_type=F32) + b_ref[...]


def moe_down(tile_expert, n_used, act, w_down, b_down, *, tn=512):
    R, de = act.shape
    D = w_down.shape[2]
    tn = min(tn, D)
    tm = MOE_TILE
    grid_spec = pltpu.PrefetchScalarGridSpec(
        num_scalar_prefetch=2,
        grid=(D // tn, R // tm),
        in_specs=[pl.BlockSpec((tm, de), lambda j, i, te, nu: (i, 0)),
                  pl.BlockSpec((None, de, tn), lambda j, i, te, nu: (te[i], 0, j)),
                  pl.BlockSpec((None, 1, tn), lambda j, i, te, nu: (te[i], 0, j))],
        out_specs=pl.BlockSpec((tm, tn), lambda j, i, te, nu: (i, j)),
        scratch_shapes=[pltpu.VMEM((de, tn), BF16)],
    )
    return pl.pallas_call(
        _moe_down_body,
        grid_spec=grid_spec,
        out_shape=jax.ShapeDtypeStruct((R, D), F32),
        compiler_params=_cparams(2),
        name="moe_down",
    )(tile_expert, n_used, act, w_down, b_down)


def moe_ffn(h, h_bf, w_router, b_router, w_gate_up, b_gate_up, w_down, b_down):
    n_tok, D = h.shape
    E = w_router.shape[1]
    tm = MOE_TILE
    logits = matmul_bias(h_bf, w_router, b_router[None, :], tm=512, tn=E, name="router")
    top_v, top_i = lax.top_k(logits, TOP_K)
    gates = jax.nn.softmax(top_v, axis=-1)
    n_assign = n_tok * TOP_K
    e_flat = top_i.reshape(-1)
    onehot = (e_flat[:, None] == jnp.arange(E, dtype=e_flat.dtype)[None, :]).astype(jnp.int32)
    within = jnp.sum((jnp.cumsum(onehot, axis=0) - 1) * onehot, axis=1)
    counts = jnp.sum(onehot, axis=0)
    padded = (counts + tm - 1) // tm * tm
    pad_end = jnp.cumsum(padded)
    dest = (pad_end - padded)[e_flat] + within
    n_tiles = -(-(n_assign + E * (tm - 1)) // tm)
    tok_flat = jnp.arange(n_assign, dtype=jnp.int32) // TOP_K
    row_tok = jnp.full((n_tiles * tm,), n_tok, jnp.int32).at[dest].set(tok_flat)
    tile_expert = jnp.minimum(jnp.searchsorted(pad_end, jnp.arange(n_tiles) * tm, side='right'),
                              E - 1).astype(jnp.int32)
    n_used = (pad_end[-1] // tm).astype(jnp.int32).reshape(1)
    x_pad = jnp.concatenate([h_bf, jnp.zeros((1, D), BF16)], axis=0)
    x_sorted = x_pad[row_tok]
    act = moe_up(tile_expert, n_used, x_sorted, w_gate_up, b_gate_up[:, None, :])
    y = moe_down(tile_expert, n_used, act, w_down, b_down[:, None, :])
    picked = y[dest.reshape(n_tok, TOP_K)]
    return jnp.sum(picked * gates[:, :, None], axis=1)


def rope(x, pos):
    half = HEAD_DIM // 2
    inv = ROPE_THETA ** (-jnp.arange(half, dtype=F32) / half)
    ang = pos.astype(F32)[:, None] * inv[None, :]
    shp = (1, pos.shape[0]) + (1,) * (x.ndim - 3) + (half,)
    cos = jnp.cos(ang).reshape(shp)
    sin = jnp.sin(ang).reshape(shp)
    x1, x2 = x[..., :half], x[..., half:]
    return jnp.concatenate([x1 * cos - x2 * sin, x2 * cos + x1 * sin], -1)


def _split_z(z):
    return [z[..., IN_OFFS[i]:IN_OFFS[i + 1]] for i in range(len(IN_SIZES))]


def _diff_scalars(diff_lambda_l, l):
    lam_init = 0.8 - 0.6 * math.exp(-0.3 * l)
    dl = diff_lambda_l.astype(F32)
    lam = jnp.exp(jnp.sum(dl[0] * dl[1])) - jnp.exp(jnp.sum(dl[2] * dl[3])) + lam_init
    return lam, lam_init


def nsa_compress(x, pe, w1, w2):
    B, n_k, G, d = x.shape
    n_cmp = (n_k - CMP_BLOCK) // CMP_STRIDE + 1
    idx = np.arange(n_cmp)[:, None] * CMP_STRIDE + np.arange(CMP_BLOCK)[None, :]
    blocks = x[:, idx] + pe[None, None, :, None, :]
    flat = blocks.transpose(0, 1, 3, 2, 4).reshape(B, n_cmp, G, CMP_BLOCK * d)
    return jax.nn.gelu(flat @ w1) @ w2


def prompt_mixers(z, P, l):
    B, T, _ = z.shape
    q_pos = jnp.arange(T, dtype=jnp.int32)
    zq_d, zkv_d, zq_n, zkv_n, zg_n, zq_f, zkv_f, zf_f, _ = _split_z(z)

    hg_d = H_DIFF // KV_DIFF
    q_d = rope(zq_d.reshape(B, T, H_DIFF, 2, HEAD_DIM), q_pos)
    kv_d = zkv_d.reshape(B, T, 2, KV_DIFF, 2 * HEAD_DIM)
    k_d = rope(kv_d[:, :, 0].reshape(B, T, KV_DIFF, 2, HEAD_DIM), q_pos)
    new_d = jnp.stack([k_d.reshape(B, T, KV_DIFF, 2 * HEAD_DIM), kv_d[:, :, 1]], axis=2)
    lam, lam_init = _diff_scalars(P['diff_lambda'][l], l)
    scalars = jnp.stack([lam, jnp.asarray(1.0 - lam_init, F32)]).astype(F32)
    q_dk = q_d.reshape(B, T, KV_DIFF, hg_d, 2, HEAD_DIM).transpose(0, 2, 4, 3, 1, 5).astype(BF16)
    k_dk = k_d.transpose(0, 2, 3, 1, 4).astype(BF16)
    v_dk = kv_d[:, :, 1].transpose(0, 2, 1, 3).astype(BF16)
    o_d = diff_attention_prompt(q_dk, k_dk, v_dk, scalars, P['diff_subln'][l][None, :])

    G, hg_n = KV_NSA, H_NSA // KV_NSA
    q_n = rope(zq_n.reshape(B, T, H_NSA, HEAD_DIM), q_pos)
    kv_n = zkv_n.reshape(B, T, 6, KV_NSA, HEAD_DIM)
    k_n = rope(kv_n[:, :, 0::2], q_pos)
    v_n = kv_n[:, :, 1::2]
    new_n = jnp.stack([k_n[:, :, 0], v_n[:, :, 0], k_n[:, :, 1], v_n[:, :, 1]], axis=2)
    new_w = jnp.stack([k_n[:, :, 2], v_n[:, :, 2]], axis=2)
    k_cmp = nsa_compress(k_n[:, :, 0], P['nsa_cmp_pe'][l][0], P['nsa_cmp_w1'][l][0], P['nsa_cmp_w2'][l][0])
    v_cmp = nsa_compress(v_n[:, :, 0], P['nsa_cmp_pe'][l][1], P['nsa_cmp_w1'][l][1], P['nsa_cmp_w2'][l][1])
    n_cmp = k_cmp.shape[1]
    n_cmp_pad = -(-n_cmp // 128) * 128
    pad = ((0, 0), (0, n_cmp_pad - n_cmp), (0, 0), (0, 0))
    to_kernel = lambda a: a.transpose(0, 2, 1, 3).astype(BF16)
    q_nk = q_n.reshape(B, T, G, hg_n, HEAD_DIM).transpose(0, 2, 3, 1, 4).astype(BF16)
    gates_n = zg_n.reshape(B, T, G, hg_n * 3).transpose(0, 2, 1, 3)
    o_n = nsa_attention_prompt(q_nk, to_kernel(jnp.pad(k_cmp, pad)), to_kernel(jnp.pad(v_cmp, pad)),
                               to_kernel(k_n[:, :, 1]), to_kernel(v_n[:, :, 1]),
                               to_kernel(k_n[:, :, 2]), to_kernel(v_n[:, :, 2]), gates_n)
    win_state = new_w[:, -min(WINDOW, T):]

    Gf, hg_f = KV_FOX, H_FOX // KV_FOX
    kv_f = zkv_f.reshape(B, T, 2, KV_FOX, HEAD_DIM)
    logf = jax.nn.log_sigmoid(zf_f)
    c = jnp.cumsum(logf, axis=1).reshape(B, T, Gf, hg_f).transpose(0, 2, 3, 1)
    q_fk = zq_f.reshape(B, T, Gf, hg_f, HEAD_DIM).transpose(0, 2, 3, 1, 4).astype(BF16)
    o_f = fox_attention_prompt(q_fk, to_kernel(kv_f[:, :, 0]), to_kernel(kv_f[:, :, 1]),
                               c[..., None], c[:, :, :, None, :])
    outs = (o_d.reshape(B * T, W_DIFF), o_n.reshape(B * T, W_NSA), o_f.reshape(B * T, W_FOX))
    return outs, (new_d, new_n, win_state, kv_f, logf)


def _masked_softmax(s, mask):
    s = jnp.where(mask, s, -jnp.inf)
    m = jnp.max(s, -1, keepdims=True)
    m = jnp.where(jnp.isfinite(m), m, 0.0)
    p = jnp.exp(s - m)
    d = jnp.sum(p, -1, keepdims=True)
    return p / jnp.where(d > 0, d, 1.0)


def _gqa_attend(qg, k, v, mask, bias=None):
    s = jnp.einsum('bqghd,bsgd->bghqs', qg, k).astype(F32) * ATTN_SCALE
    if bias is not None:
        s = s + bias
    p = _masked_softmax(s, mask)
    return jnp.einsum('bghqs,bsgd->bqghd', p.astype(v.dtype), v)


def _gather_pages(pool, page_table):
    g = pool[page_table]
    return g.reshape((page_table.shape[0], page_table.shape[1] * pool.shape[1]) + pool.shape[2:])


def sample_mixers(z, P, l, past):
    B, T, _ = z.shape
    pt = past['page_table']
    past_len = pt.shape[1] * PAGE_SIZE
    q_pos = past_len + jnp.arange(T, dtype=jnp.int32)
    zq_d, zkv_d, zq_n, zkv_n, zg_n, zq_f, zkv_f, zf_f, _ = _split_z(z)

    q_d = rope(zq_d.reshape(B, T, H_DIFF, 2, HEAD_DIM), q_pos)
    kv_d = zkv_d.reshape(B, T, 2, KV_DIFF, 2 * HEAD_DIM)
    k_d = rope(kv_d[:, :, 0].reshape(B, T, KV_DIFF, 2, HEAD_DIM), q_pos).reshape(B, T, KV_DIFF, 2 * HEAD_DIM)
    new_d = jnp.stack([k_d, kv_d[:, :, 1]], axis=2)
    full_d = jnp.concatenate([_gather_pages(past['diff'], pt), new_d], axis=1)
    n_k = full_d.shape[1]
    lam, lam_init = _diff_scalars(P['diff_lambda'][l], l)
    qg = q_d.reshape(B, T, KV_DIFF, H_DIFF // KV_DIFF, 2, HEAD_DIM)
    kd = full_d[:, :, 0].reshape(B, n_k, KV_DIFF, 2, HEAD_DIM)
    vd = full_d[:, :, 1]
    k_pos = jnp.arange(n_k)
    s = jnp.einsum('bqghcd,bsgcd->cbghqs', qg, kd).astype(F32) * ATTN_SCALE
    p = _masked_softmax(s, k_pos[None, :] <= q_pos[:, None])
    a = p[0] - lam * p[1]
    o = jnp.einsum('bghqs,bsgd->bqghd', a, vd).astype(F32)
    o = o * lax.rsqrt(jnp.mean(jnp.square(o), -1, keepdims=True) + LN_EPS) * P['diff_subln'][l] * (1.0 - lam_init)
    o_d = o.reshape(B, T, W_DIFF)

    G, Hg = KV_NSA, H_NSA // KV_NSA
    q_n = rope(zq_n.reshape(B, T, H_NSA, HEAD_DIM), q_pos).reshape(B, T, G, Hg, HEAD_DIM)
    kv_n = zkv_n.reshape(B, T, 6, KV_NSA, HEAD_DIM)
    k_n = rope(kv_n[:, :, 0::2], q_pos)
    v_n = kv_n[:, :, 1::2]
    new_n = jnp.stack([k_n[:, :, 0], v_n[:, :, 0], k_n[:, :, 1], v_n[:, :, 1]], axis=2)
    new_w = jnp.stack([k_n[:, :, 2], v_n[:, :, 2]], axis=2)
    full_c = jnp.concatenate([_gather_pages(past['nsa'][:, :, :2], pt), new_n[:, :, :2]], axis=1)
    n_kc = full_c.shape[1]
    k_cmp = nsa_compress(full_c[:, :, 0], P['nsa_cmp_pe'][l][0], P['nsa_cmp_w1'][l][0], P['nsa_cmp_w2'][l][0])
    v_cmp = nsa_compress(full_c[:, :, 1], P['nsa_cmp_pe'][l][1], P['nsa_cmp_w1'][l][1], P['nsa_cmp_w2'][l][1])
    n_cmp = k_cmp.shape[1]
    ends = np.arange(n_cmp) * CMP_STRIDE + CMP_BLOCK - 1
    s = jnp.einsum('bqghd,bngd->bghqn', q_n, k_cmp).astype(F32) * ATTN_SCALE
    p = _masked_softmax(s, jnp.asarray(ends)[None, :] <= q_pos[:, None])
    o_cmp = jnp.einsum('bghqn,bngd->bqghd', p, v_cmp)
    n_sel = -(-n_kc // SEL_BLOCK)
    cs = np.arange(n_cmp) * CMP_STRIDE
    ss = np.arange(n_sel) * SEL_BLOCK
    cover = ((cs[:, None] < ss[None, :] + SEL_BLOCK) & (cs[:, None] + CMP_BLOCK > ss[None, :])).astype(np.float32)
    p_slc = jnp.einsum('bghqn,nj->bqgj', p, jnp.asarray(cover))
    cur = (q_pos // SEL_BLOCK)[:, None]
    jj = jnp.arange(n_sel)[None, :]
    valid = jj <= cur
    forced = (jj == 0) | (jj == cur) | (jj == cur - 1)
    score = jnp.where(valid[None, :, None, :], jnp.where(forced[None, :, None, :], jnp.inf, p_slc), -jnp.inf)
    vals, sel = lax.top_k(score, min(SEL_TOPK, n_sel))
    sel_valid = vals > -jnp.inf
    bi = jnp.arange(B).reshape(B, 1, 1, 1, 1)
    gi = jnp.arange(KV_NSA).reshape(1, 1, KV_NSA, 1, 1)
    pool = past['nsa'].reshape((-1,) + past['nsa'].shape[2:])
    tok = sel[..., None] * SEL_BLOCK + jnp.arange(SEL_BLOCK)
    in_past = (tok < past_len)[..., None]
    tp = jnp.clip(tok, 0, past_len - 1)
    phys = pt[bi, tp // PAGE_SIZE] * PAGE_SIZE + tp % PAGE_SIZE
    tn = jnp.clip(tok - past_len, 0, T - 1)
    kk = jnp.where(in_past, pool[phys, 2, gi], new_n[:, :, 2][bi, tn, gi])
    vv = jnp.where(in_past, pool[phys, 3, gi], new_n[:, :, 3][bi, tn, gi])
    n_s = tok.shape[3] * SEL_BLOCK
    kk = kk.reshape(B, T, G, n_s, HEAD_DIM)
    vv = vv.reshape(B, T, G, n_s, HEAD_DIM)
    mask = ((tok <= q_pos[None, :, None, None, None]) & sel_valid[..., None]).reshape(B, T, G, n_s)
    mask = mask.transpose(0, 2, 1, 3)[:, :, None]
    s = jnp.einsum('bqghd,bqgsd->bghqs', q_n, kk).astype(F32) * ATTN_SCALE
    p = _masked_softmax(s, mask)
    o_slc = jnp.einsum('bghqs,bqgsd->bqghd', p, vv)
    n_buf = past['win'].shape[1]
    band = jnp.concatenate([past['win'], new_w], axis=1)
    k_pos_w = past_len - n_buf + jnp.arange(band.shape[1])
    dlt = q_pos[:, None] - k_pos_w[None, :]
    o_win = _gqa_attend(q_n, band[:, :, 0], band[:, :, 1], (dlt >= 0) & (dlt <= WINDOW) & (k_pos_w[None, :] >= 0))
    win_state = band[:, -n_buf:]
    g_n = jax.nn.sigmoid(zg_n.reshape(B, T, G, Hg, 3))
    o_n = (g_n[..., 0:1] * o_cmp + g_n[..., 1:2] * o_slc + g_n[..., 2:3] * o_win).reshape(B, T, W_NSA)

    Gf, Hf = KV_FOX, H_FOX // KV_FOX
    q_f = zq_f.reshape(B, T, Gf, Hf, HEAD_DIM)
    kv_f = zkv_f.reshape(B, T, 2, KV_FOX, HEAD_DIM)
    logf = jax.nn.log_sigmoid(zf_f)
    c_new = jnp.cumsum(logf, axis=1)
    full_f = jnp.concatenate([_gather_pages(past['fox'], pt), kv_f], axis=1)
    lf_past = _gather_pages(past['logf'], pt).astype(F32)
    suffix = jnp.flip(jnp.cumsum(jnp.flip(lf_past, 1), axis=1), 1)
    ck = jnp.concatenate([-suffix[:, 1:], jnp.zeros_like(suffix[:, :1]), c_new], axis=1)
    n_kf = full_f.shape[1]
    cqg = c_new.reshape(B, T, Gf, Hf)
    ck_t = ck.reshape(B, n_kf, Gf, Hf).transpose(0, 2, 3, 1)[:, :, :, None, :]
    bias = cqg.transpose(0, 2, 3, 1)[..., None] - ck_t
    o_f = _gqa_attend(q_f, full_f[:, :, 0], full_f[:, :, 1], jnp.arange(n_kf)[None, :] <= q_pos[:, None], bias)
    outs = (o_d.reshape(B * T, W_DIFF).astype(BF16), o_n.reshape(B * T, W_NSA).astype(BF16),
            o_f.reshape(B * T, W_FOX).astype(BF16))
    return outs, (new_d, new_n, win_state, kv_f, logf)


def kernel(x_prompt, x_sample, cache_diff_kv, cache_nsa_kv, cache_nsa_win, cache_fox_kv, cache_fox_logf, page_table, w_in, b_in, diff_lambda, diff_subln, nsa_cmp_pe, nsa_cmp_w1, nsa_cmp_w2, w_branch_diff, w_branch_nsa, w_branch_fox, w_out, ln1_g, ln1_b, w_router, b_router, w_gate_up, b_gate_up, w_down, b_down, ln2_g, ln2_b):
    P = dict(diff_lambda=diff_lambda, diff_subln=diff_subln, nsa_cmp_pe=nsa_cmp_pe,
             nsa_cmp_w1=nsa_cmp_w1, nsa_cmp_w2=nsa_cmp_w2)
    B, T, D = x_prompt.shape
    Bs, Ts, _ = x_sample.shape
    n_p = B * T
    x = jnp.concatenate([x_prompt.reshape(n_p, D), x_sample.reshape(Bs * Ts, D)], axis=0)
    x_bf = x.astype(BF16)
    zero_bias = jnp.zeros((1, D), F32)
    st_p, st_s = [], []
    for l in range(DEPTH):
        z = matmul_bias(x_bf, w_in[l], b_in[l][None, :], tm=512, tn=512, name="in_proj")
        outs_p, st = prompt_mixers(z[:n_p].reshape(B, T, N_IN), P, l)
        st_p.append(st)
        past = dict(page_table=page_table, diff=cache_diff_kv[l], nsa=cache_nsa_kv[l], win=cache_nsa_win[l],
                    fox=cache_fox_kv[l], logf=cache_fox_logf[l])
        outs_s, st = sample_mixers(z[n_p:].reshape(Bs, Ts, N_IN), P, l, past)
        st_s.append(st)
        o_d, o_n, o_f = (jnp.concatenate([a, b], axis=0) for a, b in zip(outs_p, outs_s))
        gates = z[:, IN_OFFS[8]:].reshape(-1, 3, D).transpose(1, 0, 2)
        merged = merge_branches(o_d, o_n, o_f, gates, w_branch_diff[l], w_branch_nsa[l], w_branch_fox[l])
        y = matmul_bias(merged, w_out[l], zero_bias, tm=512, tn=512, name="out_proj")
        h, h_bf = residual_layer_norm(x, y, ln1_g[l][None, :], ln1_b[l][None, :])
        f = moe_ffn(h, h_bf, w_router[l], b_router[l], w_gate_up[l], b_gate_up[l], w_down[l], b_down[l])
        x, x_bf = residual_layer_norm(h, f, ln2_g[l][None, :], ln2_b[l][None, :])
    y_prompt = x[:n_p].reshape(B, T, D)
    y_sample = x[n_p:].reshape(Bs, Ts, D)
    stack = lambda sts, k: jnp.stack([s[k] for s in sts], 0)
    return (y_prompt, y_sample,
            stack(st_p, 0), stack(st_p, 1), stack(st_p, 2), stack(st_p, 3), stack(st_p, 4),
            stack(st_s, 0), stack(st_s, 1), stack(st_s, 2), stack(st_s, 3), stack(st_s, 4))
```

```python
import functools
import math

import numpy as np
import jax
import jax.numpy as jnp
from jax import lax
from jax.experimental import pallas as pl
from jax.experimental.pallas import tpu as pltpu

D_MODEL = 4096
BATCH = 4
SEQ = 2048
DEPTH = 2
DEC_BATCH = 8
DEC_SEQ = 4
PAST_LEN = 16384
PAGE_SIZE = 128

HEAD_DIM = 128
ROPE_THETA = 10000.0
H_DIFF = 4
KV_DIFF = 2
H_NSA = 12
KV_NSA = 2
H_FOX = 12
KV_FOX = 4
W_DIFF = H_DIFF * 2 * HEAD_DIM
W_NSA = H_NSA * HEAD_DIM
W_FOX = H_FOX * HEAD_DIM
CMP_BLOCK = 32
CMP_STRIDE = 16
SEL_BLOCK = 64
SEL_TOPK = 16
WINDOW = 512
N_EXPERTS = 32
TOP_K = 4
D_EXPERT = D_MODEL // 2
SWIGLU_ALPHA = 1.702
SWIGLU_LIMIT = 7.0
LN_EPS = 1e-5
DEEPNORM_ALPHA = (2 * DEPTH) ** 0.25
ATTN_SCALE = HEAD_DIM ** -0.5
IN_SIZES = (
    H_DIFF * 2 * HEAD_DIM,
    2 * KV_DIFF * 2 * HEAD_DIM,
    H_NSA * HEAD_DIM,
    6 * KV_NSA * HEAD_DIM,
    3 * H_NSA,
    H_FOX * HEAD_DIM,
    2 * KV_FOX * HEAD_DIM,
    H_FOX,
    3 * D_MODEL,
)
N_IN = sum(IN_SIZES)
IN_OFFS = tuple(int(v) for v in np.cumsum((0,) + IN_SIZES))
N_PAGES = PAST_LEN // PAGE_SIZE

F32 = jnp.float32
BF16 = jnp.bfloat16
NEG_BIG = -1e30
LANES = 128
SUBLANES = 8

V7X_VMEM_LIMIT_BYTES = 48 * 1024 * 1024
Q_TILE = 128
MOE_TILE = 256
TOK_PAD = SUBLANES
NEW_PAD = LANES
PAGES_PER_STEP = 8
CMP_PAGES_PER_STEP = 16

assert DEC_SEQ <= TOK_PAD and HEAD_DIM == LANES and PAGE_SIZE % SEL_BLOCK == 0
assert N_PAGES % PAGES_PER_STEP == 0 and N_PAGES % CMP_PAGES_PER_STEP == 0
assert CMP_BLOCK == 2 * CMP_STRIDE and PAGE_SIZE % CMP_STRIDE == 0 and SEQ % CMP_STRIDE == 0
assert ((PAST_LEN + DEC_SEQ - CMP_BLOCK) // CMP_STRIDE) * CMP_STRIDE + CMP_BLOCK <= PAST_LEN
assert PAST_LEN % SEL_BLOCK == 0 and DEC_SEQ <= SEL_BLOCK and WINDOW <= PAST_LEN and SEQ >= WINDOW


def _cparams(n_axes):
    return pltpu.CompilerParams(dimension_semantics=("arbitrary",) * n_axes,
                                vmem_limit_bytes=V7X_VMEM_LIMIT_BYTES)


def _cast_weight_tile(w_ref, wbf_ref, rows_per_chunk=256):
    n_chunks = w_ref.shape[0] // rows_per_chunk

    def body(c, carry):
        r = pl.multiple_of(c * rows_per_chunk, rows_per_chunk)
        wbf_ref[pl.ds(r, rows_per_chunk), :] = w_ref[pl.ds(r, rows_per_chunk), :].astype(BF16)
        return carry

    lax.fori_loop(0, n_chunks, body, 0)


def _mm_body(x_ref, w_ref, b_ref, o_ref, wbf_ref):
    @pl.when(pl.program_id(1) == 0)
    def _():
        _cast_weight_tile(w_ref, wbf_ref)

    acc = jnp.dot(x_ref[...], wbf_ref[...], preferred_element_type=F32)
    o_ref[...] = (acc + b_ref[...]).astype(o_ref.dtype)


def matmul_bias(x, w, b, *, tm, tn, out_dtype=F32, name="matmul_bias"):
    M, K = x.shape
    N = w.shape[1]
    tn = min(tn, N)
    tm = min(tm, M)
    return pl.pallas_call(
        _mm_body,
        grid=(pl.cdiv(N, tn), pl.cdiv(M, tm)),
        in_specs=[pl.BlockSpec((tm, K), lambda j, i: (i, 0)),
                  pl.BlockSpec((K, tn), lambda j, i: (0, j)),
                  pl.BlockSpec((1, tn), lambda j, i: (0, j))],
        out_specs=pl.BlockSpec((tm, tn), lambda j, i: (i, j)),
        out_shape=jax.ShapeDtypeStruct((M, N), out_dtype),
        scratch_shapes=[pltpu.VMEM((K, tn), BF16)],
        compiler_params=_cparams(2),
        name=name,
    )(x, w, b)


def _proj_body(x_ref, w_ref, b_ref, flag_ref, cos_ref, sin_ref, *rest, with_f32, use_rope):
    if with_f32:
        of_ref, ob_ref, wbf_ref = rest
    else:
        ob_ref, wbf_ref = rest

    @pl.when(pl.program_id(1) == 0)
    def _():
        _cast_weight_tile(w_ref, wbf_ref)

    acc = jnp.dot(x_ref[...], wbf_ref[...], preferred_element_type=F32) + b_ref[...]
    for u in range(acc.shape[1] // LANES):
        cols = slice(u * LANES, (u + 1) * LANES)
        z = acc[:, cols]
        if use_rope:
            f = flag_ref[:, cols]
            cos = 1.0 + f * (cos_ref[...] - 1.0)
            sin = f * sin_ref[...]
            z = z * cos + pltpu.roll(z, HEAD_DIM // 2, axis=1) * sin
        if with_f32:
            of_ref[:, cols] = z
        ob_ref[:, cols] = z.astype(BF16)


def project(x, w, b, col0, n_cols, rope_flag, cos, sin, *, with_f32, name, tm=512, tn=512):
    M, K = x.shape
    assert col0 % tn == 0 and n_cols % tn == 0
    j0 = col0 // tn
    use_rope = rope_flag is not None
    if not use_rope:
        rope_flag = jnp.zeros((1, n_cols), F32)
    kern = functools.partial(_proj_body, with_f32=with_f32, use_rope=use_rope)
    out_block = pl.BlockSpec((tm, tn), lambda j, i: (i, j))
    outs = pl.pallas_call(
        kern,
        grid=(n_cols // tn, pl.cdiv(M, tm)),
        in_specs=[pl.BlockSpec((tm, K), lambda j, i: (i, 0)),
                  pl.BlockSpec((K, tn), lambda j, i: (0, j + j0)),
                  pl.BlockSpec((1, tn), lambda j, i: (0, j + j0)),
                  pl.BlockSpec((1, tn), lambda j, i: (0, j)),
                  pl.BlockSpec((tm, LANES), lambda j, i: (i, 0)),
                  pl.BlockSpec((tm, LANES), lambda j, i: (i, 0))],
        out_specs=[out_block, out_block] if with_f32 else [out_block],
        out_shape=([jax.ShapeDtypeStruct((M, n_cols), F32)] if with_f32 else [])
        + [jax.ShapeDtypeStruct((M, n_cols), BF16)],
        scratch_shapes=[pltpu.VMEM((K, tn), BF16)],
        compiler_params=_cparams(2),
        name=name,
    )(x, w, b, rope_flag, cos, sin)
    return (outs[0], outs[1]) if with_f32 else (None, outs[0])


def _merge_body(x_ref, od_ref, on_ref, of_ref, wg_ref, bg_ref, wd_ref, wn_ref, wf_ref, o_ref, wdb, wnb, wfb):
    @pl.when(pl.program_id(1) == 0)
    def _():
        _cast_weight_tile(wd_ref, wdb)
        _cast_weight_tile(wn_ref, wnb)
        _cast_weight_tile(wf_ref, wfb)

    x = x_ref[...]
    acc = None
    for k, (o_k, w_k) in enumerate(((od_ref, wdb), (on_ref, wnb), (of_ref, wfb))):
        gate = jax.nn.sigmoid(jnp.dot(x, wg_ref[k], preferred_element_type=F32) + bg_ref[k])
        term = gate * jnp.dot(o_k[...], w_k[...], preferred_element_type=F32)
        acc = term if acc is None else acc + term
    o_ref[...] = acc.astype(o_ref.dtype)


def merge_branches(x, o_d, o_n, o_f, w_gate, b_gate, w_d, w_n, w_f, *, tm=256, tn=256):
    M, D = x.shape
    tn = min(tn, D)
    return pl.pallas_call(
        _merge_body,
        grid=(D // tn, pl.cdiv(M, tm)),
        in_specs=[pl.BlockSpec((tm, D), lambda j, i: (i, 0)),
                  pl.BlockSpec((tm, W_DIFF), lambda j, i: (i, 0)),
                  pl.BlockSpec((tm, W_NSA), lambda j, i: (i, 0)),
                  pl.BlockSpec((tm, W_FOX), lambda j, i: (i, 0)),
                  pl.BlockSpec((3, D, tn), lambda j, i: (0, 0, j)),
                  pl.BlockSpec((3, 1, tn), lambda j, i: (0, 0, j)),
                  pl.BlockSpec((W_DIFF, tn), lambda j, i: (0, j)),
                  pl.BlockSpec((W_NSA, tn), lambda j, i: (0, j)),
                  pl.BlockSpec((W_FOX, tn), lambda j, i: (0, j))],
        out_specs=pl.BlockSpec((tm, tn), lambda j, i: (i, j)),
        out_shape=jax.ShapeDtypeStruct((M, D), BF16),
        scratch_shapes=[pltpu.VMEM((W_DIFF, tn), BF16), pltpu.VMEM((W_NSA, tn), BF16),
                        pltpu.VMEM((W_FOX, tn), BF16)],
        compiler_params=_cparams(2),
        name="merge_branches",
    )(x, o_d, o_n, o_f, w_gate, b_gate, w_d, w_n, w_f)


def _ln_body(x_ref, y_ref, g_ref, b_ref, o_ref, obf_ref):
    v = DEEPNORM_ALPHA * x_ref[...] + y_ref[...]
    mu = jnp.mean(v, -1, keepdims=True)
    c = v - mu
    var = jnp.mean(c * c, -1, keepdims=True)
    out = c * lax.rsqrt(var + LN_EPS) * g_ref[...] + b_ref[...]
    o_ref[...] = out
    obf_ref[...] = out.astype(BF16)


def residual_layer_norm(x, y, g, b, *, tm=256):
    M, D = x.shape
    return pl.pallas_call(
        _ln_body,
        grid=(pl.cdiv(M, tm),),
        in_specs=[pl.BlockSpec((tm, D), lambda i: (i, 0)),
                  pl.BlockSpec((tm, D), lambda i: (i, 0)),
                  pl.BlockSpec((1, D), lambda i: (0, 0)),
                  pl.BlockSpec((1, D), lambda i: (0, 0))],
        out_specs=[pl.BlockSpec((tm, D), lambda i: (i, 0)),
                   pl.BlockSpec((tm, D), lambda i: (i, 0))],
        out_shape=[jax.ShapeDtypeStruct((M, D), F32), jax.ShapeDtypeStruct((M, D), BF16)],
        compiler_params=_cparams(1),
        name="residual_layer_norm",
    )(x, y, g, b)


def _softmax_rows(s, mask):
    s = jnp.where(mask, s, NEG_BIG)
    m = jnp.max(s, -1, keepdims=True)
    p = jnp.where(mask, jnp.exp(s - m), 0.0)
    d = jnp.sum(p, -1, keepdims=True)
    return p * (1.0 / jnp.where(d > 0, d, 1.0))


def _dot_nt(a, b, precision=None):
    return lax.dot_general(a, b, (((1,), (1,)), ((), ())), preferred_element_type=F32, precision=precision)


def _scores(q, k):
    return _dot_nt(q, k) * ATTN_SCALE


def _stack_heads(q, units):
    return jnp.concatenate([q[:, u * LANES:(u + 1) * LANES] for u in units], axis=0)


def _online_step(state, s, mask, v):
    m, l, acc = state
    if mask is not None:
        s = jnp.where(mask, s, NEG_BIG)
    m_new = jnp.maximum(m, jnp.max(s, -1, keepdims=True))
    p = jnp.exp(s - m_new)
    if mask is not None:
        p = jnp.where(mask, p, 0.0)
    alpha = jnp.exp(m - m_new)
    l_new = alpha * l + jnp.sum(p, -1, keepdims=True)
    acc_new = alpha * acc + jnp.dot(p.astype(BF16), v, preferred_element_type=F32)
    return m_new, l_new, acc_new


def _online_finish(state):
    _, l, acc = state
    return acc * (1.0 / jnp.where(l > 0, l, 1.0))


def _load_state(m_ref, l_ref, acc_ref, idx):
    return m_ref[idx][:, :1], l_ref[idx][:, :1], acc_ref[idx]


def _store_state(m_ref, l_ref, acc_ref, idx, state):
    m, l, acc = state
    m_ref[idx] = jnp.broadcast_to(m, m_ref.shape[1:])
    l_ref[idx] = jnp.broadcast_to(l, l_ref.shape[1:])
    acc_ref[idx] = acc


def _init_state(m_ref, l_ref, acc_ref):
    m_ref[...] = jnp.full(m_ref.shape, NEG_BIG, F32)
    l_ref[...] = jnp.zeros(l_ref.shape, F32)
    acc_ref[...] = jnp.zeros(acc_ref.shape, F32)


def _page_columns(page_refs, unit, width=LANES):
    c0 = unit * LANES
    return jnp.concatenate([r[:, c0:c0 + width] for r in page_refs], axis=0).astype(BF16)


def _token_of_row(shape):
    return lax.broadcasted_iota(jnp.int32, shape, 0) % TOK_PAD


def _new_key_mask(n_rows):
    t = _token_of_row((n_rows, NEW_PAD))
    j = lax.broadcasted_iota(jnp.int32, (n_rows, NEW_PAD), 1)
    return (j <= t) & (j < DEC_SEQ)


def _subln(o, g_ref, post):
    return o * lax.rsqrt(jnp.mean(o * o, -1, keepdims=True) + LN_EPS) * g_ref[...] * post


def _diff_prompt_body(sc_ref, q_ref, k_ref, v_ref, g_ref, o_ref, *, hg, tq):
    qi = pl.program_id(2)
    T = k_ref.shape[0]
    lam = sc_ref[0]
    post = sc_ref[1]
    qpos = qi * tq + lax.broadcasted_iota(jnp.int32, (tq, T), 0)
    kpos = lax.broadcasted_iota(jnp.int32, (tq, T), 1)
    mask = (kpos <= qpos)[None]
    q = q_ref[...]
    k = k_ref[...]

    def probs(c):
        qc = _stack_heads(q, [h * 2 + c for h in range(hg)])
        s = _scores(qc, k[:, c * LANES:(c + 1) * LANES]).reshape(hg, tq, T)
        return _softmax_rows(s, mask)

    a = probs(0) - lam * probs(1)
    o = jnp.dot(a.astype(BF16).reshape(hg * tq, T), v_ref[...], preferred_element_type=F32)
    o = _subln(o, g_ref, post)
    for h in range(hg):
        o_ref[:, h * 2 * HEAD_DIM:(h + 1) * 2 * HEAD_DIM] = o[h * tq:(h + 1) * tq].astype(o_ref.dtype)


def diff_attention_prompt(q, kv, scalars, subln_g, n_batch, T, *, tq=Q_TILE):
    hg = H_DIFF // KV_DIFF
    d2 = 2 * HEAD_DIM
    nq = T // tq
    kern = functools.partial(_diff_prompt_body, hg=hg, tq=tq)
    return pl.pallas_call(
        kern,
        grid=(n_batch, KV_DIFF, nq),
        in_specs=[pl.BlockSpec(memory_space=pltpu.SMEM),
                  pl.BlockSpec((tq, hg * d2), lambda b, g, i: (b * nq + i, g)),
                  pl.BlockSpec((T, d2), lambda b, g, i: (b, g)),
                  pl.BlockSpec((T, d2), lambda b, g, i: (b, KV_DIFF + g)),
                  pl.BlockSpec((1, d2), lambda b, g, i: (0, 0))],
        out_specs=pl.BlockSpec((tq, hg * d2), lambda b, g, i: (b * nq + i, g)),
        out_shape=jax.ShapeDtypeStruct((n_batch * T, W_DIFF), BF16),
        compiler_params=_cparams(3),
        name="diff_attention_prompt",
    )(scalars, q, kv, kv, subln_g)


def _diff_sample_body(pt_ref, *refs, hg):
    pages = refs[:PAGES_PER_STEP]
    sc_ref, q_ref, new_ref, g_ref, o_ref, m_ref, l_ref, acc_ref = refs[PAGES_PER_STEP:]
    s_id = pl.program_id(1)
    n_rows = hg * TOK_PAD

    @pl.when(s_id == 0)
    def _():
        _init_state(m_ref, l_ref, acc_ref)

    for g in range(KV_DIFF):
        v = _page_columns(pages, (KV_DIFF + g) * 2, 2 * LANES)
        for c in range(2):
            idx = g * 2 + c
            s = _scores(q_ref[g, c], _page_columns(pages, g * 2 + c))
            state = _online_step(_load_state(m_ref, l_ref, acc_ref, idx), s, None, v)
            _store_state(m_ref, l_ref, acc_ref, idx, state)

    @pl.when(s_id == pl.num_programs(1) - 1)
    def _():
        lam = sc_ref[0]
        post = sc_ref[1]
        mask = _new_key_mask(n_rows)
        new = new_ref[...]
        for g in range(KV_DIFF):
            v = new[:, (KV_DIFF + g) * 2 * LANES:(KV_DIFF + g + 1) * 2 * LANES]
            outs = []
            for c in range(2):
                idx = g * 2 + c
                k = new[:, idx * LANES:(idx + 1) * LANES]
                state = _online_step(_load_state(m_ref, l_ref, acc_ref, idx), _scores(q_ref[g, c], k), mask, v)
                outs.append(_online_finish(state))
            o_ref[g] = _subln(outs[0] - lam * outs[1], g_ref, post)


def _page_specs(width, n_pages_per_step):
    def spec(k):
        return pl.BlockSpec((PAGE_SIZE, width),
                            lambda b, s, pt, *_: (pt[b * N_PAGES + s * n_pages_per_step + k], 0))
    return [spec(k) for k in range(n_pages_per_step)]


def diff_attention_sample(pt_flat, scalars, cache, q, new_kv, subln_g):
    Bs = q.shape[0]
    hg = H_DIFF // KV_DIFF
    n_rows = hg * TOK_PAD
    width = cache.shape[1]
    d2 = 2 * HEAD_DIM
    grid_spec = pltpu.PrefetchScalarGridSpec(
        num_scalar_prefetch=1,
        grid=(Bs, N_PAGES // PAGES_PER_STEP),
        in_specs=_page_specs(width, PAGES_PER_STEP) + [
            pl.BlockSpec(memory_space=pltpu.SMEM),
            pl.BlockSpec((None, KV_DIFF, 2, n_rows, HEAD_DIM), lambda b, s, *_: (b, 0, 0, 0, 0)),
            pl.BlockSpec((None, NEW_PAD, width), lambda b, s, *_: (b, 0, 0)),
            pl.BlockSpec((1, d2), lambda b, s, *_: (0, 0))],
        out_specs=pl.BlockSpec((None, KV_DIFF, n_rows, d2), lambda b, s, *_: (b, 0, 0, 0)),
        scratch_shapes=[pltpu.VMEM((KV_DIFF * 2, n_rows, LANES), F32), pltpu.VMEM((KV_DIFF * 2, n_rows, LANES), F32),
                        pltpu.VMEM((KV_DIFF * 2, n_rows, d2), F32)],
    )
    return pl.pallas_call(
        functools.partial(_diff_sample_body, hg=hg),
        grid_spec=grid_spec,
        out_shape=jax.ShapeDtypeStruct((Bs, KV_DIFF, n_rows, d2), F32),
        compiler_params=_cparams(2),
        name="diff_attention_sample",
    )(pt_flat, *([cache] * PAGES_PER_STEP), scalars, q, new_kv, subln_g)


def _fox_prompt_body(q_ref, k_ref, v_ref, cq_ref, ck_ref, o_ref, *, hg, tq):
    qi = pl.program_id(2)
    T = k_ref.shape[0]
    qpos = qi * tq + lax.broadcasted_iota(jnp.int32, (tq, T), 0)
    kpos = lax.broadcasted_iota(jnp.int32, (tq, T), 1)
    mask = (kpos <= qpos)[None]
    q = _stack_heads(q_ref[...], range(hg))
    cq = cq_ref[...]
    ck = ck_ref[...]
    bias = jnp.concatenate([cq[:, h:h + 1] - ck[h:h + 1, :] for h in range(hg)], axis=0)
    s = (_scores(q, k_ref[...]) + bias).reshape(hg, tq, T)
    p = _softmax_rows(s, mask)
    o = jnp.dot(p.astype(BF16).reshape(hg * tq, T), v_ref[...], preferred_element_type=F32)
    for h in range(hg):
        o_ref[:, h * HEAD_DIM:(h + 1) * HEAD_DIM] = o[h * tq:(h + 1) * tq].astype(o_ref.dtype)


def fox_attention_prompt(q, kv, c_col, c_row, n_batch, T, *, tq=Q_TILE):
    hg = H_FOX // KV_FOX
    nq = T // tq
    kern = functools.partial(_fox_prompt_body, hg=hg, tq=tq)
    return pl.pallas_call(
        kern,
        grid=(n_batch, KV_FOX, nq),
        in_specs=[pl.BlockSpec((tq, hg * HEAD_DIM), lambda b, g, i: (b * nq + i, g)),
                  pl.BlockSpec((T, HEAD_DIM), lambda b, g, i: (b, g)),
                  pl.BlockSpec((T, HEAD_DIM), lambda b, g, i: (b, KV_FOX + g)),
                  pl.BlockSpec((None, tq, hg), lambda b, g, i: (g, b * nq + i, 0)),
                  pl.BlockSpec((None, None, hg, T), lambda b, g, i: (b, g, 0, 0))],
        out_specs=pl.BlockSpec((tq, hg * HEAD_DIM), lambda b, g, i: (b * nq + i, g)),
        out_shape=jax.ShapeDtypeStruct((n_batch * T, W_FOX), BF16),
        compiler_params=_cparams(3),
        name="fox_attention_prompt",
    )(q, kv, kv, c_col, c_row)


def _head_rows(c, hg):
    return jnp.concatenate([jnp.broadcast_to(c[h:h + 1, :], (TOK_PAD, c.shape[1])) for h in range(hg)], axis=0)


def _fox_sample_body(pt_ref, *refs, hg):
    pages = refs[:PAGES_PER_STEP]
    q_ref, ckp_ref, cq_ref, ckn_ref, new_ref, o_ref, m_ref, l_ref, acc_ref = refs[PAGES_PER_STEP:]
    s_id = pl.program_id(1)
    n_rows = hg * TOK_PAD

    @pl.when(s_id == 0)
    def _():
        _init_state(m_ref, l_ref, acc_ref)

    for g in range(KV_FOX):
        bias = cq_ref[g] - _head_rows(ckp_ref[g], hg)
        s = _scores(q_ref[g], _page_columns(pages, g)) + bias
        state = _online_step(_load_state(m_ref, l_ref, acc_ref, g), s, None, _page_columns(pages, KV_FOX + g))
        _store_state(m_ref, l_ref, acc_ref, g, state)

    @pl.when(s_id == pl.num_programs(1) - 1)
    def _():
        mask = _new_key_mask(n_rows)
        new = new_ref[...]
        for g in range(KV_FOX):
            bias = cq_ref[g] - _head_rows(ckn_ref[g], hg)
            s = _scores(q_ref[g], new[:, g * LANES:(g + 1) * LANES]) + bias
            v = new[:, (KV_FOX + g) * LANES:(KV_FOX + g + 1) * LANES]
            o_ref[g] = _online_finish(_online_step(_load_state(m_ref, l_ref, acc_ref, g), s, mask, v))


def fox_attention_sample(pt_flat, cache, q, ck_past, cq_col, ck_new, new_kv):
    Bs = q.shape[0]
    hg = H_FOX // KV_FOX
    n_rows = hg * TOK_PAD
    width = cache.shape[1]
    keys_per_step = PAGES_PER_STEP * PAGE_SIZE
    grid_spec = pltpu.PrefetchScalarGridSpec(
        num_scalar_prefetch=1,
        grid=(Bs, N_PAGES // PAGES_PER_STEP),
        in_specs=_page_specs(width, PAGES_PER_STEP) + [
            pl.BlockSpec((None, KV_FOX, n_rows, HEAD_DIM), lambda b, s, *_: (b, 0, 0, 0)),
            pl.BlockSpec((None, KV_FOX, hg, keys_per_step), lambda b, s, *_: (b, 0, 0, s)),
            pl.BlockSpec((None, KV_FOX, n_rows, 1), lambda b, s, *_: (b, 0, 0, 0)),
            pl.BlockSpec((None, KV_FOX, hg, NEW_PAD), lambda b, s, *_: (b, 0, 0, 0)),
            pl.BlockSpec((None, NEW_PAD, width), lambda b, s, *_: (b, 0, 0))],
        out_specs=pl.BlockSpec((None, KV_FOX, n_rows, HEAD_DIM), lambda b, s, *_: (b, 0, 0, 0)),
        scratch_shapes=[pltpu.VMEM((KV_FOX, n_rows, LANES), F32), pltpu.VMEM((KV_FOX, n_rows, LANES), F32),
                        pltpu.VMEM((KV_FOX, n_rows, HEAD_DIM), F32)],
    )
    return pl.pallas_call(
        functools.partial(_fox_sample_body, hg=hg),
        grid_spec=grid_spec,
        out_shape=jax.ShapeDtypeStruct((Bs, KV_FOX, n_rows, HEAD_DIM), F32),
        compiler_params=_cparams(2),
        name="fox_attention_sample",
    )(pt_flat, *([cache] * PAGES_PER_STEP), q, ck_past, cq_col, ck_new, new_kv)


def _gelu_tanh(x):
    return 0.5 * x * (1.0 + jnp.tanh(math.sqrt(2.0 / math.pi) * (x + 0.044715 * x * x * x)))


def _compress_body(*refs, n_src, tok_cols, paged):
    if paged:
        refs = refs[1:]
    srcs = refs[:n_src]
    pe_ref, w1_ref, w2_ref, o_ref, a_ref, b_ref = refs[n_src:]
    s_id = pl.program_id(1)
    n_units = 2 * KV_NSA
    col = pl.multiple_of(s_id * LANES, LANES)
    for u in range(n_units):
        kv = u // KV_NSA
        for half, dst in ((0, a_ref), (1, b_ref)):
            rows = []
            for src in srcs:
                pieces = [src[:, i * tok_cols + u * LANES:i * tok_cols + (u + 1) * LANES]
                          + pe_ref[kv, half * CMP_STRIDE + i:half * CMP_STRIDE + i + 1, :]
                          for i in range(CMP_STRIDE)]
                rows.append(jnp.concatenate(pieces, axis=1))
            x = jnp.concatenate(rows, axis=0).astype(BF16)
            dst[u, :, pl.ds(col, LANES)] = _dot_nt(w1_ref[kv, half], x)

    @pl.when(s_id == pl.num_programs(1) - 1)
    def _():
        n_chunks = a_ref.shape[2]
        for u in range(n_units):
            kv = u // KV_NSA
            y = a_ref[u] + pltpu.roll(b_ref[u], n_chunks - 1, axis=1)
            o_ref[u] = jnp.dot(w2_ref[kv], _gelu_tanh(y).astype(BF16), preferred_element_type=F32).astype(BF16)


def _compress_weights(pe, w1, w2):
    half = CMP_STRIDE * HEAD_DIM
    w1t = jnp.stack([jnp.stack([w1[kv, :half].T, w1[kv, half:].T]) for kv in range(2)]).astype(BF16)
    w2t = jnp.transpose(w2, (0, 2, 1)).astype(BF16)
    return pe, w1t, w2t


def compress_prompt(src, pe, w1t, w2t, n_batch, T):
    n_chunks = T // CMP_STRIDE
    assert n_chunks == LANES
    tok_cols = 2 * KV_NSA * HEAD_DIM
    kern = functools.partial(_compress_body, n_src=1, tok_cols=tok_cols, paged=False)
    full3 = lambda b, s: (0, 0, 0)
    return pl.pallas_call(
        kern,
        grid=(n_batch, 1),
        in_specs=[pl.BlockSpec((n_chunks, CMP_STRIDE * tok_cols), lambda b, s: (b, 0)),
                  pl.BlockSpec(pe.shape, full3),
                  pl.BlockSpec(w1t.shape, lambda b, s: (0, 0, 0, 0)),
                  pl.BlockSpec(w2t.shape, full3)],
        out_specs=pl.BlockSpec((None, 2 * KV_NSA, HEAD_DIM, n_chunks), lambda b, s: (b, 0, 0, 0)),
        out_shape=jax.ShapeDtypeStruct((n_batch, 2 * KV_NSA, HEAD_DIM, n_chunks), BF16),
        scratch_shapes=[pltpu.VMEM((2 * KV_NSA, HEAD_DIM, n_chunks), F32),
                        pltpu.VMEM((2 * KV_NSA, HEAD_DIM, n_chunks), F32)],
        compiler_params=_cparams(2),
        name="nsa_compress_prompt",
    )(src, pe, w1t, w2t)


def compress_paged(pt_flat, cache, pe, w1t, w2t, n_batch):
    chunks_per_page = PAGE_SIZE // CMP_STRIDE
    tok_cols = cache.shape[2] // CMP_STRIDE
    n_chunks = PAST_LEN // CMP_STRIDE
    assert chunks_per_page * CMP_PAGES_PER_STEP == LANES
    kern = functools.partial(_compress_body, n_src=CMP_PAGES_PER_STEP, tok_cols=tok_cols, paged=True)

    def page_spec(k):
        return pl.BlockSpec((None, chunks_per_page, cache.shape[2]),
                            lambda b, s, pt: (pt[b * N_PAGES + s * CMP_PAGES_PER_STEP + k], 0, 0))

    grid_spec = pltpu.PrefetchScalarGridSpec(
        num_scalar_prefetch=1,
        grid=(n_batch, N_PAGES // CMP_PAGES_PER_STEP),
        in_specs=[page_spec(k) for k in range(CMP_PAGES_PER_STEP)] + [
            pl.BlockSpec(pe.shape, lambda b, s, pt: (0, 0, 0)),
            pl.BlockSpec(w1t.shape, lambda b, s, pt: (0, 0, 0, 0)),
            pl.BlockSpec(w2t.shape, lambda b, s, pt: (0, 0, 0))],
        out_specs=pl.BlockSpec((None, 2 * KV_NSA, HEAD_DIM, n_chunks), lambda b, s, pt: (b, 0, 0, 0)),
        scratch_shapes=[pltpu.VMEM((2 * KV_NSA, HEAD_DIM, n_chunks), F32),
                        pltpu.VMEM((2 * KV_NSA, HEAD_DIM, n_chunks), F32)],
    )
    return pl.pallas_call(
        kern,
        grid_spec=grid_spec,
        out_shape=jax.ShapeDtypeStruct((n_batch, 2 * KV_NSA, HEAD_DIM, n_chunks), BF16),
        compiler_params=_cparams(2),
        name="nsa_compress_paged",
    )(pt_flat, *([cache] * CMP_PAGES_PER_STEP), pe, w1t, w2t)


def _block_keys(score_t, cur):
    jj = lax.broadcasted_iota(jnp.int32, score_t.shape, 0)
    valid = jj <= cur
    forced = (jj == 0) | (jj == cur) | (jj == cur - 1)
    return jnp.where(valid, jnp.where(forced, jnp.inf, score_t), -jnp.inf), valid


def _select_blocks(score_t, cur, n_sel):
    key, valid = _block_keys(score_t, cur)
    jj = lax.broadcasted_iota(jnp.int32, score_t.shape, 0)
    rank = jnp.zeros(score_t.shape, F32)
    for i in range(n_sel):
        ki = key[i:i + 1, :]
        beats = (ki > key) | ((ki == key) & (jj > i))
        rank = rank + jnp.where(beats, 1.0, 0.0)
    return jnp.where(valid & (rank < float(min(SEL_TOPK, n_sel))), 1.0, 0.0)


def _cover_matrix(n_sel_rows, n_cmp_cols, n_cmp, n_sel):
    cs = np.arange(n_cmp_cols) * CMP_STRIDE
    ss = np.arange(n_sel_rows) * SEL_BLOCK
    cover = (cs[None, :] < ss[:, None] + SEL_BLOCK) & (cs[None, :] + CMP_BLOCK > ss[:, None])
    cover &= (np.arange(n_cmp_cols)[None, :] < n_cmp) & (np.arange(n_sel_rows)[:, None] < n_sel)
    return jnp.asarray(cover.astype(np.float32))


def _nsa_prompt_body(q_ref, cmp_ref, ks_ref, vs_ref, kw_ref, vw_ref, gate_ref, cover_ref, expand_ref, o_ref,
                     *, hg, tq, n_win, n_cmp):
    qi = pl.program_id(2)
    T = ks_ref.shape[0]
    n_cmp_pad = cmp_ref.shape[2]
    n_sel = cover_ref.shape[0]
    q = _stack_heads(q_ref[...], range(hg))
    q0 = qi * tq

    s = (jnp.dot(q, cmp_ref[0], preferred_element_type=F32) * ATTN_SCALE).reshape(hg, tq, n_cmp_pad)
    qpos_c = q0 + lax.broadcasted_iota(jnp.int32, (tq, n_cmp_pad), 0)
    nn = lax.broadcasted_iota(jnp.int32, (tq, n_cmp_pad), 1)
    p_cmp = _softmax_rows(s, ((nn * CMP_STRIDE + (CMP_BLOCK - 1) <= qpos_c) & (nn < n_cmp))[None])
    o_cmp = _dot_nt(p_cmp.astype(BF16).reshape(hg * tq, n_cmp_pad), cmp_ref[1])

    p_sum = jnp.sum(p_cmp, axis=0)
    score_t = _dot_nt(cover_ref[...], p_sum, precision=lax.Precision.HIGHEST)
    cur = (q0 + lax.broadcasted_iota(jnp.int32, (n_sel, tq), 1)) // SEL_BLOCK
    sel_t = _select_blocks(score_t, cur, n_sel)
    sel_keys = jnp.dot(sel_t.T.astype(BF16), expand_ref[...], preferred_element_type=F32)

    qpos = q0 + lax.broadcasted_iota(jnp.int32, (tq, T), 0)
    kpos = lax.broadcasted_iota(jnp.int32, (tq, T), 1)
    mask_slc = ((sel_keys > 0.5) & (kpos <= qpos))[None]
    s = _scores(q, ks_ref[...]).reshape(hg, tq, T)
    p_slc = _softmax_rows(s, mask_slc)
    o_slc = jnp.dot(p_slc.astype(BF16).reshape(hg * tq, T), vs_ref[...], preferred_element_type=F32)

    start = pl.multiple_of(jnp.maximum(q0 + tq - n_win, 0), tq)
    kw = kw_ref[pl.ds(start, n_win), :]
    vw = vw_ref[pl.ds(start, n_win), :]
    dlt = (q0 + lax.broadcasted_iota(jnp.int32, (tq, n_win), 0)) - (
        start + lax.broadcasted_iota(jnp.int32, (tq, n_win), 1))
    mask_win = ((dlt >= 0) & (dlt <= WINDOW))[None]
    s = _scores(q, kw).reshape(hg, tq, n_win)
    p_win = _softmax_rows(s, mask_win)
    o_win = jnp.dot(p_win.astype(BF16).reshape(hg * tq, n_win), vw, preferred_element_type=F32)

    gate = jax.nn.sigmoid(gate_ref[...])
    for h in range(hg):
        rows = slice(h * tq, (h + 1) * tq)
        o = (gate[:, 3 * h:3 * h + 1] * o_cmp[rows] + gate[:, 3 * h + 1:3 * h + 2] * o_slc[rows]
             + gate[:, 3 * h + 2:3 * h + 3] * o_win[rows])
        o_ref[:, h * HEAD_DIM:(h + 1) * HEAD_DIM] = o.astype(o_ref.dtype)


def nsa_attention_prompt(q, kv, cmp_t, gates, n_batch, T, *, tq=Q_TILE):
    hg = H_NSA // KV_NSA
    G = KV_NSA
    nq = T // tq
    n_cmp_pad = cmp_t.shape[3]
    n_cmp = (T - CMP_BLOCK) // CMP_STRIDE + 1
    n_sel = -(-T // SEL_BLOCK)
    cover_t = _cover_matrix(n_sel, n_cmp_pad, n_cmp, n_sel)
    expand = jnp.asarray((np.arange(T)[None, :] // SEL_BLOCK == np.arange(n_sel)[:, None]).astype(np.float32),
                         dtype=BF16)
    n_win = min(WINDOW + tq, T)
    kern = functools.partial(_nsa_prompt_body, hg=hg, tq=tq, n_win=n_win, n_cmp=n_cmp)
    unit = lambda j: (lambda b, g, i: (b, j * G + g))
    return pl.pallas_call(
        kern,
        grid=(n_batch, G, nq),
        in_specs=[pl.BlockSpec((tq, hg * HEAD_DIM), lambda b, g, i: (b * nq + i, g)),
                  pl.BlockSpec((None, 2, None, HEAD_DIM, n_cmp_pad), lambda b, g, i: (b, 0, g, 0, 0)),
                  pl.BlockSpec((T, HEAD_DIM), unit(2)),
                  pl.BlockSpec((T, HEAD_DIM), unit(3)),
                  pl.BlockSpec((T, HEAD_DIM), unit(4)),
                  pl.BlockSpec((T, HEAD_DIM), unit(5)),
                  pl.BlockSpec((None, tq, hg * 3), lambda b, g, i: (g, b * nq + i, 0)),
                  pl.BlockSpec((n_sel, n_cmp_pad), lambda b, g, i: (0, 0)),
                  pl.BlockSpec((n_sel, T), lambda b, g, i: (0, 0))],
        out_specs=pl.BlockSpec((tq, hg * HEAD_DIM), lambda b, g, i: (b * nq + i, g)),
        out_shape=jax.ShapeDtypeStruct((n_batch * T, W_NSA), BF16),
        compiler_params=_cparams(3),
        name="nsa_attention_prompt",
    )(q, cmp_t.reshape(n_batch, 2, G, HEAD_DIM, n_cmp_pad), kv, kv, kv, kv, gates, cover_t, expand)


def _nsa_sample_body(pt_ref, *refs, hg, n_cmp, n_sel):
    pages = refs[:PAGES_PER_STEP]
    (q_ref, cmp_ref, new_ref, win_ref, gate_ref, cover_ref, expand_ref, o_ref,
     sel_ref, key_ref, ocmp_ref, m_ref, l_ref, acc_ref) = refs[PAGES_PER_STEP:]
    s_id = pl.program_id(1)
    G = KV_NSA
    n_rows = hg * TOK_PAD
    n_cmp_pad = cmp_ref.shape[3]
    n_sel_pad = cover_ref.shape[0]
    cur = PAST_LEN // SEL_BLOCK
    blocks_per_step = PAGES_PER_STEP * PAGE_SIZE // SEL_BLOCK

    @pl.when(s_id == 0)
    def _():
        _init_state(m_ref, l_ref, acc_ref)
        nn = lax.broadcasted_iota(jnp.int32, (n_rows, n_cmp_pad), 1)
        for g in range(G):
            s = jnp.dot(q_ref[g], cmp_ref[0, g], preferred_element_type=F32) * ATTN_SCALE
            p = _softmax_rows(s, nn < n_cmp)
            ocmp_ref[g] = _dot_nt(p.astype(BF16), cmp_ref[1, g])
            p_sum = jnp.sum(p.reshape(hg, TOK_PAD, n_cmp_pad), axis=0)
            p_sum = jnp.concatenate([p_sum, jnp.zeros((LANES - TOK_PAD, n_cmp_pad), F32)], axis=0)
            score_t = _dot_nt(cover_ref[...], p_sum, precision=lax.Precision.HIGHEST)
            key, valid = _block_keys(score_t, cur)
            key_ref[...] = key
            jj = lax.broadcasted_iota(jnp.int32, key.shape, 0)

            def rank_step(i, rank):
                ki = key_ref[pl.ds(i, 1), :]
                kk = key_ref[...]
                beats = (ki > kk) | ((ki == kk) & (jj > i))
                return rank + jnp.where(beats, 1.0, 0.0)

            rank = lax.fori_loop(0, n_sel, rank_step, jnp.zeros(key.shape, F32))
            sel_ref[g] = jnp.where(valid & (jj < n_sel) & (rank < float(min(SEL_TOPK, n_sel))), 1.0, 0.0)

    row0 = pl.multiple_of(s_id * blocks_per_step, blocks_per_step)
    for g in range(G):
        sel_blk = sel_ref[g, pl.ds(row0, blocks_per_step), :]
        tok_keys = jnp.dot(sel_blk.T[:TOK_PAD].astype(BF16), expand_ref[...], preferred_element_type=F32)
        mask = jnp.concatenate([tok_keys] * hg, axis=0) > 0.5
        s = _scores(q_ref[g], _page_columns(pages, 2 * G + g))
        state = _online_step(_load_state(m_ref, l_ref, acc_ref, g), s, mask, _page_columns(pages, 3 * G + g))
        _store_state(m_ref, l_ref, acc_ref, g, state)

    @pl.when(s_id == pl.num_programs(1) - 1)
    def _():
        new = new_ref[...]
        mask_new = _new_key_mask(n_rows)
        t_row = _token_of_row((n_rows, WINDOW))
        i_key = lax.broadcasted_iota(jnp.int32, (n_rows, WINDOW), 1)
        mask_win = i_key >= t_row
        unit = lambda j, g: new[:, (j * G + g) * LANES:(j * G + g + 1) * LANES]
        for g in range(G):
            q = q_ref[g]
            o_slc = _online_finish(_online_step(_load_state(m_ref, l_ref, acc_ref, g), _scores(q, unit(2, g)),
                                                mask_new, unit(3, g)))
            kw = win_ref[:, g * LANES:(g + 1) * LANES].astype(BF16)
            vw = win_ref[:, (G + g) * LANES:(G + g + 1) * LANES].astype(BF16)
            init = (jnp.full((n_rows, 1), NEG_BIG, F32), jnp.zeros((n_rows, 1), F32),
                    jnp.zeros((n_rows, HEAD_DIM), F32))
            state = _online_step(init, _scores(q, kw), mask_win, vw)
            o_win = _online_finish(_online_step(state, _scores(q, unit(4, g)), mask_new, unit(5, g)))
            gate = jax.nn.sigmoid(gate_ref[g])
            o_ref[g] = gate[:, 0:1] * ocmp_ref[g] + gate[:, 1:2] * o_slc + gate[:, 2:3] * o_win


def nsa_attention_sample(pt_flat, cache, q, cmp_t, new_kv, win, gates):
    Bs = q.shape[0]
    G = KV_NSA
    hg = H_NSA // KV_NSA
    n_rows = hg * TOK_PAD
    n_k = PAST_LEN + DEC_SEQ
    n_cmp = (n_k - CMP_BLOCK) // CMP_STRIDE + 1
    n_cmp_pad = cmp_t.shape[3]
    n_sel = -(-n_k // SEL_BLOCK)
    blocks_per_step = PAGES_PER_STEP * PAGE_SIZE // SEL_BLOCK
    n_sel_pad = -(-n_sel // blocks_per_step) * blocks_per_step
    cover_t = _cover_matrix(n_sel_pad, n_cmp_pad, n_cmp, n_sel)
    keys_per_step = PAGES_PER_STEP * PAGE_SIZE
    expand = jnp.asarray((np.arange(keys_per_step)[None, :] // SEL_BLOCK
                          == np.arange(blocks_per_step)[:, None]).astype(np.float32), dtype=BF16)
    width = cache.shape[1]
    kern = functools.partial(_nsa_sample_body, hg=hg, n_cmp=n_cmp, n_sel=n_sel)
    grid_spec = pltpu.PrefetchScalarGridSpec(
        num_scalar_prefetch=1,
        grid=(Bs, N_PAGES // PAGES_PER_STEP),
        in_specs=_page_specs(width, PAGES_PER_STEP) + [
            pl.BlockSpec((None, G, n_rows, HEAD_DIM), lambda b, s, *_: (b, 0, 0, 0)),
            pl.BlockSpec((None, 2, G, HEAD_DIM, n_cmp_pad), lambda b, s, *_: (b, 0, 0, 0, 0)),
            pl.BlockSpec((None, NEW_PAD, new_kv.shape[2]), lambda b, s, *_: (b, 0, 0)),
            pl.BlockSpec((None, WINDOW, win.shape[2]), lambda b, s, *_: (b, 0, 0)),
            pl.BlockSpec((None, G, n_rows, 3), lambda b, s, *_: (b, 0, 0, 0)),
            pl.BlockSpec((n_sel_pad, n_cmp_pad), lambda b, s, *_: (0, 0)),
            pl.BlockSpec((blocks_per_step, keys_per_step), lambda b, s, *_: (0, 0))],
        out_specs=pl.BlockSpec((None, G, n_rows, HEAD_DIM), lambda b, s, *_: (b, 0, 0, 0)),
        scratch_shapes=[pltpu.VMEM((G, n_sel_pad, LANES), F32), pltpu.VMEM((n_sel_pad, LANES), F32),
                        pltpu.VMEM((G, n_rows, HEAD_DIM), F32),
                        pltpu.VMEM((G, n_rows, LANES), F32), pltpu.VMEM((G, n_rows, LANES), F32),
                        pltpu.VMEM((G, n_rows, HEAD_DIM), F32)],
    )
    return pl.pallas_call(
        kern,
        grid_spec=grid_spec,
        out_shape=jax.ShapeDtypeStruct((Bs, G, n_rows, HEAD_DIM), F32),
        compiler_params=_cparams(2),
        name="nsa_attention_sample",
    )(pt_flat, *([cache] * PAGES_PER_STEP), q, cmp_t.reshape(Bs, 2, G, HEAD_DIM, n_cmp_pad), new_kv, win, gates,
      cover_t, expand)


def _expert_changed(te_ref, i):
    prev = te_ref[jnp.maximum(i - 1, 0)]
    return (i == 0) | (te_ref[i] != prev)


def _moe_up_body(te_ref, nu_ref, x_ref, wg_ref, wu_ref, bg_ref, bu_ref, o_ref, wgb, wub):
    i = pl.program_id(1)

    @pl.when(_expert_changed(te_ref, i))
    def _():
        _cast_weight_tile(wg_ref, wgb)
        _cast_weight_tile(wu_ref, wub)

    @pl.when(i < nu_ref[0])
    def _():
        x = x_ref[...]
        g = jnp.dot(x, wgb[...], preferred_element_type=F32) + bg_ref[...]
        u = jnp.dot(x, wub[...], preferred_element_type=F32) + bu_ref[...]
        g = jnp.minimum(g, SWIGLU_LIMIT)
        u = jnp.clip(u, -SWIGLU_LIMIT, SWIGLU_LIMIT)
        o_ref[...] = ((u + 1.0) * (g * jax.nn.sigmoid(SWIGLU_ALPHA * g))).astype(o_ref.dtype)


def moe_up(tile_expert, n_used, x_sorted, w_gate_up, b_gate_up, *, tn=256):
    R, D = x_sorted.shape
    de = w_gate_up.shape[2] // 2
    tn = min(tn, de)
    nj = de // tn
    tm = MOE_TILE
    grid_spec = pltpu.PrefetchScalarGridSpec(
        num_scalar_prefetch=2,
        grid=(nj, R // tm),
        in_specs=[pl.BlockSpec((tm, D), lambda j, i, te, nu: (i, 0)),
                  pl.BlockSpec((None, D, tn), lambda j, i, te, nu: (te[i], 0, j)),
                  pl.BlockSpec((None, D, tn), lambda j, i, te, nu: (te[i], 0, j + nj)),
                  pl.BlockSpec((None, 1, tn), lambda j, i, te, nu: (te[i], 0, j)),
                  pl.BlockSpec((None, 1, tn), lambda j, i, te, nu: (te[i], 0, j + nj))],
        out_specs=pl.BlockSpec((tm, tn), lambda j, i, te, nu: (i, j)),
        scratch_shapes=[pltpu.VMEM((D, tn), BF16), pltpu.VMEM((D, tn), BF16)],
    )
    return pl.pallas_call(
        _moe_up_body,
        grid_spec=grid_spec,
        out_shape=jax.ShapeDtypeStruct((R, de), BF16),
        compiler_params=_cparams(2),
        name="moe_up",
    )(tile_expert, n_used, x_sorted, w_gate_up, w_gate_up, b_gate_up, b_gate_up)


def _moe_down_body(te_ref, nu_ref, a_ref, w_ref, b_ref, o_ref, wb):
    i = pl.program_id(1)

    @pl.when(_expert_changed(te_ref, i))
    def _():
        _cast_weight_tile(w_ref, wb)

    @pl.when(i < nu_ref[0])
    def _():
        o_ref[...] = jnp.dot(a_ref[...], wb[...], preferred_element_type=F32) + b_ref[...]


def moe_down(tile_expert, n_used, act, w_down, b_down, *, tn=512):
    R, de = act.shape
    D = w_down.shape[2]
    tn = min(tn, D)
    tm = MOE_TILE
    grid_spec = pltpu.PrefetchScalarGridSpec(
        num_scalar_prefetch=2,
        grid=(D // tn, R // tm),
        in_specs=[pl.BlockSpec((tm, de), lambda j, i, te, nu: (i, 0)),
                  pl.BlockSpec((None, de, tn), lambda j, i, te, nu: (te[i], 0, j)),
                  pl.BlockSpec((None, 1, tn), lambda j, i, te, nu: (te[i], 0, j))],
        out_specs=pl.BlockSpec((tm, tn), lambda j, i, te, nu: (i, j)),
        scratch_shapes=[pltpu.VMEM((de, tn), BF16)],
    )
    return pl.pallas_call(
        _moe_down_body,
        grid_spec=grid_spec,
        out_shape=jax.ShapeDtypeStruct((R, D), F32),
        compiler_params=_cparams(2),
        name="moe_down",
    )(tile_expert, n_used, act, w_down, b_down)


def moe_ffn(h, h_bf, w_router, b_router, w_gate_up, b_gate_up, w_down, b_down):
    n_tok, D = h.shape
    E = w_router.shape[1]
    tm = MOE_TILE
    logits = matmul_bias(h_bf, w_router, b_router[None, :], tm=512, tn=E, name="router")
    top_v, top_i = lax.top_k(logits, TOP_K)
    gates = jax.nn.softmax(top_v, axis=-1)
    n_assign = n_tok * TOP_K
    e_flat = top_i.reshape(-1)
    onehot = (e_flat[:, None] == jnp.arange(E, dtype=e_flat.dtype)[None, :]).astype(jnp.int32)
    within = jnp.sum((jnp.cumsum(onehot, axis=0) - 1) * onehot, axis=1)
    counts = jnp.sum(onehot, axis=0)
    padded = (counts + tm - 1) // tm * tm
    pad_end = jnp.cumsum(padded)
    dest = (pad_end - padded)[e_flat] + within
    n_tiles = -(-(n_assign + E * (tm - 1)) // tm)
    tok_flat = jnp.arange(n_assign, dtype=jnp.int32) // TOP_K
    row_tok = jnp.full((n_tiles * tm,), n_tok, jnp.int32).at[dest].set(tok_flat)
    tile_expert = jnp.minimum(jnp.searchsorted(pad_end, jnp.arange(n_tiles) * tm, side='right'),
                              E - 1).astype(jnp.int32)
    n_used = (pad_end[-1] // tm).astype(jnp.int32).reshape(1)
    x_pad = jnp.concatenate([h_bf, jnp.zeros((1, D), BF16)], axis=0)
    x_sorted = x_pad[row_tok]
    act = moe_up(tile_expert, n_used, x_sorted, w_gate_up, b_gate_up[:, None, :])
    y = moe_down(tile_expert, n_used, act, w_down, b_down[:, None, :])
    picked = y[dest.reshape(n_tok, TOP_K)]
    return jnp.sum(picked * gates[:, :, None], axis=1)


def _rope_tables(pos):
    half = HEAD_DIM // 2
    inv = ROPE_THETA ** (-jnp.arange(half, dtype=F32) / half)
    ang = pos.astype(F32)[:, None] * inv[None, :]
    cos, sin = jnp.cos(ang), jnp.sin(ang)
    return jnp.concatenate([cos, cos], -1), jnp.concatenate([-sin, sin], -1)


def _unit_flags(pattern):
    return jnp.asarray(np.repeat(np.asarray(pattern, np.float32), LANES)[None, :])


def _diff_scalars(diff_lambda_l, l):
    lam_init = 0.8 - 0.6 * math.exp(-0.3 * l)
    dl = diff_lambda_l.astype(F32)
    lam = jnp.exp(jnp.sum(dl[0] * dl[1])) - jnp.exp(jnp.sum(dl[2] * dl[3])) + lam_init
    return jnp.stack([lam, jnp.asarray(1.0 - lam_init, F32)]).astype(F32)


def _sample_q_rows(q, n_groups, hg, n_comp):
    Bs, Ts, _ = q.shape
    q = q.reshape(Bs, Ts, n_groups, hg, n_comp, HEAD_DIM)
    q = jnp.pad(q, ((0, 0), (0, TOK_PAD - Ts), (0, 0), (0, 0), (0, 0), (0, 0)))
    q = q.transpose(0, 2, 4, 3, 1, 5).reshape(Bs, n_groups, n_comp, hg * TOK_PAD, HEAD_DIM)
    return q[:, :, 0] if n_comp == 1 else q


def _sample_out_rows(o, hg, Ts):
    Bs, G, _, dv = o.shape
    o = o.reshape(Bs, G, hg, TOK_PAD, dv)[:, :, :, :Ts]
    return o.transpose(0, 3, 1, 2, 4).reshape(Bs * Ts, G * hg * dv)


def _pad_new_rows(a):
    return jnp.pad(a, ((0, 0), (0, NEW_PAD - a.shape[1]), (0, 0)))


def kernel(x_prompt, x_sample, cache_diff_kv, cache_nsa_kv, cache_nsa_win, cache_fox_kv, cache_fox_logf, page_table, w_in, b_in, diff_lambda, diff_subln, nsa_cmp_pe, nsa_cmp_w1, nsa_cmp_w2, w_branch_diff, w_branch_nsa, w_branch_fox, w_out, ln1_g, ln1_b, w_router, b_router, w_gate_up, b_gate_up, w_down, b_down, ln2_g, ln2_b):
    B, T, D = x_prompt.shape
    Bs, Ts, _ = x_sample.shape
    n_p = B * T
    n_s = Bs * Ts
    n_pool = cache_diff_kv.shape[1]
    x = jnp.concatenate([x_prompt.reshape(n_p, D), x_sample.reshape(n_s, D)], axis=0)
    x_bf = x.astype(BF16)
    pos = jnp.concatenate([jnp.tile(jnp.arange(T, dtype=jnp.int32), B),
                           jnp.tile(PAST_LEN + jnp.arange(Ts, dtype=jnp.int32), Bs)])
    cos, sin = _rope_tables(pos)
    pt_flat = page_table.reshape(-1).astype(jnp.int32)
    zero_bias = jnp.zeros((1, D), F32)
    flag_all = lambda n: _unit_flags([1] * n)
    flag_diff_kv = _unit_flags([1] * (KV_DIFF * 2) + [0] * (KV_DIFF * 2))
    flag_nsa_kv = _unit_flags(([1] * KV_NSA + [0] * KV_NSA) * 3)
    hg_d, hg_n, hg_f = H_DIFF // KV_DIFF, H_NSA // KV_NSA, H_FOX // KV_FOX
    o_qd, o_kvd, o_qn, o_kvn, o_gn, o_qf, o_kvf, o_ff, o_gm = IN_OFFS[:9]

    st_p, st_s = [], []
    for l in range(DEPTH):
        wl, bl = w_in[l], b_in[l][None, :]
        _, qd = project(x_bf, wl, bl, o_qd, W_DIFF, flag_all(H_DIFF * 2), cos, sin, with_f32=False, name="proj_diff_q")
        kvd_f, kvd = project(x_bf, wl, bl, o_kvd, IN_SIZES[1], flag_diff_kv, cos, sin, with_f32=True,
                             name="proj_diff_kv")
        _, qn = project(x_bf, wl, bl, o_qn, W_NSA, flag_all(H_NSA), cos, sin, with_f32=False, name="proj_nsa_q")
        kvn_f, kvn = project(x_bf, wl, bl, o_kvn, IN_SIZES[3], flag_nsa_kv, cos, sin, with_f32=True,
                             name="proj_nsa_kv")
        w_fox = wl[:, o_qf:o_ff]
        b_fox = bl[:, o_qf:o_ff]
        _, qf = project(x_bf, w_fox, b_fox, 0, W_FOX, None, cos, sin, with_f32=False, name="proj_fox_q")
        kvf_f, kvf = project(x_bf, w_fox, b_fox, W_FOX, IN_SIZES[6], None, cos, sin, with_f32=True,
                             name="proj_fox_kv")
        w_small = jnp.concatenate([wl[:, o_gn:o_qf], wl[:, o_ff:o_gm]], axis=1)
        b_small = jnp.concatenate([bl[:, o_gn:o_qf], bl[:, o_ff:o_gm]], axis=1)
        z_small = matmul_bias(x_bf, w_small, b_small, tm=512, tn=w_small.shape[1], name="proj_small")
        zg_n = z_small[:, :IN_SIZES[4]]
        logf = jax.nn.log_sigmoid(z_small[:, IN_SIZES[4]:])
        scalars = _diff_scalars(diff_lambda[l], l)
        subln_g = diff_subln[l][None, :]
        pe, w1t, w2t = _compress_weights(nsa_cmp_pe[l], nsa_cmp_w1[l], nsa_cmp_w2[l])

        o_d_p = diff_attention_prompt(qd, kvd, scalars, subln_g, B, T)
        cmp_src = kvn_f[:n_p, :2 * KV_NSA * HEAD_DIM].reshape(n_p // CMP_STRIDE, CMP_STRIDE * 2 * KV_NSA * HEAD_DIM)
        cmp_p = compress_prompt(cmp_src, pe, w1t, w2t, B, T)
        gates_n = zg_n.reshape(-1, KV_NSA, hg_n * 3).transpose(1, 0, 2)
        o_n_p = nsa_attention_prompt(qn, kvn, cmp_p, gates_n, B, T)
        c_p = jnp.cumsum(logf[:n_p].reshape(B, T, H_FOX), axis=1)
        c_col = jnp.pad(c_p.reshape(n_p, KV_FOX, hg_f), ((0, n_s), (0, 0), (0, 0))).transpose(1, 0, 2)
        c_row = c_p.reshape(B, T, KV_FOX, hg_f).transpose(0, 2, 3, 1)
        o_f_p = fox_attention_prompt(qf, kvf, c_col, c_row, B, T)

        rows_s = lambda a: a[n_p:].reshape(Bs, Ts, a.shape[1])
        o = diff_attention_sample(pt_flat, scalars, cache_diff_kv[l].reshape(n_pool * PAGE_SIZE, -1),
                                  _sample_q_rows(rows_s(qd), KV_DIFF, hg_d, 2), _pad_new_rows(rows_s(kvd)), subln_g)
        o_d_s = _sample_out_rows(o, hg_d, Ts)
        nsa_pool = cache_nsa_kv[l]
        cmp_s = compress_paged(pt_flat, nsa_pool.reshape(n_pool, PAGE_SIZE // CMP_STRIDE, -1), pe, w1t, w2t, Bs)
        gates_s = jnp.pad(rows_s(zg_n).reshape(Bs, Ts, KV_NSA, hg_n, 3), ((0, 0), (0, TOK_PAD - Ts), (0, 0), (0, 0), (0, 0)))
        gates_s = gates_s.transpose(0, 2, 3, 1, 4).reshape(Bs, KV_NSA, hg_n * TOK_PAD, 3)
        o = nsa_attention_sample(pt_flat, nsa_pool.reshape(n_pool * PAGE_SIZE, -1),
                                 _sample_q_rows(rows_s(qn), KV_NSA, hg_n, 1), cmp_s, _pad_new_rows(rows_s(kvn)),
                                 cache_nsa_win[l].reshape(Bs, cache_nsa_win.shape[2], -1), gates_s)
        o_n_s = _sample_out_rows(o, hg_n, Ts)
        lf_past = cache_fox_logf[l][page_table].reshape(Bs, PAST_LEN, H_FOX).astype(F32)
        suffix = jnp.flip(jnp.cumsum(jnp.flip(lf_past, 1), axis=1), 1)
        ck_past = jnp.concatenate([-suffix[:, 1:], jnp.zeros_like(suffix[:, :1])], axis=1)
        ck_past = ck_past.reshape(Bs, PAST_LEN, KV_FOX, hg_f).transpose(0, 2, 3, 1)
        c_new = jnp.cumsum(rows_s(logf), axis=1)
        c_new_g = c_new.reshape(Bs, Ts, KV_FOX, hg_f).transpose(0, 2, 3, 1)
        cq_col = jnp.pad(c_new_g, ((0, 0), (0, 0), (0, 0), (0, TOK_PAD - Ts))).reshape(Bs, KV_FOX, hg_f * TOK_PAD, 1)
        ck_new = jnp.pad(c_new_g, ((0, 0), (0, 0), (0, 0), (0, NEW_PAD - Ts)))
        o = fox_attention_sample(pt_flat, cache_fox_kv[l].reshape(n_pool * PAGE_SIZE, -1),
                                 _sample_q_rows(rows_s(qf), KV_FOX, hg_f, 1), ck_past, cq_col, ck_new,
                                 _pad_new_rows(rows_s(kvf)))
        o_f_s = _sample_out_rows(o, hg_f, Ts)

        new_d, new_n, new_w = kvd_f, kvn_f[:, :4 * KV_NSA * HEAD_DIM], kvn_f[:, 4 * KV_NSA * HEAD_DIM:]
        win_p = new_w[:n_p].reshape(B, T, 2, KV_NSA, HEAD_DIM)[:, T - min(WINDOW, T):]
        win_s = jnp.concatenate([cache_nsa_win[l], new_w[n_p:].reshape(Bs, Ts, 2, KV_NSA, HEAD_DIM)],
                                axis=1)[:, -cache_nsa_win.shape[2]:]
        st_p.append((new_d[:n_p].reshape(B, T, 2, KV_DIFF, 2 * HEAD_DIM), new_n[:n_p].reshape(B, T, 4, KV_NSA, HEAD_DIM),
                     win_p, kvf_f[:n_p].reshape(B, T, 2, KV_FOX, HEAD_DIM), logf[:n_p].reshape(B, T, H_FOX)))
        st_s.append((new_d[n_p:].reshape(Bs, Ts, 2, KV_DIFF, 2 * HEAD_DIM), new_n[n_p:].reshape(Bs, Ts, 4, KV_NSA, HEAD_DIM),
                     win_s, kvf_f[n_p:].reshape(Bs, Ts, 2, KV_FOX, HEAD_DIM), logf[n_p:].reshape(Bs, Ts, H_FOX)))

        o_d = jnp.concatenate([o_d_p, o_d_s.astype(BF16)], axis=0)
        o_n = jnp.concatenate([o_n_p, o_n_s.astype(BF16)], axis=0)
        o_f = jnp.concatenate([o_f_p, o_f_s.astype(BF16)], axis=0)
        w_gate = wl[:, o_gm:].reshape(D, 3, D).transpose(1, 0, 2).astype(BF16)
        b_gate = bl[:, o_gm:].reshape(3, 1, D)
        merged = merge_branches(x_bf, o_d, o_n, o_f, w_gate, b_gate, w_branch_diff[l], w_branch_nsa[l], w_branch_fox[l])
        y = matmul_bias(merged, w_out[l], zero_bias, tm=512, tn=512, name="out_proj")
        h, h_bf = residual_layer_norm(x, y, ln1_g[l][None, :], ln1_b[l][None, :])
        f = moe_ffn(h, h_bf, w_router[l], b_router[l], w_gate_up[l], b_gate_up[l], w_down[l], b_down[l])
        x, x_bf = residual_layer_norm(h, f, ln2_g[l][None, :], ln2_b[l][None, :])
    y_prompt = x[:n_p].reshape(B, T, D)
    y_sample = x[n_p:].reshape(Bs, Ts, D)
    stack = lambda sts, k: jnp.stack([s[k] for s in sts], 0)
    return (y_prompt, y_sample,
            stack(st_p, 0), stack(st_p, 1), stack(st_p, 2), stack(st_p, 3), stack(st_p, 4),
            stack(st_s, 0), stack(st_s, 1), stack(st_s, 2), stack(st_s, 3), stack(st_s, 4))
```

```python
import functools
import math

import numpy as np
import jax
import jax.numpy as jnp
from jax import lax
from jax.experimental import pallas as pl
from jax.experimental.pallas import tpu as pltpu

D_MODEL = 4096
BATCH = 4
SEQ = 2048
DEPTH = 2
DEC_BATCH = 8
DEC_SEQ = 4
PAST_LEN = 16384
PAGE_SIZE = 128

HEAD_DIM = 128
ROPE_THETA = 10000.0
H_DIFF = 4
KV_DIFF = 2
H_NSA = 12
KV_NSA = 2
H_FOX = 12
KV_FOX = 4
W_DIFF = H_DIFF * 2 * HEAD_DIM
W_NSA = H_NSA * HEAD_DIM
W_FOX = H_FOX * HEAD_DIM
CMP_BLOCK = 32
CMP_STRIDE = 16
SEL_BLOCK = 64
SEL_TOPK = 16
WINDOW = 512
N_EXPERTS = 32
TOP_K = 4
D_EXPERT = D_MODEL // 2
SWIGLU_ALPHA = 1.702
SWIGLU_LIMIT = 7.0
LN_EPS = 1e-5
DEEPNORM_ALPHA = (2 * DEPTH) ** 0.25
ATTN_SCALE = HEAD_DIM ** -0.5
IN_SIZES = (
    H_DIFF * 2 * HEAD_DIM,
    2 * KV_DIFF * 2 * HEAD_DIM,
    H_NSA * HEAD_DIM,
    6 * KV_NSA * HEAD_DIM,
    3 * H_NSA,
    H_FOX * HEAD_DIM,
    2 * KV_FOX * HEAD_DIM,
    H_FOX,
    3 * D_MODEL,
)
N_IN = sum(IN_SIZES)
IN_OFFS = tuple(int(v) for v in np.cumsum((0,) + IN_SIZES))
N_PAGES = PAST_LEN // PAGE_SIZE

F32 = jnp.float32
BF16 = jnp.bfloat16
NEG_BIG = -1e30
LANES = 128
SUBLANES = 8

V7X_VMEM_LIMIT_BYTES = 48 * 1024 * 1024
Q_TILE = 128
MOE_TILE = 256
TOK_PAD = SUBLANES
NEW_PAD = LANES
ROW_UNITS = 8
PAGES_PER_STEP = 8
CMP_PAGES_PER_STEP = 16

assert DEC_SEQ <= TOK_PAD and HEAD_DIM == LANES and PAGE_SIZE % SEL_BLOCK == 0
assert N_PAGES % PAGES_PER_STEP == 0 and N_PAGES % CMP_PAGES_PER_STEP == 0
assert CMP_BLOCK == 2 * CMP_STRIDE and PAGE_SIZE % CMP_STRIDE == 0 and SEQ % CMP_STRIDE == 0
assert ((PAST_LEN + DEC_SEQ - CMP_BLOCK) // CMP_STRIDE) * CMP_STRIDE + CMP_BLOCK <= PAST_LEN
assert PAST_LEN % SEL_BLOCK == 0 and DEC_SEQ <= SEL_BLOCK and WINDOW <= PAST_LEN and SEQ >= WINDOW


def _cparams(n_axes):
    return pltpu.CompilerParams(dimension_semantics=("arbitrary",) * n_axes,
                                vmem_limit_bytes=V7X_VMEM_LIMIT_BYTES)


def _cast_weight_tile(w_ref, wbf_ref, rows_per_chunk=256):
    n_chunks = w_ref.shape[0] // rows_per_chunk

    def body(c, carry):
        r = pl.multiple_of(c * rows_per_chunk, rows_per_chunk)
        wbf_ref[pl.ds(r, rows_per_chunk), :] = w_ref[pl.ds(r, rows_per_chunk), :].astype(BF16)
        return carry

    lax.fori_loop(0, n_chunks, body, 0)


def _mm_body(x_ref, w_ref, b_ref, o_ref, wbf_ref):
    @pl.when(pl.program_id(1) == 0)
    def _():
        _cast_weight_tile(w_ref, wbf_ref)

    acc = jnp.dot(x_ref[...], wbf_ref[...], preferred_element_type=F32)
    o_ref[...] = (acc + b_ref[...]).astype(o_ref.dtype)


def _weight_spec(w, layer, k_rows, tn, col_block):
    if w.ndim == 2:
        return pl.BlockSpec((k_rows, tn), lambda j, i: (0, col_block(j)))
    return pl.BlockSpec((None, k_rows, tn), lambda j, i: (layer, 0, col_block(j)))


def matmul_bias(x, w, b, *, tm, tn, layer=0, out_dtype=F32, name="matmul_bias"):
    M, K = x.shape
    N = w.shape[-1]
    tn = min(tn, N)
    tm = min(tm, M)
    return pl.pallas_call(
        _mm_body,
        grid=(pl.cdiv(N, tn), pl.cdiv(M, tm)),
        in_specs=[pl.BlockSpec((tm, K), lambda j, i: (i, 0)),
                  _weight_spec(w, layer, K, tn, lambda j: j),
                  pl.BlockSpec((1, tn), lambda j, i: (0, j))],
        out_specs=pl.BlockSpec((tm, tn), lambda j, i: (i, j)),
        out_shape=jax.ShapeDtypeStruct((M, N), out_dtype),
        scratch_shapes=[pltpu.VMEM((K, tn), BF16)],
        compiler_params=_cparams(2),
        name=name,
    )(x, w, b)


def _proj_body(x_ref, w_ref, b_ref, flag_ref, cos_ref, sin_ref, *rest, with_f32, use_rope):
    if with_f32:
        of_ref, ob_ref, wbf_ref = rest
    else:
        ob_ref, wbf_ref = rest

    @pl.when(pl.program_id(1) == 0)
    def _():
        _cast_weight_tile(w_ref, wbf_ref)

    acc = jnp.dot(x_ref[...], wbf_ref[...], preferred_element_type=F32) + b_ref[...]
    for u in range(acc.shape[1] // LANES):
        cols = slice(u * LANES, (u + 1) * LANES)
        z = acc[:, cols]
        if use_rope:
            f = flag_ref[:, cols]
            cos = 1.0 + f * (cos_ref[...] - 1.0)
            sin = f * sin_ref[...]
            z = z * cos + pltpu.roll(z, HEAD_DIM // 2, axis=1) * sin
        if with_f32:
            of_ref[:, cols] = z
        ob_ref[:, cols] = z.astype(BF16)


def project(x, w, b, col0, n_cols, rope_flag, cos, sin, *, with_f32, name, layer=0, tm=512, tn=512):
    M, K = x.shape
    assert col0 % tn == 0 and n_cols % tn == 0
    j0 = col0 // tn
    use_rope = rope_flag is not None
    if not use_rope:
        rope_flag = jnp.zeros((1, n_cols), F32)
    kern = functools.partial(_proj_body, with_f32=with_f32, use_rope=use_rope)
    out_block = pl.BlockSpec((tm, tn), lambda j, i: (i, j))
    outs = pl.pallas_call(
        kern,
        grid=(n_cols // tn, pl.cdiv(M, tm)),
        in_specs=[pl.BlockSpec((tm, K), lambda j, i: (i, 0)),
                  _weight_spec(w, layer, K, tn, lambda j: j + j0),
                  pl.BlockSpec((1, tn), lambda j, i: (0, j + j0)),
                  pl.BlockSpec((1, tn), lambda j, i: (0, j)),
                  pl.BlockSpec((tm, LANES), lambda j, i: (i, 0)),
                  pl.BlockSpec((tm, LANES), lambda j, i: (i, 0))],
        out_specs=[out_block, out_block] if with_f32 else [out_block],
        out_shape=([jax.ShapeDtypeStruct((M, n_cols), F32)] if with_f32 else [])
        + [jax.ShapeDtypeStruct((M, n_cols), BF16)],
        scratch_shapes=[pltpu.VMEM((K, tn), BF16)],
        compiler_params=_cparams(2),
        name=name,
    )(x, w, b, rope_flag, cos, sin)
    return (outs[0], outs[1]) if with_f32 else (None, outs[0])


def _realign_body(a_ref, b_ref, o_ref, *, shift):
    w = jnp.concatenate([a_ref[...], b_ref[...]], axis=1)
    width = w.shape[1]
    o_ref[...] = pltpu.roll(w, width - shift, axis=1)[:, :o_ref.shape[1]].astype(o_ref.dtype)


def realign_columns(w, layer, col0, n_cols, out_dtype, *, tr=512, tn=512):
    _, K, _ = w.shape
    tr = min(tr, K)
    aligned = col0 // tn * tn
    shift = col0 - aligned
    assert shift < LANES and n_cols % tn == 0 and K % tr == 0
    ja, jb, units = aligned // tn, aligned // LANES, tn // LANES
    return pl.pallas_call(
        functools.partial(_realign_body, shift=shift),
        grid=(n_cols // tn, K // tr),
        in_specs=[pl.BlockSpec((None, tr, tn), lambda j, r: (layer, r, ja + j)),
                  pl.BlockSpec((None, tr, LANES), lambda j, r: (layer, r, jb + (j + 1) * units))],
        out_specs=pl.BlockSpec((tr, tn), lambda j, r: (r, j)),
        out_shape=jax.ShapeDtypeStruct((K, n_cols), out_dtype),
        compiler_params=_cparams(2),
        name="realign_columns",
    )(w, w)


def _merge_body(x_ref, od_ref, on_ref, of_ref, wg0_ref, wg1_ref, wg2_ref, bg_ref, wd_ref, wn_ref, wf_ref, o_ref,
                wdb, wnb, wfb):
    @pl.when(pl.program_id(1) == 0)
    def _():
        _cast_weight_tile(wd_ref, wdb)
        _cast_weight_tile(wn_ref, wnb)
        _cast_weight_tile(wf_ref, wfb)

    x = x_ref[...]
    acc = None
    for k, (wg_k, o_k, w_k) in enumerate(((wg0_ref, od_ref, wdb), (wg1_ref, on_ref, wnb), (wg2_ref, of_ref, wfb))):
        gate = jax.nn.sigmoid(jnp.dot(x, wg_k[...], preferred_element_type=F32) + bg_ref[k])
        term = gate * jnp.dot(o_k[...], w_k[...], preferred_element_type=F32)
        acc = term if acc is None else acc + term
    o_ref[...] = acc.astype(o_ref.dtype)


def merge_branches(x, o_d, o_n, o_f, w_gate, b_gate, w_d, w_n, w_f, layer, *, tm=256, tn=256):
    M, D = x.shape
    tn = min(tn, D)
    nj = D // tn
    gate_spec = lambda k: pl.BlockSpec((D, tn), lambda j, i: (0, k * nj + j))
    return pl.pallas_call(
        _merge_body,
        grid=(nj, pl.cdiv(M, tm)),
        in_specs=[pl.BlockSpec((tm, D), lambda j, i: (i, 0)),
                  pl.BlockSpec((tm, W_DIFF), lambda j, i: (i, 0)),
                  pl.BlockSpec((tm, W_NSA), lambda j, i: (i, 0)),
                  pl.BlockSpec((tm, W_FOX), lambda j, i: (i, 0)),
                  gate_spec(0), gate_spec(1), gate_spec(2),
                  pl.BlockSpec((3, 1, tn), lambda j, i: (0, 0, j)),
                  _weight_spec(w_d, layer, W_DIFF, tn, lambda j: j),
                  _weight_spec(w_n, layer, W_NSA, tn, lambda j: j),
                  _weight_spec(w_f, layer, W_FOX, tn, lambda j: j)],
        out_specs=pl.BlockSpec((tm, tn), lambda j, i: (i, j)),
        out_shape=jax.ShapeDtypeStruct((M, D), BF16),
        scratch_shapes=[pltpu.VMEM((W_DIFF, tn), BF16), pltpu.VMEM((W_NSA, tn), BF16),
                        pltpu.VMEM((W_FOX, tn), BF16)],
        compiler_params=_cparams(2),
        name="merge_branches",
    )(x, o_d, o_n, o_f, w_gate, w_gate, w_gate, b_gate, w_d, w_n, w_f)


def _ln_body(x_ref, y_ref, g_ref, b_ref, o_ref, obf_ref):
    v = DEEPNORM_ALPHA * x_ref[...] + y_ref[...]
    mu = jnp.mean(v, -1, keepdims=True)
    c = v - mu
    var = jnp.mean(c * c, -1, keepdims=True)
    out = c * lax.rsqrt(var + LN_EPS) * g_ref[...] + b_ref[...]
    o_ref[...] = out
    obf_ref[...] = out.astype(BF16)


def residual_layer_norm(x, y, g, b, *, tm=256):
    M, D = x.shape
    return pl.pallas_call(
        _ln_body,
        grid=(pl.cdiv(M, tm),),
        in_specs=[pl.BlockSpec((tm, D), lambda i: (i, 0)),
                  pl.BlockSpec((tm, D), lambda i: (i, 0)),
                  pl.BlockSpec((1, D), lambda i: (0, 0)),
                  pl.BlockSpec((1, D), lambda i: (0, 0))],
        out_specs=[pl.BlockSpec((tm, D), lambda i: (i, 0)),
                   pl.BlockSpec((tm, D), lambda i: (i, 0))],
        out_shape=[jax.ShapeDtypeStruct((M, D), F32), jax.ShapeDtypeStruct((M, D), BF16)],
        compiler_params=_cparams(1),
        name="residual_layer_norm",
    )(x, y, g, b)


def _combine_ln_body(x_ref, *refs):
    y_refs = refs[:TOP_K]
    gate_ref, g_ref, b_ref, o_ref, obf_ref = refs[TOP_K:]
    gate = gate_ref[...]
    f = None
    for k in range(TOP_K):
        term = gate[:, k:k + 1] * y_refs[k][...]
        f = term if f is None else f + term
    v = DEEPNORM_ALPHA * x_ref[...] + f
    mu = jnp.mean(v, -1, keepdims=True)
    c = v - mu
    var = jnp.mean(c * c, -1, keepdims=True)
    out = c * lax.rsqrt(var + LN_EPS) * g_ref[...] + b_ref[...]
    o_ref[...] = out
    obf_ref[...] = out.astype(BF16)


COMBINE_TILE = 128


def combine_layer_norm(x, y_picks, gates, g, b):
    M, D = x.shape
    tm = COMBINE_TILE
    pick_spec = lambda k: pl.BlockSpec((None, tm, D), lambda i: (k, i, 0))
    return pl.pallas_call(
        _combine_ln_body,
        grid=(pl.cdiv(M, tm),),
        in_specs=[pl.BlockSpec((tm, D), lambda i: (i, 0))] + [pick_spec(k) for k in range(TOP_K)] + [
                  pl.BlockSpec((tm, TOP_K), lambda i: (i, 0)),
                  pl.BlockSpec((1, D), lambda i: (0, 0)),
                  pl.BlockSpec((1, D), lambda i: (0, 0))],
        out_specs=[pl.BlockSpec((tm, D), lambda i: (i, 0)),
                   pl.BlockSpec((tm, D), lambda i: (i, 0))],
        out_shape=[jax.ShapeDtypeStruct((M, D), F32), jax.ShapeDtypeStruct((M, D), BF16)],
        compiler_params=_cparams(1),
        name="moe_combine_layer_norm",
    )(x, *([y_picks] * TOP_K), gates, g, b)


def _softmax_rows(s, mask):
    s = jnp.where(mask, s, NEG_BIG)
    m = jnp.max(s, -1, keepdims=True)
    p = jnp.where(mask, jnp.exp(s - m), 0.0)
    d = jnp.sum(p, -1, keepdims=True)
    return p * (1.0 / jnp.where(d > 0, d, 1.0))


def _dot_nt(a, b, precision=None):
    return lax.dot_general(a, b, (((1,), (1,)), ((), ())), preferred_element_type=F32, precision=precision)


def _scores(q, k):
    return _dot_nt(q, k) * ATTN_SCALE


def _stack_heads(q, units):
    return jnp.concatenate([q[:, u * LANES:(u + 1) * LANES] for u in units], axis=0)


def _online_step(state, s, mask, v):
    m, l, acc = state
    if mask is not None:
        s = jnp.where(mask, s, NEG_BIG)
    m_new = jnp.maximum(m, jnp.max(s, -1, keepdims=True))
    p = jnp.exp(s - m_new)
    if mask is not None:
        p = jnp.where(mask, p, 0.0)
    alpha = jnp.exp(m - m_new)
    l_new = alpha * l + jnp.sum(p, -1, keepdims=True)
    acc_new = alpha * acc + jnp.dot(p.astype(BF16), v, preferred_element_type=F32)
    return m_new, l_new, acc_new


def _online_finish(state):
    _, l, acc = state
    return acc * (1.0 / jnp.where(l > 0, l, 1.0))


def _load_state(m_ref, l_ref, acc_ref, idx):
    return m_ref[idx][:, :1], l_ref[idx][:, :1], acc_ref[idx]


def _store_state(m_ref, l_ref, acc_ref, idx, state):
    m, l, acc = state
    m_ref[idx] = jnp.broadcast_to(m, m_ref.shape[1:])
    l_ref[idx] = jnp.broadcast_to(l, l_ref.shape[1:])
    acc_ref[idx] = acc


def _init_state(m_ref, l_ref, acc_ref):
    m_ref[...] = jnp.full(m_ref.shape, NEG_BIG, F32)
    l_ref[...] = jnp.zeros(l_ref.shape, F32)
    acc_ref[...] = jnp.zeros(acc_ref.shape, F32)


def _page_unit(page_refs, unit):
    return jnp.concatenate([r[pl.ds(unit, PAGE_SIZE, stride=ROW_UNITS), :] for r in page_refs], axis=0).astype(BF16)


def _token_of_row(shape):
    return lax.broadcasted_iota(jnp.int32, shape, 0) % TOK_PAD


def _new_key_mask(n_rows):
    t = _token_of_row((n_rows, NEW_PAD))
    j = lax.broadcasted_iota(jnp.int32, (n_rows, NEW_PAD), 1)
    return (j <= t) & (j < DEC_SEQ)


def _subln(o, g_ref, post):
    return o * lax.rsqrt(jnp.mean(o * o, -1, keepdims=True) + LN_EPS) * g_ref[...] * post


def _diff_prompt_body(sc_ref, q_ref, k_ref, v_ref, g_ref, o_ref, *, hg, tq):
    qi = pl.program_id(2)
    T = k_ref.shape[0]
    lam = sc_ref[0]
    post = sc_ref[1]
    qpos = qi * tq + lax.broadcasted_iota(jnp.int32, (tq, T), 0)
    kpos = lax.broadcasted_iota(jnp.int32, (tq, T), 1)
    mask = (kpos <= qpos)[None]
    q = q_ref[...]
    k = k_ref[...]

    def probs(c):
        qc = _stack_heads(q, [h * 2 + c for h in range(hg)])
        s = _scores(qc, k[:, c * LANES:(c + 1) * LANES]).reshape(hg, tq, T)
        return _softmax_rows(s, mask)

    a = probs(0) - lam * probs(1)
    o = jnp.dot(a.astype(BF16).reshape(hg * tq, T), v_ref[...], preferred_element_type=F32)
    o = _subln(o, g_ref, post)
    for h in range(hg):
        o_ref[:, h * 2 * HEAD_DIM:(h + 1) * 2 * HEAD_DIM] = o[h * tq:(h + 1) * tq].astype(o_ref.dtype)


def diff_attention_prompt(q, kv, scalars, subln_g, n_batch, T, *, tq=Q_TILE):
    hg = H_DIFF // KV_DIFF
    d2 = 2 * HEAD_DIM
    nq = T // tq
    kern = functools.partial(_diff_prompt_body, hg=hg, tq=tq)
    return pl.pallas_call(
        kern,
        grid=(n_batch, KV_DIFF, nq),
        in_specs=[pl.BlockSpec(memory_space=pltpu.SMEM),
                  pl.BlockSpec((tq, hg * d2), lambda b, g, i: (b * nq + i, g)),
                  pl.BlockSpec((T, d2), lambda b, g, i: (b, g)),
                  pl.BlockSpec((T, d2), lambda b, g, i: (b, KV_DIFF + g)),
                  pl.BlockSpec((1, d2), lambda b, g, i: (0, 0))],
        out_specs=pl.BlockSpec((tq, hg * d2), lambda b, g, i: (b * nq + i, g)),
        out_shape=jax.ShapeDtypeStruct((n_batch * T, W_DIFF), BF16),
        compiler_params=_cparams(3),
        name="diff_attention_prompt",
    )(scalars, q, kv, kv, subln_g)


def _diff_sample_body(pt_ref, *refs, hg):
    pages = refs[:PAGES_PER_STEP]
    sc_ref, q_ref, new_ref, g_ref, o_ref, m_ref, l_ref, acc_ref = refs[PAGES_PER_STEP:]
    s_id = pl.program_id(1)
    n_rows = hg * TOK_PAD

    @pl.when(s_id == 0)
    def _():
        _init_state(m_ref, l_ref, acc_ref)

    for g in range(KV_DIFF):
        v = jnp.concatenate([_page_unit(pages, 2 * KV_DIFF + g), _page_unit(pages, 3 * KV_DIFF + g)], axis=1)
        for c in range(2):
            idx = g * 2 + c
            s = _scores(q_ref[g, c], _page_unit(pages, c * KV_DIFF + g))
            state = _online_step(_load_state(m_ref, l_ref, acc_ref, idx), s, None, v)
            _store_state(m_ref, l_ref, acc_ref, idx, state)

    @pl.when(s_id == pl.num_programs(1) - 1)
    def _():
        lam = sc_ref[0]
        post = sc_ref[1]
        mask = _new_key_mask(n_rows)
        new = new_ref[...]
        for g in range(KV_DIFF):
            v = new[:, (KV_DIFF + g) * 2 * LANES:(KV_DIFF + g + 1) * 2 * LANES]
            outs = []
            for c in range(2):
                idx = g * 2 + c
                k = new[:, idx * LANES:(idx + 1) * LANES]
                state = _online_step(_load_state(m_ref, l_ref, acc_ref, idx), _scores(q_ref[g, c], k), mask, v)
                outs.append(_online_finish(state))
            o_ref[g] = _subln(outs[0] - lam * outs[1], g_ref, post)


def _cache_rows(cache):
    assert math.prod(cache.shape[3:]) == ROW_UNITS * LANES and cache.shape[2] == PAGE_SIZE
    return cache.reshape(-1, LANES)


def _page_specs(page0, n_pages_per_step):
    def spec(k):
        return pl.BlockSpec((PAGE_SIZE * ROW_UNITS, LANES),
                            lambda b, s, pt, *_: (page0 + pt[b * N_PAGES + s * n_pages_per_step + k], 0))
    return [spec(k) for k in range(n_pages_per_step)]


def diff_attention_sample(pt_flat, scalars, cache, page0, q, new_kv, subln_g):
    Bs = q.shape[0]
    hg = H_DIFF // KV_DIFF
    n_rows = hg * TOK_PAD
    width = new_kv.shape[2]
    d2 = 2 * HEAD_DIM
    grid_spec = pltpu.PrefetchScalarGridSpec(
        num_scalar_prefetch=1,
        grid=(Bs, N_PAGES // PAGES_PER_STEP),
        in_specs=_page_specs(page0, PAGES_PER_STEP) + [
            pl.BlockSpec(memory_space=pltpu.SMEM),
            pl.BlockSpec((None, KV_DIFF, 2, n_rows, HEAD_DIM), lambda b, s, *_: (b, 0, 0, 0, 0)),
            pl.BlockSpec((None, NEW_PAD, width), lambda b, s, *_: (b, 0, 0)),
            pl.BlockSpec((1, d2), lambda b, s, *_: (0, 0))],
        out_specs=pl.BlockSpec((None, KV_DIFF, n_rows, d2), lambda b, s, *_: (b, 0, 0, 0)),
        scratch_shapes=[pltpu.VMEM((KV_DIFF * 2, n_rows, LANES), F32), pltpu.VMEM((KV_DIFF * 2, n_rows, LANES), F32),
                        pltpu.VMEM((KV_DIFF * 2, n_rows, d2), F32)],
    )
    return pl.pallas_call(
        functools.partial(_diff_sample_body, hg=hg),
        grid_spec=grid_spec,
        out_shape=jax.ShapeDtypeStruct((Bs, KV_DIFF, n_rows, d2), F32),
        compiler_params=_cparams(2),
        name="diff_attention_sample",
    )(pt_flat, *([cache] * PAGES_PER_STEP), scalars, q, new_kv, subln_g)


def _fox_prompt_body(q_ref, k_ref, v_ref, cq_ref, ck_ref, o_ref, *, hg, tq):
    qi = pl.program_id(2)
    T = k_ref.shape[0]
    qpos = qi * tq + lax.broadcasted_iota(jnp.int32, (tq, T), 0)
    kpos = lax.broadcasted_iota(jnp.int32, (tq, T), 1)
    mask = (kpos <= qpos)[None]
    q = _stack_heads(q_ref[...], range(hg))
    cq = cq_ref[...]
    ck = ck_ref[...]
    bias = jnp.concatenate([cq[:, h:h + 1] - ck[h:h + 1, :] for h in range(hg)], axis=0)
    s = (_scores(q, k_ref[...]) + bias).reshape(hg, tq, T)
    p = _softmax_rows(s, mask)
    o = jnp.dot(p.astype(BF16).reshape(hg * tq, T), v_ref[...], preferred_element_type=F32)
    for h in range(hg):
        o_ref[:, h * HEAD_DIM:(h + 1) * HEAD_DIM] = o[h * tq:(h + 1) * tq].astype(o_ref.dtype)


def fox_attention_prompt(q, kv, c_col, c_row, n_batch, T, *, tq=Q_TILE):
    hg = H_FOX // KV_FOX
    nq = T // tq
    kern = functools.partial(_fox_prompt_body, hg=hg, tq=tq)
    return pl.pallas_call(
        kern,
        grid=(n_batch, KV_FOX, nq),
        in_specs=[pl.BlockSpec((tq, hg * HEAD_DIM), lambda b, g, i: (b * nq + i, g)),
                  pl.BlockSpec((T, HEAD_DIM), lambda b, g, i: (b, g)),
                  pl.BlockSpec((T, HEAD_DIM), lambda b, g, i: (b, KV_FOX + g)),
                  pl.BlockSpec((None, tq, hg), lambda b, g, i: (g, b * nq + i, 0)),
                  pl.BlockSpec((None, None, hg, T), lambda b, g, i: (b, g, 0, 0))],
        out_specs=pl.BlockSpec((tq, hg * HEAD_DIM), lambda b, g, i: (b * nq + i, g)),
        out_shape=jax.ShapeDtypeStruct((n_batch * T, W_FOX), BF16),
        compiler_params=_cparams(3),
        name="fox_attention_prompt",
    )(q, kv, kv, c_col, c_row)


def _head_rows(c, hg):
    return jnp.concatenate([jnp.broadcast_to(c[h:h + 1, :], (TOK_PAD, c.shape[1])) for h in range(hg)], axis=0)


def _fox_sample_body(pt_ref, *refs, hg):
    pages = refs[:PAGES_PER_STEP]
    q_ref, ckp_ref, cq_ref, ckn_ref, new_ref, o_ref, m_ref, l_ref, acc_ref = refs[PAGES_PER_STEP:]
    s_id = pl.program_id(1)
    n_rows = hg * TOK_PAD

    @pl.when(s_id == 0)
    def _():
        _init_state(m_ref, l_ref, acc_ref)

    for g in range(KV_FOX):
        bias = cq_ref[g] - _head_rows(ckp_ref[g], hg)
        s = _scores(q_ref[g], _page_unit(pages, g)) + bias
        state = _online_step(_load_state(m_ref, l_ref, acc_ref, g), s, None, _page_unit(pages, KV_FOX + g))
        _store_state(m_ref, l_ref, acc_ref, g, state)

    @pl.when(s_id == pl.num_programs(1) - 1)
    def _():
        mask = _new_key_mask(n_rows)
        new = new_ref[...]
        for g in range(KV_FOX):
            bias = cq_ref[g] - _head_rows(ckn_ref[g], hg)
            s = _scores(q_ref[g], new[:, g * LANES:(g + 1) * LANES]) + bias
            v = new[:, (KV_FOX + g) * LANES:(KV_FOX + g + 1) * LANES]
            o_ref[g] = _online_finish(_online_step(_load_state(m_ref, l_ref, acc_ref, g), s, mask, v))


def fox_attention_sample(pt_flat, cache, page0, q, ck_past, cq_col, ck_new, new_kv):
    Bs = q.shape[0]
    hg = H_FOX // KV_FOX
    n_rows = hg * TOK_PAD
    width = new_kv.shape[2]
    keys_per_step = PAGES_PER_STEP * PAGE_SIZE
    grid_spec = pltpu.PrefetchScalarGridSpec(
        num_scalar_prefetch=1,
        grid=(Bs, N_PAGES // PAGES_PER_STEP),
        in_specs=_page_specs(page0, PAGES_PER_STEP) + [
            pl.BlockSpec((None, KV_FOX, n_rows, HEAD_DIM), lambda b, s, *_: (b, 0, 0, 0)),
            pl.BlockSpec((None, KV_FOX, hg, keys_per_step), lambda b, s, *_: (b, 0, 0, s)),
            pl.BlockSpec((None, KV_FOX, n_rows, 1), lambda b, s, *_: (b, 0, 0, 0)),
            pl.BlockSpec((None, KV_FOX, hg, NEW_PAD), lambda b, s, *_: (b, 0, 0, 0)),
            pl.BlockSpec((None, NEW_PAD, width), lambda b, s, *_: (b, 0, 0))],
        out_specs=pl.BlockSpec((None, KV_FOX, n_rows, HEAD_DIM), lambda b, s, *_: (b, 0, 0, 0)),
        scratch_shapes=[pltpu.VMEM((KV_FOX, n_rows, LANES), F32), pltpu.VMEM((KV_FOX, n_rows, LANES), F32),
                        pltpu.VMEM((KV_FOX, n_rows, HEAD_DIM), F32)],
    )
    return pl.pallas_call(
        functools.partial(_fox_sample_body, hg=hg),
        grid_spec=grid_spec,
        out_shape=jax.ShapeDtypeStruct((Bs, KV_FOX, n_rows, HEAD_DIM), F32),
        compiler_params=_cparams(2),
        name="fox_attention_sample",
    )(pt_flat, *([cache] * PAGES_PER_STEP), q, ck_past, cq_col, ck_new, new_kv)


def _gelu_tanh(x):
    return 0.5 * x * (1.0 + jnp.tanh(math.sqrt(2.0 / math.pi) * (x + 0.044715 * x * x * x)))


def _compress_body(*refs, n_src, paged):
    if paged:
        refs = refs[1:]
    srcs = refs[:n_src]
    pe_ref, w1_ref, w2_ref, o_ref, a_ref, b_ref = refs[n_src:]
    s_id = pl.program_id(1)
    n_units = 2 * KV_NSA
    col = pl.multiple_of(s_id * LANES, LANES)

    def chunk_rows(i, u):
        if paged:
            return jnp.concatenate(
                [src[pl.ds(i * ROW_UNITS + u, PAGE_SIZE // CMP_STRIDE, stride=CMP_STRIDE * ROW_UNITS), :]
                 for src in srcs], axis=0)
        return srcs[u][pl.ds(i, LANES, stride=CMP_STRIDE), :]

    for u in range(n_units):
        kv = u // KV_NSA
        for half, dst in ((0, a_ref), (1, b_ref)):
            pieces = [chunk_rows(i, u) + pe_ref[kv, half * CMP_STRIDE + i:half * CMP_STRIDE + i + 1, :]
                      for i in range(CMP_STRIDE)]
            x = jnp.concatenate(pieces, axis=1).astype(BF16)
            dst[u, :, pl.ds(col, LANES)] = _dot_nt(w1_ref[kv, half], x)

    @pl.when(s_id == pl.num_programs(1) - 1)
    def _():
        n_chunks = a_ref.shape[2]
        for u in range(n_units):
            kv = u // KV_NSA
            y = a_ref[u] + pltpu.roll(b_ref[u], n_chunks - 1, axis=1)
            o_ref[u] = jnp.dot(w2_ref[kv], _gelu_tanh(y).astype(BF16), preferred_element_type=F32).astype(BF16)


def _compress_weights(pe, w1, w2):
    half = CMP_STRIDE * HEAD_DIM
    w1t = jnp.stack([jnp.stack([w1[kv, :half].T, w1[kv, half:].T]) for kv in range(2)]).astype(BF16)
    w2t = jnp.transpose(w2, (0, 2, 1)).astype(BF16)
    return pe, w1t, w2t


def compress_prompt(src, pe, w1t, w2t, n_batch, T):
    n_chunks = T // CMP_STRIDE
    assert n_chunks == LANES
    n_units = 2 * KV_NSA
    kern = functools.partial(_compress_body, n_src=n_units, paged=False)
    full3 = lambda b, s: (0, 0, 0)
    unit_spec = lambda u: pl.BlockSpec((T, HEAD_DIM), lambda b, s: (b, u))
    return pl.pallas_call(
        kern,
        grid=(n_batch, 1),
        in_specs=[unit_spec(u) for u in range(n_units)] + [
                  pl.BlockSpec(pe.shape, full3),
                  pl.BlockSpec(w1t.shape, lambda b, s: (0, 0, 0, 0)),
                  pl.BlockSpec(w2t.shape, full3)],
        out_specs=pl.BlockSpec((None, 2 * KV_NSA, HEAD_DIM, n_chunks), lambda b, s: (b, 0, 0, 0)),
        out_shape=jax.ShapeDtypeStruct((n_batch, 2 * KV_NSA, HEAD_DIM, n_chunks), BF16),
        scratch_shapes=[pltpu.VMEM((2 * KV_NSA, HEAD_DIM, n_chunks), F32),
                        pltpu.VMEM((2 * KV_NSA, HEAD_DIM, n_chunks), F32)],
        compiler_params=_cparams(2),
        name="nsa_compress_prompt",
    )(*([src] * n_units), pe, w1t, w2t)


def compress_paged(pt_flat, cache, page0, pe, w1t, w2t, n_batch):
    chunks_per_page = PAGE_SIZE // CMP_STRIDE
    n_chunks = PAST_LEN // CMP_STRIDE
    assert chunks_per_page * CMP_PAGES_PER_STEP == LANES
    kern = functools.partial(_compress_body, n_src=CMP_PAGES_PER_STEP, paged=True)
    grid_spec = pltpu.PrefetchScalarGridSpec(
        num_scalar_prefetch=1,
        grid=(n_batch, N_PAGES // CMP_PAGES_PER_STEP),
        in_specs=_page_specs(page0, CMP_PAGES_PER_STEP) + [
            pl.BlockSpec(pe.shape, lambda b, s, pt: (0, 0, 0)),
            pl.BlockSpec(w1t.shape, lambda b, s, pt: (0, 0, 0, 0)),
            pl.BlockSpec(w2t.shape, lambda b, s, pt: (0, 0, 0))],
        out_specs=pl.BlockSpec((None, 2 * KV_NSA, HEAD_DIM, n_chunks), lambda b, s, pt: (b, 0, 0, 0)),
        scratch_shapes=[pltpu.VMEM((2 * KV_NSA, HEAD_DIM, n_chunks), F32),
                        pltpu.VMEM((2 * KV_NSA, HEAD_DIM, n_chunks), F32)],
    )
    return pl.pallas_call(
        kern,
        grid_spec=grid_spec,
        out_shape=jax.ShapeDtypeStruct((n_batch, 2 * KV_NSA, HEAD_DIM, n_chunks), BF16),
        compiler_params=_cparams(2),
        name="nsa_compress_paged",
    )(pt_flat, *([cache] * CMP_PAGES_PER_STEP), pe, w1t, w2t)


def _block_keys(score_t, cur):
    jj = lax.broadcasted_iota(jnp.int32, score_t.shape, 0)
    valid = jj <= cur
    forced = (jj == 0) | (jj == cur) | (jj == cur - 1)
    return jnp.where(valid, jnp.where(forced, jnp.inf, score_t), -jnp.inf), valid


def _select_blocks(score_t, cur, n_sel):
    key, valid = _block_keys(score_t, cur)
    jj = lax.broadcasted_iota(jnp.int32, score_t.shape, 0)
    rank = jnp.zeros(score_t.shape, F32)
    for i in range(n_sel):
        ki = key[i:i + 1, :]
        beats = (ki > key) | ((ki == key) & (jj > i))
        rank = rank + jnp.where(beats, 1.0, 0.0)
    return jnp.where(valid & (rank < float(min(SEL_TOPK, n_sel))), 1.0, 0.0)


def _cover_matrix(n_sel_rows, n_cmp_cols, n_cmp, n_sel):
    cs = np.arange(n_cmp_cols) * CMP_STRIDE
    ss = np.arange(n_sel_rows) * SEL_BLOCK
    cover = (cs[None, :] < ss[:, None] + SEL_BLOCK) & (cs[None, :] + CMP_BLOCK > ss[:, None])
    cover &= (np.arange(n_cmp_cols)[None, :] < n_cmp) & (np.arange(n_sel_rows)[:, None] < n_sel)
    return jnp.asarray(cover.astype(np.float32))


def _nsa_prompt_body(q_ref, cmp_ref, ks_ref, vs_ref, kw_ref, vw_ref, gate_ref, cover_ref, expand_ref, o_ref,
                     *, hg, tq, n_win, n_cmp):
    qi = pl.program_id(2)
    T = ks_ref.shape[0]
    n_cmp_pad = cmp_ref.shape[2]
    n_sel = cover_ref.shape[0]
    q = _stack_heads(q_ref[...], range(hg))
    q0 = qi * tq

    s = (jnp.dot(q, cmp_ref[0], preferred_element_type=F32) * ATTN_SCALE).reshape(hg, tq, n_cmp_pad)
    qpos_c = q0 + lax.broadcasted_iota(jnp.int32, (tq, n_cmp_pad), 0)
    nn = lax.broadcasted_iota(jnp.int32, (tq, n_cmp_pad), 1)
    p_cmp = _softmax_rows(s, ((nn * CMP_STRIDE + (CMP_BLOCK - 1) <= qpos_c) & (nn < n_cmp))[None])
    o_cmp = _dot_nt(p_cmp.astype(BF16).reshape(hg * tq, n_cmp_pad), cmp_ref[1])

    p_sum = jnp.sum(p_cmp, axis=0)
    score_t = _dot_nt(cover_ref[...], p_sum, precision=lax.Precision.HIGHEST)
    cur = (q0 + lax.broadcasted_iota(jnp.int32, (n_sel, tq), 1)) // SEL_BLOCK
    sel_t = _select_blocks(score_t, cur, n_sel)
    sel_keys = jnp.dot(sel_t.T.astype(BF16), expand_ref[...], preferred_element_type=F32)

    qpos = q0 + lax.broadcasted_iota(jnp.int32, (tq, T), 0)
    kpos = lax.broadcasted_iota(jnp.int32, (tq, T), 1)
    mask_slc = ((sel_keys > 0.5) & (kpos <= qpos))[None]
    s = _scores(q, ks_ref[...]).reshape(hg, tq, T)
    p_slc = _softmax_rows(s, mask_slc)
    o_slc = jnp.dot(p_slc.astype(BF16).reshape(hg * tq, T), vs_ref[...], preferred_element_type=F32)

    start = pl.multiple_of(jnp.maximum(q0 + tq - n_win, 0), tq)
    kw = kw_ref[pl.ds(start, n_win), :]
    vw = vw_ref[pl.ds(start, n_win), :]
    dlt = (q0 + lax.broadcasted_iota(jnp.int32, (tq, n_win), 0)) - (
        start + lax.broadcasted_iota(jnp.int32, (tq, n_win), 1))
    mask_win = ((dlt >= 0) & (dlt <= WINDOW))[None]
    s = _scores(q, kw).reshape(hg, tq, n_win)
    p_win = _softmax_rows(s, mask_win)
    o_win = jnp.dot(p_win.astype(BF16).reshape(hg * tq, n_win), vw, preferred_element_type=F32)

    gate = jax.nn.sigmoid(gate_ref[...])
    for h in range(hg):
        rows = slice(h * tq, (h + 1) * tq)
        o = (gate[:, 3 * h:3 * h + 1] * o_cmp[rows] + gate[:, 3 * h + 1:3 * h + 2] * o_slc[rows]
             + gate[:, 3 * h + 2:3 * h + 3] * o_win[rows])
        o_ref[:, h * HEAD_DIM:(h + 1) * HEAD_DIM] = o.astype(o_ref.dtype)


def nsa_attention_prompt(q, kv, cmp_t, gates, n_batch, T, *, tq=Q_TILE):
    hg = H_NSA // KV_NSA
    G = KV_NSA
    nq = T // tq
    n_cmp_pad = cmp_t.shape[3]
    n_cmp = (T - CMP_BLOCK) // CMP_STRIDE + 1
    n_sel = -(-T // SEL_BLOCK)
    cover_t = _cover_matrix(n_sel, n_cmp_pad, n_cmp, n_sel)
    expand = jnp.asarray((np.arange(T)[None, :] // SEL_BLOCK == np.arange(n_sel)[:, None]).astype(np.float32),
                         dtype=BF16)
    n_win = min(WINDOW + tq, T)
    kern = functools.partial(_nsa_prompt_body, hg=hg, tq=tq, n_win=n_win, n_cmp=n_cmp)
    unit = lambda j: (lambda b, g, i: (b, j * G + g))
    return pl.pallas_call(
        kern,
        grid=(n_batch, G, nq),
        in_specs=[pl.BlockSpec((tq, hg * HEAD_DIM), lambda b, g, i: (b * nq + i, g)),
                  pl.BlockSpec((None, 2, None, HEAD_DIM, n_cmp_pad), lambda b, g, i: (b, 0, g, 0, 0)),
                  pl.BlockSpec((T, HEAD_DIM), unit(2)),
                  pl.BlockSpec((T, HEAD_DIM), unit(3)),
                  pl.BlockSpec((T, HEAD_DIM), unit(4)),
                  pl.BlockSpec((T, HEAD_DIM), unit(5)),
                  pl.BlockSpec((None, tq, hg * 3), lambda b, g, i: (g, b * nq + i, 0)),
                  pl.BlockSpec((n_sel, n_cmp_pad), lambda b, g, i: (0, 0)),
                  pl.BlockSpec((n_sel, T), lambda b, g, i: (0, 0))],
        out_specs=pl.BlockSpec((tq, hg * HEAD_DIM), lambda b, g, i: (b * nq + i, g)),
        out_shape=jax.ShapeDtypeStruct((n_batch * T, W_NSA), BF16),
        compiler_params=_cparams(3),
        name="nsa_attention_prompt",
    )(q, cmp_t.reshape(n_batch, 2, G, HEAD_DIM, n_cmp_pad), kv, kv, kv, kv, gates, cover_t, expand)


def _nsa_sample_body(pt_ref, *refs, hg, n_cmp, n_sel):
    pages = refs[:PAGES_PER_STEP]
    (q_ref, cmp_ref, new_ref, win_ref, gate_ref, cover_ref, expand_ref, o_ref,
     sel_ref, key_ref, ocmp_ref, m_ref, l_ref, acc_ref) = refs[PAGES_PER_STEP:]
    s_id = pl.program_id(1)
    G = KV_NSA
    n_rows = hg * TOK_PAD
    n_cmp_pad = cmp_ref.shape[3]
    n_sel_pad = cover_ref.shape[0]
    cur = PAST_LEN // SEL_BLOCK
    blocks_per_step = PAGES_PER_STEP * PAGE_SIZE // SEL_BLOCK

    @pl.when(s_id == 0)
    def _():
        _init_state(m_ref, l_ref, acc_ref)
        nn = lax.broadcasted_iota(jnp.int32, (n_rows, n_cmp_pad), 1)
        for g in range(G):
            s = jnp.dot(q_ref[g], cmp_ref[0, g], preferred_element_type=F32) * ATTN_SCALE
            p = _softmax_rows(s, nn < n_cmp)
            ocmp_ref[g] = _dot_nt(p.astype(BF16), cmp_ref[1, g])
            p_sum = jnp.sum(p.reshape(hg, TOK_PAD, n_cmp_pad), axis=0)
            p_sum = jnp.concatenate([p_sum, jnp.zeros((LANES - TOK_PAD, n_cmp_pad), F32)], axis=0)
            score_t = _dot_nt(cover_ref[...], p_sum, precision=lax.Precision.HIGHEST)
            key, valid = _block_keys(score_t, cur)
            key_ref[...] = key
            jj = lax.broadcasted_iota(jnp.int32, key.shape, 0)

            def rank_step(i, rank):
                ki = key_ref[pl.ds(i, 1), :]
                kk = key_ref[...]
                beats = (ki > kk) | ((ki == kk) & (jj > i))
                return rank + jnp.where(beats, 1.0, 0.0)

            rank = lax.fori_loop(0, n_sel, rank_step, jnp.zeros(key.shape, F32))
            sel_ref[g] = jnp.where(valid & (jj < n_sel) & (rank < float(min(SEL_TOPK, n_sel))), 1.0, 0.0)

    row0 = pl.multiple_of(s_id * blocks_per_step, blocks_per_step)
    for g in range(G):
        sel_blk = sel_ref[g, pl.ds(row0, blocks_per_step), :]
        tok_keys = jnp.dot(sel_blk.T[:TOK_PAD].astype(BF16), expand_ref[...], preferred_element_type=F32)
        mask = jnp.concatenate([tok_keys] * hg, axis=0) > 0.5
        s = _scores(q_ref[g], _page_unit(pages, 2 * G + g))
        state = _online_step(_load_state(m_ref, l_ref, acc_ref, g), s, mask, _page_unit(pages, 3 * G + g))
        _store_state(m_ref, l_ref, acc_ref, g, state)

    @pl.when(s_id == pl.num_programs(1) - 1)
    def _():
        new = new_ref[...]
        mask_new = _new_key_mask(n_rows)
        t_row = _token_of_row((n_rows, WINDOW))
        i_key = lax.broadcasted_iota(jnp.int32, (n_rows, WINDOW), 1)
        mask_win = i_key >= t_row
        unit = lambda j, g: new[:, (j * G + g) * LANES:(j * G + g + 1) * LANES]
        for g in range(G):
            q = q_ref[g]
            o_slc = _online_finish(_online_step(_load_state(m_ref, l_ref, acc_ref, g), _scores(q, unit(2, g)),
                                                mask_new, unit(3, g)))
            kw = win_ref[pl.ds(g, WINDOW, stride=2 * G), :].astype(BF16)
            vw = win_ref[pl.ds(G + g, WINDOW, stride=2 * G), :].astype(BF16)
            init = (jnp.full((n_rows, 1), NEG_BIG, F32), jnp.zeros((n_rows, 1), F32),
                    jnp.zeros((n_rows, HEAD_DIM), F32))
            state = _online_step(init, _scores(q, kw), mask_win, vw)
            o_win = _online_finish(_online_step(state, _scores(q, unit(4, g)), mask_new, unit(5, g)))
            gate = jax.nn.sigmoid(gate_ref[g])
            o_ref[g] = gate[:, 0:1] * ocmp_ref[g] + gate[:, 1:2] * o_slc + gate[:, 2:3] * o_win


def nsa_attention_sample(pt_flat, cache, page0, q, cmp_t, new_kv, win, win0, gates):
    Bs = q.shape[0]
    G = KV_NSA
    hg = H_NSA // KV_NSA
    n_rows = hg * TOK_PAD
    n_k = PAST_LEN + DEC_SEQ
    n_cmp = (n_k - CMP_BLOCK) // CMP_STRIDE + 1
    n_cmp_pad = cmp_t.shape[3]
    n_sel = -(-n_k // SEL_BLOCK)
    blocks_per_step = PAGES_PER_STEP * PAGE_SIZE // SEL_BLOCK
    n_sel_pad = -(-n_sel // blocks_per_step) * blocks_per_step
    cover_t = _cover_matrix(n_sel_pad, n_cmp_pad, n_cmp, n_sel)
    keys_per_step = PAGES_PER_STEP * PAGE_SIZE
    expand = jnp.asarray((np.arange(keys_per_step)[None, :] // SEL_BLOCK
                          == np.arange(blocks_per_step)[:, None]).astype(np.float32), dtype=BF16)
    kern = functools.partial(_nsa_sample_body, hg=hg, n_cmp=n_cmp, n_sel=n_sel)
    grid_spec = pltpu.PrefetchScalarGridSpec(
        num_scalar_prefetch=1,
        grid=(Bs, N_PAGES // PAGES_PER_STEP),
        in_specs=_page_specs(page0, PAGES_PER_STEP) + [
            pl.BlockSpec((None, G, n_rows, HEAD_DIM), lambda b, s, *_: (b, 0, 0, 0)),
            pl.BlockSpec((None, 2, G, HEAD_DIM, n_cmp_pad), lambda b, s, *_: (b, 0, 0, 0, 0)),
            pl.BlockSpec((None, NEW_PAD, new_kv.shape[2]), lambda b, s, *_: (b, 0, 0)),
            pl.BlockSpec((WINDOW * 2 * G, LANES), lambda b, s, *_: (win0 + b, 0)),
            pl.BlockSpec((None, G, n_rows, 3), lambda b, s, *_: (b, 0, 0, 0)),
            pl.BlockSpec((n_sel_pad, n_cmp_pad), lambda b, s, *_: (0, 0)),
            pl.BlockSpec((blocks_per_step, keys_per_step), lambda b, s, *_: (0, 0))],
        out_specs=pl.BlockSpec((None, G, n_rows, HEAD_DIM), lambda b, s, *_: (b, 0, 0, 0)),
        scratch_shapes=[pltpu.VMEM((G, n_sel_pad, LANES), F32), pltpu.VMEM((n_sel_pad, LANES), F32),
                        pltpu.VMEM((G, n_rows, HEAD_DIM), F32),
                        pltpu.VMEM((G, n_rows, LANES), F32), pltpu.VMEM((G, n_rows, LANES), F32),
                        pltpu.VMEM((G, n_rows, HEAD_DIM), F32)],
    )
    return pl.pallas_call(
        kern,
        grid_spec=grid_spec,
        out_shape=jax.ShapeDtypeStruct((Bs, G, n_rows, HEAD_DIM), F32),
        compiler_params=_cparams(2),
        name="nsa_attention_sample",
    )(pt_flat, *([cache] * PAGES_PER_STEP), q, cmp_t.reshape(Bs, 2, G, HEAD_DIM, n_cmp_pad), new_kv, win, gates,
      cover_t, expand)


def _expert_changed(te_ref, i):
    prev = te_ref[jnp.maximum(i - 1, 0)]
    return (i == 0) | (te_ref[i] != prev)


def _moe_up_body(te_ref, nu_ref, x_ref, wg_ref, wu_ref, bg_ref, bu_ref, o_ref, wgb, wub):
    i = pl.program_id(1)

    @pl.when(_expert_changed(te_ref, i))
    def _():
        _cast_weight_tile(wg_ref, wgb)
        _cast_weight_tile(wu_ref, wub)

    @pl.when(i < nu_ref[0])
    def _():
        x = x_ref[...]
        g = jnp.dot(x, wgb[...], preferred_element_type=F32) + bg_ref[...]
        u = jnp.dot(x, wub[...], preferred_element_type=F32) + bu_ref[...]
        g = jnp.minimum(g, SWIGLU_LIMIT)
        u = jnp.clip(u, -SWIGLU_LIMIT, SWIGLU_LIMIT)
        o_ref[...] = ((u + 1.0) * (g * jax.nn.sigmoid(SWIGLU_ALPHA * g))).astype(o_ref.dtype)


def moe_up(tile_expert, n_used, x_sorted, w_gate_up, b_gate_up, layer, *, tn=256):
    R, D = x_sorted.shape
    de = w_gate_up.shape[3] // 2
    tn = min(tn, de)
    nj = de // tn
    tm = MOE_TILE
    grid_spec = pltpu.PrefetchScalarGridSpec(
        num_scalar_prefetch=2,
        grid=(nj, R // tm),
        in_specs=[pl.BlockSpec((tm, D), lambda j, i, te, nu: (i, 0)),
                  pl.BlockSpec((None, None, D, tn), lambda j, i, te, nu: (layer, te[i], 0, j)),
                  pl.BlockSpec((None, None, D, tn), lambda j, i, te, nu: (layer, te[i], 0, j + nj)),
                  pl.BlockSpec((None, None, 1, tn), lambda j, i, te, nu: (layer, te[i], 0, j)),
                  pl.BlockSpec((None, None, 1, tn), lambda j, i, te, nu: (layer, te[i], 0, j + nj))],
        out_specs=pl.BlockSpec((tm, tn), lambda j, i, te, nu: (i, j)),
        scratch_shapes=[pltpu.VMEM((D, tn), BF16), pltpu.VMEM((D, tn), BF16)],
    )
    return pl.pallas_call(
        _moe_up_body,
        grid_spec=grid_spec,
        out_shape=jax.ShapeDtypeStruct((R, de), BF16),
        compiler_params=_cparams(2),
        name="moe_up",
    )(tile_expert, n_used, x_sorted, w_gate_up, w_gate_up, b_gate_up, b_gate_up)


def _moe_down_body(te_ref, nu_ref, a_ref, w_ref, b_ref, o_ref, wb):
    i = pl.program_id(1)

    @pl.when(_expert_changed(te_ref, i))
    def _():
        _cast_weight_tile(w_ref, wb)

    @pl.when(i < nu_ref[0])
    def _():
        o_ref[...] = jnp.dot(a_ref[...], wb[...], preferred_element_type=F32) + b_ref[...]


def moe_down(tile_expert, n_used, act, w_down, b_down, layer, *, tn=512):
    R, de = act.shape
    D = w_down.shape[3]
    tn = min(tn, D)
    tm = MOE_TILE
    grid_spec = pltpu.PrefetchScalarGridSpec(
        num_scalar_prefetch=2,
        grid=(D // tn, R // tm),
        in_specs=[pl.BlockSpec((tm, de), lambda j, i, te, nu: (i, 0)),
                  pl.BlockSpec((None, None, de, tn), lambda j, i, te, nu: (layer, te[i], 0, j)),
                  pl.BlockSpec((None, None, 1, tn), lambda j, i, te, nu: (layer, te[i], 0, j))],
        out_specs=pl.BlockSpec((tm, tn), lambda j, i, te, nu: (i, j)),
        scratch_shapes=[pltpu.VMEM((de, tn), BF16)],
    )
    return pl.pallas_call(
        _moe_down_body,
        grid_spec=grid_spec,
        out_shape=jax.ShapeDtypeStruct((R, D), F32),
        compiler_params=_cparams(2),
        name="moe_down",
    )(tile_expert, n_used, act, w_down, b_down)


def moe_ffn(h_bf, w_router, b_router, w_gate_up, b_gate_up, w_down, b_down, layer):
    n_tok, D = h_bf.shape
    E = w_router.shape[2]
    tm = MOE_TILE
    logits = matmul_bias(h_bf, w_router, b_router[layer][None, :], tm=512, tn=E, layer=layer, name="router")
    top_v, top_i = lax.top_k(logits, TOP_K)
    gates = jax.nn.softmax(top_v, axis=-1)
    n_assign = n_tok * TOP_K
    e_flat = top_i.reshape(-1)
    onehot = (e_flat[:, None] == jnp.arange(E, dtype=e_flat.dtype)[None, :]).astype(jnp.int32)
    within = jnp.sum((jnp.cumsum(onehot, axis=0) - 1) * onehot, axis=1)
    counts = jnp.sum(onehot, axis=0)
    padded = (counts + tm - 1) // tm * tm
    pad_end = jnp.cumsum(padded)
    dest = (pad_end - padded)[e_flat] + within
    n_tiles = -(-(n_assign + E * (tm - 1)) // tm)
    tok_flat = jnp.arange(n_assign, dtype=jnp.int32) // TOP_K
    row_tok = jnp.full((n_tiles * tm,), n_tok, jnp.int32).at[dest].set(tok_flat)
    tile_expert = jnp.minimum(jnp.searchsorted(pad_end, jnp.arange(n_tiles) * tm, side='right'),
                              E - 1).astype(jnp.int32)
    n_used = (pad_end[-1] // tm).astype(jnp.int32).reshape(1)
    x_pad = jnp.concatenate([h_bf, jnp.zeros((1, D), BF16)], axis=0)
    x_sorted = x_pad[row_tok]
    act = moe_up(tile_expert, n_used, x_sorted, w_gate_up, b_gate_up[:, :, None, :], layer)
    y = moe_down(tile_expert, n_used, act, w_down, b_down[:, :, None, :], layer)
    n_pad = -(-n_tok // COMBINE_TILE) * COMBINE_TILE
    dest_k = jnp.pad(dest.reshape(n_tok, TOP_K).T, ((0, 0), (0, n_pad - n_tok)))
    return y[dest_k], gates


def _rope_tables(pos):
    half = HEAD_DIM // 2
    inv = ROPE_THETA ** (-jnp.arange(half, dtype=F32) / half)
    ang = pos.astype(F32)[:, None] * inv[None, :]
    cos, sin = jnp.cos(ang), jnp.sin(ang)
    return jnp.concatenate([cos, cos], -1), jnp.concatenate([-sin, sin], -1)


def _unit_flags(pattern):
    return jnp.asarray(np.repeat(np.asarray(pattern, np.float32), LANES)[None, :])


def _diff_scalars(diff_lambda_l, l):
    lam_init = 0.8 - 0.6 * math.exp(-0.3 * l)
    dl = diff_lambda_l.astype(F32)
    lam = jnp.exp(jnp.sum(dl[0] * dl[1])) - jnp.exp(jnp.sum(dl[2] * dl[3])) + lam_init
    return jnp.stack([lam, jnp.asarray(1.0 - lam_init, F32)]).astype(F32)


def _sample_q_rows(q, n_groups, hg, n_comp):
    Bs, Ts, _ = q.shape
    q = q.reshape(Bs, Ts, n_groups, hg, n_comp, HEAD_DIM)
    q = jnp.pad(q, ((0, 0), (0, TOK_PAD - Ts), (0, 0), (0, 0), (0, 0), (0, 0)))
    q = q.transpose(0, 2, 4, 3, 1, 5).reshape(Bs, n_groups, n_comp, hg * TOK_PAD, HEAD_DIM)
    return q[:, :, 0] if n_comp == 1 else q


def _sample_out_rows(o, hg, Ts):
    Bs, G, _, dv = o.shape
    o = o.reshape(Bs, G, hg, TOK_PAD, dv)[:, :, :, :Ts]
    return o.transpose(0, 3, 1, 2, 4).reshape(Bs * Ts, G * hg * dv)


def _pad_new_rows(a):
    return jnp.pad(a, ((0, 0), (0, NEW_PAD - a.shape[1]), (0, 0)))


def kernel(x_prompt, x_sample, cache_diff_kv, cache_nsa_kv, cache_nsa_win, cache_fox_kv, cache_fox_logf, page_table, w_in, b_in, diff_lambda, diff_subln, nsa_cmp_pe, nsa_cmp_w1, nsa_cmp_w2, w_branch_diff, w_branch_nsa, w_branch_fox, w_out, ln1_g, ln1_b, w_router, b_router, w_gate_up, b_gate_up, w_down, b_down, ln2_g, ln2_b):
    B, T, D = x_prompt.shape
    Bs, Ts, _ = x_sample.shape
    n_p = B * T
    n_s = Bs * Ts
    n_pool = cache_diff_kv.shape[1]
    x = jnp.concatenate([x_prompt.reshape(n_p, D), x_sample.reshape(n_s, D)], axis=0)
    x_bf = x.astype(BF16)
    pos = jnp.concatenate([jnp.tile(jnp.arange(T, dtype=jnp.int32), B),
                           jnp.tile(PAST_LEN + jnp.arange(Ts, dtype=jnp.int32), Bs)])
    cos, sin = _rope_tables(pos)
    pt_flat = page_table.reshape(-1).astype(jnp.int32)
    zero_bias = jnp.zeros((1, D), F32)
    flag_all = lambda n: _unit_flags([1] * n)
    flag_diff_kv = _unit_flags([1] * (KV_DIFF * 2) + [0] * (KV_DIFF * 2))
    flag_nsa_kv = _unit_flags(([1] * KV_NSA + [0] * KV_NSA) * 3)
    hg_d, hg_n, hg_f = H_DIFF // KV_DIFF, H_NSA // KV_NSA, H_FOX // KV_FOX
    o_qd, o_kvd, o_qn, o_kvn, o_gn, o_qf, o_kvf, o_ff, o_gm = IN_OFFS[:9]
    nsa_rows, fox_rows = _cache_rows(cache_nsa_kv), _cache_rows(cache_fox_kv)
    diff_units = cache_diff_kv.reshape(cache_diff_kv.shape[:4] + (KV_DIFF, 2, LANES))
    diff_rows = _cache_rows(jnp.swapaxes(diff_units, 4, 5))
    win_rows = cache_nsa_win.reshape(-1, LANES)
    assert cache_nsa_win.shape[2] == WINDOW

    st_p, st_s = [], []
    for l in range(DEPTH):
        bl = b_in[l][None, :]
        _, qd = project(x_bf, w_in, bl, o_qd, W_DIFF, flag_all(H_DIFF * 2), cos, sin, with_f32=False, layer=l,
                        name="proj_diff_q")
        kvd_f, kvd = project(x_bf, w_in, bl, o_kvd, IN_SIZES[1], flag_diff_kv, cos, sin, with_f32=True, layer=l,
                             name="proj_diff_kv")
        _, qn = project(x_bf, w_in, bl, o_qn, W_NSA, flag_all(H_NSA), cos, sin, with_f32=False, layer=l,
                        name="proj_nsa_q")
        kvn_f, kvn = project(x_bf, w_in, bl, o_kvn, IN_SIZES[3], flag_nsa_kv, cos, sin, with_f32=True, layer=l,
                             name="proj_nsa_kv")
        w_fox = realign_columns(w_in, l, o_qf, o_ff - o_qf, F32)
        b_fox = bl[:, o_qf:o_ff]
        _, qf = project(x_bf, w_fox, b_fox, 0, W_FOX, None, cos, sin, with_f32=False, name="proj_fox_q")
        kvf_f, kvf = project(x_bf, w_fox, b_fox, W_FOX, IN_SIZES[6], None, cos, sin, with_f32=True,
                             name="proj_fox_kv")
        u_gn, u_ff = o_gn // LANES * LANES, o_ff // LANES * LANES
        z_gn, _ = project(x_bf, w_in, bl, u_gn, LANES, None, cos, sin, with_f32=True, layer=l, tn=LANES,
                          name="proj_nsa_gate")
        z_ff, _ = project(x_bf, w_in, bl, u_ff, LANES, None, cos, sin, with_f32=True, layer=l, tn=LANES,
                          name="proj_fox_forget")
        zg_n = z_gn[:, o_gn - u_gn:o_gn - u_gn + IN_SIZES[4]]
        logf = jax.nn.log_sigmoid(z_ff[:, o_ff - u_ff:o_ff - u_ff + IN_SIZES[7]])
        scalars = _diff_scalars(diff_lambda[l], l)
        subln_g = diff_subln[l][None, :]
        pe, w1t, w2t = _compress_weights(nsa_cmp_pe[l], nsa_cmp_w1[l], nsa_cmp_w2[l])

        o_d_p = diff_attention_prompt(qd, kvd, scalars, subln_g, B, T)
        cmp_p = compress_prompt(kvn_f, pe, w1t, w2t, B, T)
        gates_n = zg_n.reshape(-1, KV_NSA, hg_n * 3).transpose(1, 0, 2)
        o_n_p = nsa_attention_prompt(qn, kvn, cmp_p, gates_n, B, T)
        c_p = jnp.cumsum(logf[:n_p].reshape(B, T, H_FOX), axis=1)
        c_col = jnp.pad(c_p.reshape(n_p, KV_FOX, hg_f), ((0, n_s), (0, 0), (0, 0))).transpose(1, 0, 2)
        c_row = c_p.reshape(B, T, KV_FOX, hg_f).transpose(0, 2, 3, 1)
        o_f_p = fox_attention_prompt(qf, kvf, c_col, c_row, B, T)

        rows_s = lambda a: a[n_p:].reshape(Bs, Ts, a.shape[1])
        page0 = l * n_pool
        o = diff_attention_sample(pt_flat, scalars, diff_rows, page0,
                                  _sample_q_rows(rows_s(qd), KV_DIFF, hg_d, 2), _pad_new_rows(rows_s(kvd)), subln_g)
        o_d_s = _sample_out_rows(o, hg_d, Ts)
        cmp_s = compress_paged(pt_flat, nsa_rows, page0, pe, w1t, w2t, Bs)
        gates_s = jnp.pad(rows_s(zg_n).reshape(Bs, Ts, KV_NSA, hg_n, 3), ((0, 0), (0, TOK_PAD - Ts), (0, 0), (0, 0), (0, 0)))
        gates_s = gates_s.transpose(0, 2, 3, 1, 4).reshape(Bs, KV_NSA, hg_n * TOK_PAD, 3)
        o = nsa_attention_sample(pt_flat, nsa_rows, page0,
                                 _sample_q_rows(rows_s(qn), KV_NSA, hg_n, 1), cmp_s, _pad_new_rows(rows_s(kvn)),
                                 win_rows, l * Bs, gates_s)
        o_n_s = _sample_out_rows(o, hg_n, Ts)
        lf_past = cache_fox_logf[l][page_table].reshape(Bs, PAST_LEN, H_FOX).astype(F32)
        suffix = jnp.flip(jnp.cumsum(jnp.flip(lf_past, 1), axis=1), 1)
        ck_past = jnp.concatenate([-suffix[:, 1:], jnp.zeros_like(suffix[:, :1])], axis=1)
        ck_past = ck_past.reshape(Bs, PAST_LEN, KV_FOX, hg_f).transpose(0, 2, 3, 1)
        c_new = jnp.cumsum(rows_s(logf), axis=1)
        c_new_g = c_new.reshape(Bs, Ts, KV_FOX, hg_f).transpose(0, 2, 3, 1)
        cq_col = jnp.pad(c_new_g, ((0, 0), (0, 0), (0, 0), (0, TOK_PAD - Ts))).reshape(Bs, KV_FOX, hg_f * TOK_PAD, 1)
        ck_new = jnp.pad(c_new_g, ((0, 0), (0, 0), (0, 0), (0, NEW_PAD - Ts)))
        o = fox_attention_sample(pt_flat, fox_rows, page0,
                                 _sample_q_rows(rows_s(qf), KV_FOX, hg_f, 1), ck_past, cq_col, ck_new,
                                 _pad_new_rows(rows_s(kvf)))
        o_f_s = _sample_out_rows(o, hg_f, Ts)

        new_d, new_n, new_w = kvd_f, kvn_f[:, :4 * KV_NSA * HEAD_DIM], kvn_f[:, 4 * KV_NSA * HEAD_DIM:]
        win_p = new_w[:n_p].reshape(B, T, 2, KV_NSA, HEAD_DIM)[:, T - min(WINDOW, T):]
        win_s = jnp.concatenate([cache_nsa_win[l], new_w[n_p:].reshape(Bs, Ts, 2, KV_NSA, HEAD_DIM)],
                                axis=1)[:, -cache_nsa_win.shape[2]:]
        st_p.append((new_d[:n_p].reshape(B, T, 2, KV_DIFF, 2 * HEAD_DIM), new_n[:n_p].reshape(B, T, 4, KV_NSA, HEAD_DIM),
                     win_p, kvf_f[:n_p].reshape(B, T, 2, KV_FOX, HEAD_DIM), logf[:n_p].reshape(B, T, H_FOX)))
        st_s.append((new_d[n_p:].reshape(Bs, Ts, 2, KV_DIFF, 2 * HEAD_DIM), new_n[n_p:].reshape(Bs, Ts, 4, KV_NSA, HEAD_DIM),
                     win_s, kvf_f[n_p:].reshape(Bs, Ts, 2, KV_FOX, HEAD_DIM), logf[n_p:].reshape(Bs, Ts, H_FOX)))

        o_d = jnp.concatenate([o_d_p, o_d_s.astype(BF16)], axis=0)
        o_n = jnp.concatenate([o_n_p, o_n_s.astype(BF16)], axis=0)
        o_f = jnp.concatenate([o_f_p, o_f_s.astype(BF16)], axis=0)
        w_gate = realign_columns(w_in, l, o_gm, 3 * D, BF16)
        b_gate = bl[:, o_gm:].reshape(3, 1, D)
        merged = merge_branches(x_bf, o_d, o_n, o_f, w_gate, b_gate, w_branch_diff, w_branch_nsa, w_branch_fox, l)
        y = matmul_bias(merged, w_out, zero_bias, tm=512, tn=512, layer=l, name="out_proj")
        h, h_bf = residual_layer_norm(x, y, ln1_g[l][None, :], ln1_b[l][None, :])
        y_rows, gates = moe_ffn(h_bf, w_router, b_router, w_gate_up, b_gate_up, w_down, b_down, l)
        x, x_bf = combine_layer_norm(h, y_rows, gates, ln2_g[l][None, :], ln2_b[l][None, :])
    y_prompt = x[:n_p].reshape(B, T, D)
    y_sample = x[n_p:].reshape(Bs, Ts, D)
    stack = lambda sts, k: jnp.stack([s[k] for s in sts], 0)
    return (y_prompt, y_sample,
            stack(st_p, 0), stack(st_p, 1), stack(st_p, 2), stack(st_p, 3), stack(st_p, 4),
            stack(st_s, 0), stack(st_s, 1), stack(st_s, 2), stack(st_s, 3), stack(st_s, 4))
```

```python
import functools
import math

import numpy as np
import jax
import jax.numpy as jnp
from jax import lax
from jax.experimental import pallas as pl
from jax.experimental.pallas import tpu as pltpu

D_MODEL = 4096
BATCH = 4
SEQ = 2048
DEPTH = 2
DEC_BATCH = 8
DEC_SEQ = 4
PAST_LEN = 16384
PAGE_SIZE = 128

HEAD_DIM = 128
ROPE_THETA = 10000.0
H_DIFF = 4
KV_DIFF = 2
H_NSA = 12
KV_NSA = 2
H_FOX = 12
KV_FOX = 4
W_DIFF = H_DIFF * 2 * HEAD_DIM
W_NSA = H_NSA * HEAD_DIM
W_FOX = H_FOX * HEAD_DIM
CMP_BLOCK = 32
CMP_STRIDE = 16
SEL_BLOCK = 64
SEL_TOPK = 16
WINDOW = 512
N_EXPERTS = 32
TOP_K = 4
D_EXPERT = D_MODEL // 2
SWIGLU_ALPHA = 1.702
SWIGLU_LIMIT = 7.0
LN_EPS = 1e-5
DEEPNORM_ALPHA = (2 * DEPTH) ** 0.25
ATTN_SCALE = HEAD_DIM ** -0.5
IN_SIZES = (
    H_DIFF * 2 * HEAD_DIM,
    2 * KV_DIFF * 2 * HEAD_DIM,
    H_NSA * HEAD_DIM,
    6 * KV_NSA * HEAD_DIM,
    3 * H_NSA,
    H_FOX * HEAD_DIM,
    2 * KV_FOX * HEAD_DIM,
    H_FOX,
    3 * D_MODEL,
)
N_IN = sum(IN_SIZES)
IN_OFFS = tuple(int(v) for v in np.cumsum((0,) + IN_SIZES))
N_PAGES = PAST_LEN // PAGE_SIZE

F32 = jnp.float32
BF16 = jnp.bfloat16
NEG_BIG = -1e30
LANES = 128
SUBLANES = 8

V7X_VMEM_LIMIT_BYTES = 48 * 1024 * 1024
Q_TILE = 128
CAUSAL_STEP = 512
V7X_VMEM_LIMIT_LARGE_BYTES = 56 * 1024 * 1024
MOE_TILE = 256
TOK_PAD = SUBLANES
NEW_PAD = LANES
ROW_UNITS = 8
PAGES_PER_STEP = 8
CMP_PAGES_PER_STEP = 16

assert DEC_SEQ <= TOK_PAD and HEAD_DIM == LANES and PAGE_SIZE % SEL_BLOCK == 0
assert N_PAGES % PAGES_PER_STEP == 0 and N_PAGES % CMP_PAGES_PER_STEP == 0
assert CMP_BLOCK == 2 * CMP_STRIDE and PAGE_SIZE % CMP_STRIDE == 0 and SEQ % CMP_STRIDE == 0
assert ((PAST_LEN + DEC_SEQ - CMP_BLOCK) // CMP_STRIDE) * CMP_STRIDE + CMP_BLOCK <= PAST_LEN
assert PAST_LEN % SEL_BLOCK == 0 and DEC_SEQ <= SEL_BLOCK and WINDOW <= PAST_LEN and SEQ >= WINDOW


def _cparams(n_axes, vmem_limit_bytes=V7X_VMEM_LIMIT_BYTES):
    return pltpu.CompilerParams(dimension_semantics=("arbitrary",) * n_axes,
                                vmem_limit_bytes=vmem_limit_bytes)


def _for_causal_extent(qi, tq, n_keys_total, fn):
    step = min(CAUSAL_STEP, n_keys_total)
    need = (qi * tq + tq + step - 1) // step
    for c in range(1, n_keys_total // step + 1):
        @pl.when(need == c)
        def _():
            fn(c * step)


def _cast_weight_tile(w_ref, wbf_ref, rows_per_chunk=256):
    n_chunks = w_ref.shape[0] // rows_per_chunk

    def body(c, carry):
        r = pl.multiple_of(c * rows_per_chunk, rows_per_chunk)
        wbf_ref[pl.ds(r, rows_per_chunk), :] = w_ref[pl.ds(r, rows_per_chunk), :].astype(BF16)
        return carry

    lax.fori_loop(0, n_chunks, body, 0)


def _mm_body(x_ref, w_ref, b_ref, o_ref, wbf_ref):
    @pl.when(pl.program_id(1) == 0)
    def _():
        _cast_weight_tile(w_ref, wbf_ref)

    acc = jnp.dot(x_ref[...], wbf_ref[...], preferred_element_type=F32)
    o_ref[...] = (acc + b_ref[...]).astype(o_ref.dtype)


def _weight_spec(w, layer, k_rows, tn, col_block):
    if w.ndim == 2:
        return pl.BlockSpec((k_rows, tn), lambda j, i: (0, col_block(j)))
    return pl.BlockSpec((None, k_rows, tn), lambda j, i: (layer, 0, col_block(j)))


def matmul_bias(x, w, b, *, tm, tn, layer=0, out_dtype=F32, name="matmul_bias"):
    M, K = x.shape
    N = w.shape[-1]
    tn = min(tn, N)
    tm = min(tm, M)
    return pl.pallas_call(
        _mm_body,
        grid=(pl.cdiv(N, tn), pl.cdiv(M, tm)),
        in_specs=[pl.BlockSpec((tm, K), lambda j, i: (i, 0)),
                  _weight_spec(w, layer, K, tn, lambda j: j),
                  pl.BlockSpec((1, tn), lambda j, i: (0, j))],
        out_specs=pl.BlockSpec((tm, tn), lambda j, i: (i, j)),
        out_shape=jax.ShapeDtypeStruct((M, N), out_dtype),
        scratch_shapes=[pltpu.VMEM((K, tn), BF16)],
        compiler_params=_cparams(2),
        name=name,
    )(x, w, b)


def _proj_body(x_ref, w_ref, b_ref, flag_ref, cos_ref, sin_ref, *rest, with_f32, use_rope):
    if with_f32:
        of_ref, ob_ref, wbf_ref = rest
    else:
        ob_ref, wbf_ref = rest

    @pl.when(pl.program_id(1) == 0)
    def _():
        _cast_weight_tile(w_ref, wbf_ref)

    acc = jnp.dot(x_ref[...], wbf_ref[...], preferred_element_type=F32) + b_ref[...]
    for u in range(acc.shape[1] // LANES):
        cols = slice(u * LANES, (u + 1) * LANES)
        z = acc[:, cols]
        if use_rope:
            f = flag_ref[:, cols]
            cos = 1.0 + f * (cos_ref[...] - 1.0)
            sin = f * sin_ref[...]
            z = z * cos + pltpu.roll(z, HEAD_DIM // 2, axis=1) * sin
        if with_f32:
            of_ref[:, cols] = z
        ob_ref[:, cols] = z.astype(BF16)


def project(x, w, b, col0, n_cols, rope_flag, cos, sin, *, with_f32, name, layer=0, tm=512, tn=512):
    M, K = x.shape
    assert col0 % tn == 0 and n_cols % tn == 0
    j0 = col0 // tn
    use_rope = rope_flag is not None
    if not use_rope:
        rope_flag = jnp.zeros((1, n_cols), F32)
    kern = functools.partial(_proj_body, with_f32=with_f32, use_rope=use_rope)
    out_block = pl.BlockSpec((tm, tn), lambda j, i: (i, j))
    outs = pl.pallas_call(
        kern,
        grid=(n_cols // tn, pl.cdiv(M, tm)),
        in_specs=[pl.BlockSpec((tm, K), lambda j, i: (i, 0)),
                  _weight_spec(w, layer, K, tn, lambda j: j + j0),
                  pl.BlockSpec((1, tn), lambda j, i: (0, j + j0)),
                  pl.BlockSpec((1, tn), lambda j, i: (0, j)),
                  pl.BlockSpec((tm, LANES), lambda j, i: (i, 0)),
                  pl.BlockSpec((tm, LANES), lambda j, i: (i, 0))],
        out_specs=[out_block, out_block] if with_f32 else [out_block],
        out_shape=([jax.ShapeDtypeStruct((M, n_cols), F32)] if with_f32 else [])
        + [jax.ShapeDtypeStruct((M, n_cols), BF16)],
        scratch_shapes=[pltpu.VMEM((K, tn), BF16)],
        compiler_params=_cparams(2),
        name=name,
    )(x, w, b, rope_flag, cos, sin)
    return (outs[0], outs[1]) if with_f32 else (None, outs[0])


def _realign_body(a_ref, b_ref, o_ref, *, shift):
    w = jnp.concatenate([a_ref[...], b_ref[...]], axis=1)
    width = w.shape[1]
    o_ref[...] = pltpu.roll(w, width - shift, axis=1)[:, :o_ref.shape[1]].astype(o_ref.dtype)


def realign_columns(w, layer, col0, n_cols, out_dtype, *, tr=512, tn=512):
    _, K, _ = w.shape
    tr = min(tr, K)
    aligned = col0 // tn * tn
    shift = col0 - aligned
    assert shift < LANES and n_cols % tn == 0 and K % tr == 0
    ja, jb, units = aligned // tn, aligned // LANES, tn // LANES
    return pl.pallas_call(
        functools.partial(_realign_body, shift=shift),
        grid=(n_cols // tn, K // tr),
        in_specs=[pl.BlockSpec((None, tr, tn), lambda j, r: (layer, r, ja + j)),
                  pl.BlockSpec((None, tr, LANES), lambda j, r: (layer, r, jb + (j + 1) * units))],
        out_specs=pl.BlockSpec((tr, tn), lambda j, r: (r, j)),
        out_shape=jax.ShapeDtypeStruct((K, n_cols), out_dtype),
        compiler_params=_cparams(2),
        name="realign_columns",
    )(w, w)


def _merge_body(x_ref, od_ref, on_ref, of_ref, wg0_ref, wg1_ref, wg2_ref, bg_ref, wd_ref, wn_ref, wf_ref, o_ref,
                wdb, wnb, wfb):
    @pl.when(pl.program_id(1) == 0)
    def _():
        _cast_weight_tile(wd_ref, wdb)
        _cast_weight_tile(wn_ref, wnb)
        _cast_weight_tile(wf_ref, wfb)

    x = x_ref[...]
    acc = None
    for k, (wg_k, o_k, w_k) in enumerate(((wg0_ref, od_ref, wdb), (wg1_ref, on_ref, wnb), (wg2_ref, of_ref, wfb))):
        gate = jax.nn.sigmoid(jnp.dot(x, wg_k[...], preferred_element_type=F32) + bg_ref[k])
        term = gate * jnp.dot(o_k[...], w_k[...], preferred_element_type=F32)
        acc = term if acc is None else acc + term
    o_ref[...] = acc.astype(o_ref.dtype)


def merge_branches(x, o_d, o_n, o_f, w_gate, b_gate, w_d, w_n, w_f, layer, *, tm=256, tn=256):
    M, D = x.shape
    tn = min(tn, D)
    nj = D // tn
    gate_spec = lambda k: pl.BlockSpec((D, tn), lambda j, i: (0, k * nj + j))
    return pl.pallas_call(
        _merge_body,
        grid=(nj, pl.cdiv(M, tm)),
        in_specs=[pl.BlockSpec((tm, D), lambda j, i: (i, 0)),
                  pl.BlockSpec((tm, W_DIFF), lambda j, i: (i, 0)),
                  pl.BlockSpec((tm, W_NSA), lambda j, i: (i, 0)),
                  pl.BlockSpec((tm, W_FOX), lambda j, i: (i, 0)),
                  gate_spec(0), gate_spec(1), gate_spec(2),
                  pl.BlockSpec((3, 1, tn), lambda j, i: (0, 0, j)),
                  _weight_spec(w_d, layer, W_DIFF, tn, lambda j: j),
                  _weight_spec(w_n, layer, W_NSA, tn, lambda j: j),
                  _weight_spec(w_f, layer, W_FOX, tn, lambda j: j)],
        out_specs=pl.BlockSpec((tm, tn), lambda j, i: (i, j)),
        out_shape=jax.ShapeDtypeStruct((M, D), BF16),
        scratch_shapes=[pltpu.VMEM((W_DIFF, tn), BF16), pltpu.VMEM((W_NSA, tn), BF16),
                        pltpu.VMEM((W_FOX, tn), BF16)],
        compiler_params=_cparams(2),
        name="merge_branches",
    )(x, o_d, o_n, o_f, w_gate, w_gate, w_gate, b_gate, w_d, w_n, w_f)


def _ln_body(x_ref, y_ref, g_ref, b_ref, o_ref, obf_ref):
    v = DEEPNORM_ALPHA * x_ref[...] + y_ref[...]
    mu = jnp.mean(v, -1, keepdims=True)
    c = v - mu
    var = jnp.mean(c * c, -1, keepdims=True)
    out = c * lax.rsqrt(var + LN_EPS) * g_ref[...] + b_ref[...]
    o_ref[...] = out
    obf_ref[...] = out.astype(BF16)


def residual_layer_norm(x, y, g, b, *, tm=256):
    M, D = x.shape
    return pl.pallas_call(
        _ln_body,
        grid=(pl.cdiv(M, tm),),
        in_specs=[pl.BlockSpec((tm, D), lambda i: (i, 0)),
                  pl.BlockSpec((tm, D), lambda i: (i, 0)),
                  pl.BlockSpec((1, D), lambda i: (0, 0)),
                  pl.BlockSpec((1, D), lambda i: (0, 0))],
        out_specs=[pl.BlockSpec((tm, D), lambda i: (i, 0)),
                   pl.BlockSpec((tm, D), lambda i: (i, 0))],
        out_shape=[jax.ShapeDtypeStruct((M, D), F32), jax.ShapeDtypeStruct((M, D), BF16)],
        compiler_params=_cparams(1),
        name="residual_layer_norm",
    )(x, y, g, b)


def _combine_ln_body(x_ref, *refs):
    y_refs = refs[:TOP_K]
    gate_ref, g_ref, b_ref, o_ref, obf_ref = refs[TOP_K:]
    gate = gate_ref[...]
    f = None
    for k in range(TOP_K):
        term = gate[:, k:k + 1] * y_refs[k][...].astype(F32)
        f = term if f is None else f + term
    v = DEEPNORM_ALPHA * x_ref[...] + f
    mu = jnp.mean(v, -1, keepdims=True)
    c = v - mu
    var = jnp.mean(c * c, -1, keepdims=True)
    out = c * lax.rsqrt(var + LN_EPS) * g_ref[...] + b_ref[...]
    o_ref[...] = out
    obf_ref[...] = out.astype(BF16)


COMBINE_TILE = 256


def combine_layer_norm(x, y_picks, gates, g, b):
    M, D = x.shape
    tm = COMBINE_TILE
    pick_spec = lambda k: pl.BlockSpec((None, tm, D), lambda i: (k, i, 0))
    return pl.pallas_call(
        _combine_ln_body,
        grid=(pl.cdiv(M, tm),),
        in_specs=[pl.BlockSpec((tm, D), lambda i: (i, 0))] + [pick_spec(k) for k in range(TOP_K)] + [
                  pl.BlockSpec((tm, TOP_K), lambda i: (i, 0)),
                  pl.BlockSpec((1, D), lambda i: (0, 0)),
                  pl.BlockSpec((1, D), lambda i: (0, 0))],
        out_specs=[pl.BlockSpec((tm, D), lambda i: (i, 0)),
                   pl.BlockSpec((tm, D), lambda i: (i, 0))],
        out_shape=[jax.ShapeDtypeStruct((M, D), F32), jax.ShapeDtypeStruct((M, D), BF16)],
        compiler_params=_cparams(1),
        name="moe_combine_layer_norm",
    )(x, *([y_picks] * TOP_K), gates, g, b)


def _softmax_rows(s, mask):
    s = jnp.where(mask, s, NEG_BIG)
    m = jnp.max(s, -1, keepdims=True)
    p = jnp.where(mask, jnp.exp(s - m), 0.0)
    d = jnp.sum(p, -1, keepdims=True)
    return p * (1.0 / jnp.where(d > 0, d, 1.0))


def _exp_rows(s, mask):
    s = jnp.where(mask, s, NEG_BIG)
    p = jnp.exp(s - jnp.max(s, -1, keepdims=True))
    return p, 1.0 / jnp.sum(p, -1, keepdims=True)


def _dot_nt(a, b, precision=None):
    return lax.dot_general(a, b, (((1,), (1,)), ((), ())), preferred_element_type=F32, precision=precision)


def _scores(q, k):
    return _dot_nt(q, k) * ATTN_SCALE


def _stack_heads(q, units):
    return jnp.concatenate([q[:, u * LANES:(u + 1) * LANES] for u in units], axis=0)


def _online_step(state, s, mask, v):
    m, l, acc = state
    if mask is not None:
        s = jnp.where(mask, s, NEG_BIG)
    m_new = jnp.maximum(m, jnp.max(s, -1, keepdims=True))
    p = jnp.exp(s - m_new)
    if mask is not None:
        p = jnp.where(mask, p, 0.0)
    alpha = jnp.exp(m - m_new)
    l_new = alpha * l + jnp.sum(p, -1, keepdims=True)
    acc_new = alpha * acc + jnp.dot(p.astype(BF16), v, preferred_element_type=F32)
    return m_new, l_new, acc_new


def _online_finish(state):
    _, l, acc = state
    return acc * (1.0 / jnp.where(l > 0, l, 1.0))


def _load_state(m_ref, l_ref, acc_ref, idx):
    return m_ref[idx][:, :1], l_ref[idx][:, :1], acc_ref[idx]


def _store_state(m_ref, l_ref, acc_ref, idx, state):
    m, l, acc = state
    m_ref[idx] = jnp.broadcast_to(m, m_ref.shape[1:])
    l_ref[idx] = jnp.broadcast_to(l, l_ref.shape[1:])
    acc_ref[idx] = acc


def _init_state(m_ref, l_ref, acc_ref):
    m_ref[...] = jnp.full(m_ref.shape, NEG_BIG, F32)
    l_ref[...] = jnp.zeros(l_ref.shape, F32)
    acc_ref[...] = jnp.zeros(acc_ref.shape, F32)


def _page_unit(page_refs, unit):
    return jnp.concatenate([r[pl.ds(unit, PAGE_SIZE, stride=ROW_UNITS), :] for r in page_refs], axis=0).astype(BF16)


def _token_of_row(shape):
    return lax.broadcasted_iota(jnp.int32, shape, 0) % TOK_PAD


def _new_key_mask(n_rows):
    t = _token_of_row((n_rows, NEW_PAD))
    j = lax.broadcasted_iota(jnp.int32, (n_rows, NEW_PAD), 1)
    return (j <= t) & (j < DEC_SEQ)


def _subln(o, g_ref, post):
    return o * lax.rsqrt(jnp.mean(o * o, -1, keepdims=True) + LN_EPS) * g_ref[...] * post


def _diff_prompt_body(sc_ref, q_ref, k_ref, v_ref, g_ref, o_ref, *, hg, tq):
    qi = pl.program_id(2)
    lam = sc_ref[0]
    post = sc_ref[1]
    q = q_ref[...]

    def attend(T):
        qpos = qi * tq + lax.broadcasted_iota(jnp.int32, (tq, T), 0)
        kpos = lax.broadcasted_iota(jnp.int32, (tq, T), 1)
        mask = (kpos <= qpos)[None]

        def attend_component(c):
            qc = _stack_heads(q, [h * 2 + c for h in range(hg)])
            s = _scores(qc, k_ref[:T, c * LANES:(c + 1) * LANES]).reshape(hg, tq, T)
            p, inv = _exp_rows(s, mask)
            pv = jnp.dot(p.astype(BF16).reshape(hg * tq, T), v_ref[:T, :], preferred_element_type=F32)
            return pv * inv.reshape(hg * tq, 1)

        o = _subln(attend_component(0) - lam * attend_component(1), g_ref, post)
        for h in range(hg):
            o_ref[:, h * 2 * HEAD_DIM:(h + 1) * 2 * HEAD_DIM] = o[h * tq:(h + 1) * tq].astype(o_ref.dtype)

    _for_causal_extent(qi, tq, k_ref.shape[0], attend)


def diff_attention_prompt(q, kv, scalars, subln_g, n_batch, T, *, tq=Q_TILE):
    hg = H_DIFF // KV_DIFF
    d2 = 2 * HEAD_DIM
    nq = T // tq
    kern = functools.partial(_diff_prompt_body, hg=hg, tq=tq)
    return pl.pallas_call(
        kern,
        grid=(n_batch, KV_DIFF, nq),
        in_specs=[pl.BlockSpec(memory_space=pltpu.SMEM),
                  pl.BlockSpec((tq, hg * d2), lambda b, g, i: (b * nq + i, g)),
                  pl.BlockSpec((T, d2), lambda b, g, i: (b, g)),
                  pl.BlockSpec((T, d2), lambda b, g, i: (b, KV_DIFF + g)),
                  pl.BlockSpec((1, d2), lambda b, g, i: (0, 0))],
        out_specs=pl.BlockSpec((tq, hg * d2), lambda b, g, i: (b * nq + i, g)),
        out_shape=jax.ShapeDtypeStruct((n_batch * T, W_DIFF), BF16),
        compiler_params=_cparams(3),
        name="diff_attention_prompt",
    )(scalars, q, kv, kv, subln_g)


def _diff_sample_body(pt_ref, *refs, hg):
    pages = refs[:PAGES_PER_STEP]
    sc_ref, q_ref, new_ref, g_ref, o_ref, m_ref, l_ref, acc_ref = refs[PAGES_PER_STEP:]
    s_id = pl.program_id(1)
    n_rows = hg * TOK_PAD

    @pl.when(s_id == 0)
    def _():
        _init_state(m_ref, l_ref, acc_ref)

    for g in range(KV_DIFF):
        v = jnp.concatenate([_page_unit(pages, 2 * KV_DIFF + g), _page_unit(pages, 3 * KV_DIFF + g)], axis=1)
        for c in range(2):
            idx = g * 2 + c
            s = _scores(q_ref[g, c], _page_unit(pages, c * KV_DIFF + g))
            state = _online_step(_load_state(m_ref, l_ref, acc_ref, idx), s, None, v)
            _store_state(m_ref, l_ref, acc_ref, idx, state)

    @pl.when(s_id == pl.num_programs(1) - 1)
    def _():
        lam = sc_ref[0]
        post = sc_ref[1]
        mask = _new_key_mask(n_rows)
        new = new_ref[...]
        for g in range(KV_DIFF):
            v = new[:, (KV_DIFF + g) * 2 * LANES:(KV_DIFF + g + 1) * 2 * LANES]
            outs = []
            for c in range(2):
                idx = g * 2 + c
                k = new[:, idx * LANES:(idx + 1) * LANES]
                state = _online_step(_load_state(m_ref, l_ref, acc_ref, idx), _scores(q_ref[g, c], k), mask, v)
                outs.append(_online_finish(state))
            o_ref[g] = _subln(outs[0] - lam * outs[1], g_ref, post)


def _cache_rows(cache):
    assert math.prod(cache.shape[3:]) == ROW_UNITS * LANES and cache.shape[2] == PAGE_SIZE
    return cache.reshape(-1, LANES)


def _page_specs(page0, n_pages_per_step):
    def spec(k):
        return pl.BlockSpec((PAGE_SIZE * ROW_UNITS, LANES),
                            lambda b, s, pt, *_: (page0 + pt[b * N_PAGES + s * n_pages_per_step + k], 0))
    return [spec(k) for k in range(n_pages_per_step)]


def diff_attention_sample(pt_flat, scalars, cache, page0, q, new_kv, subln_g):
    Bs = q.shape[0]
    hg = H_DIFF // KV_DIFF
    n_rows = hg * TOK_PAD
    width = new_kv.shape[2]
    d2 = 2 * HEAD_DIM
    grid_spec = pltpu.PrefetchScalarGridSpec(
        num_scalar_prefetch=1,
        grid=(Bs, N_PAGES // PAGES_PER_STEP),
        in_specs=_page_specs(page0, PAGES_PER_STEP) + [
            pl.BlockSpec(memory_space=pltpu.SMEM),
            pl.BlockSpec((None, KV_DIFF, 2, n_rows, HEAD_DIM), lambda b, s, *_: (b, 0, 0, 0, 0)),
            pl.BlockSpec((None, NEW_PAD, width), lambda b, s, *_: (b, 0, 0)),
            pl.BlockSpec((1, d2), lambda b, s, *_: (0, 0))],
        out_specs=pl.BlockSpec((None, KV_DIFF, n_rows, d2), lambda b, s, *_: (b, 0, 0, 0)),
        scratch_shapes=[pltpu.VMEM((KV_DIFF * 2, n_rows, LANES), F32), pltpu.VMEM((KV_DIFF * 2, n_rows, LANES), F32),
                        pltpu.VMEM((KV_DIFF * 2, n_rows, d2), F32)],
    )
    return pl.pallas_call(
        functools.partial(_diff_sample_body, hg=hg),
        grid_spec=grid_spec,
        out_shape=jax.ShapeDtypeStruct((Bs, KV_DIFF, n_rows, d2), F32),
        compiler_params=_cparams(2),
        name="diff_attention_sample",
    )(pt_flat, *([cache] * PAGES_PER_STEP), scalars, q, new_kv, subln_g)


def _fox_prompt_body(q_ref, k_ref, v_ref, cq_ref, ck_ref, o_ref, *, hg, tq):
    qi = pl.program_id(2)
    q = _stack_heads(q_ref[...], range(hg))
    cq = cq_ref[...]

    def attend(T):
        qpos = qi * tq + lax.broadcasted_iota(jnp.int32, (tq, T), 0)
        kpos = lax.broadcasted_iota(jnp.int32, (tq, T), 1)
        mask = (kpos <= qpos)[None]
        bias = jnp.concatenate([cq[:, h:h + 1] - ck_ref[h:h + 1, :T] for h in range(hg)], axis=0)
        s = (_scores(q, k_ref[:T, :]) + bias).reshape(hg, tq, T)
        p, inv = _exp_rows(s, mask)
        o = jnp.dot(p.astype(BF16).reshape(hg * tq, T), v_ref[:T, :], preferred_element_type=F32)
        o = o * inv.reshape(hg * tq, 1)
        for h in range(hg):
            o_ref[:, h * HEAD_DIM:(h + 1) * HEAD_DIM] = o[h * tq:(h + 1) * tq].astype(o_ref.dtype)

    _for_causal_extent(qi, tq, k_ref.shape[0], attend)


def fox_attention_prompt(q, kv, c_col, c_row, n_batch, T, *, tq=Q_TILE):
    hg = H_FOX // KV_FOX
    nq = T // tq
    kern = functools.partial(_fox_prompt_body, hg=hg, tq=tq)
    return pl.pallas_call(
        kern,
        grid=(n_batch, KV_FOX, nq),
        in_specs=[pl.BlockSpec((tq, hg * HEAD_DIM), lambda b, g, i: (b * nq + i, g)),
                  pl.BlockSpec((T, HEAD_DIM), lambda b, g, i: (b, g)),
                  pl.BlockSpec((T, HEAD_DIM), lambda b, g, i: (b, KV_FOX + g)),
                  pl.BlockSpec((None, tq, hg), lambda b, g, i: (g, b * nq + i, 0)),
                  pl.BlockSpec((None, None, hg, T), lambda b, g, i: (b, g, 0, 0))],
        out_specs=pl.BlockSpec((tq, hg * HEAD_DIM), lambda b, g, i: (b * nq + i, g)),
        out_shape=jax.ShapeDtypeStruct((n_batch * T, W_FOX), BF16),
        compiler_params=_cparams(3),
        name="fox_attention_prompt",
    )(q, kv, kv, c_col, c_row)


def _head_rows(c, hg):
    return jnp.concatenate([jnp.broadcast_to(c[h:h + 1, :], (TOK_PAD, c.shape[1])) for h in range(hg)], axis=0)


def _fox_sample_body(pt_ref, *refs, hg):
    pages = refs[:PAGES_PER_STEP]
    q_ref, ckp_ref, cq_ref, ckn_ref, new_ref, o_ref, m_ref, l_ref, acc_ref = refs[PAGES_PER_STEP:]
    s_id = pl.program_id(1)
    n_rows = hg * TOK_PAD

    @pl.when(s_id == 0)
    def _():
        _init_state(m_ref, l_ref, acc_ref)

    for g in range(KV_FOX):
        bias = cq_ref[g] - _head_rows(ckp_ref[g], hg)
        s = _scores(q_ref[g], _page_unit(pages, g)) + bias
        state = _online_step(_load_state(m_ref, l_ref, acc_ref, g), s, None, _page_unit(pages, KV_FOX + g))
        _store_state(m_ref, l_ref, acc_ref, g, state)

    @pl.when(s_id == pl.num_programs(1) - 1)
    def _():
        mask = _new_key_mask(n_rows)
        new = new_ref[...]
        for g in range(KV_FOX):
            bias = cq_ref[g] - _head_rows(ckn_ref[g], hg)
            s = _scores(q_ref[g], new[:, g * LANES:(g + 1) * LANES]) + bias
            v = new[:, (KV_FOX + g) * LANES:(KV_FOX + g + 1) * LANES]
            o_ref[g] = _online_finish(_online_step(_load_state(m_ref, l_ref, acc_ref, g), s, mask, v))


def fox_attention_sample(pt_flat, cache, page0, q, ck_past, cq_col, ck_new, new_kv):
    Bs = q.shape[0]
    hg = H_FOX // KV_FOX
    n_rows = hg * TOK_PAD
    width = new_kv.shape[2]
    keys_per_step = PAGES_PER_STEP * PAGE_SIZE
    grid_spec = pltpu.PrefetchScalarGridSpec(
        num_scalar_prefetch=1,
        grid=(Bs, N_PAGES // PAGES_PER_STEP),
        in_specs=_page_specs(page0, PAGES_PER_STEP) + [
            pl.BlockSpec((None, KV_FOX, n_rows, HEAD_DIM), lambda b, s, *_: (b, 0, 0, 0)),
            pl.BlockSpec((None, KV_FOX, hg, keys_per_step), lambda b, s, *_: (b, 0, 0, s)),
            pl.BlockSpec((None, KV_FOX, n_rows, 1), lambda b, s, *_: (b, 0, 0, 0)),
            pl.BlockSpec((None, KV_FOX, hg, NEW_PAD), lambda b, s, *_: (b, 0, 0, 0)),
            pl.BlockSpec((None, NEW_PAD, width), lambda b, s, *_: (b, 0, 0))],
        out_specs=pl.BlockSpec((None, KV_FOX, n_rows, HEAD_DIM), lambda b, s, *_: (b, 0, 0, 0)),
        scratch_shapes=[pltpu.VMEM((KV_FOX, n_rows, LANES), F32), pltpu.VMEM((KV_FOX, n_rows, LANES), F32),
                        pltpu.VMEM((KV_FOX, n_rows, HEAD_DIM), F32)],
    )
    return pl.pallas_call(
        functools.partial(_fox_sample_body, hg=hg),
        grid_spec=grid_spec,
        out_shape=jax.ShapeDtypeStruct((Bs, KV_FOX, n_rows, HEAD_DIM), F32),
        compiler_params=_cparams(2),
        name="fox_attention_sample",
    )(pt_flat, *([cache] * PAGES_PER_STEP), q, ck_past, cq_col, ck_new, new_kv)


def _gelu_tanh(x):
    return 0.5 * x * (1.0 + jnp.tanh(math.sqrt(2.0 / math.pi) * (x + 0.044715 * x * x * x)))


def _compress_body(*refs, n_src, paged):
    if paged:
        refs = refs[1:]
    srcs = refs[:n_src]
    pe_ref, w1_ref, w2_ref, o_ref, a_ref, b_ref = refs[n_src:]
    s_id = pl.program_id(1)
    n_units = 2 * KV_NSA
    col = pl.multiple_of(s_id * LANES, LANES)

    def chunk_rows(i, u):
        if paged:
            return jnp.concatenate(
                [src[pl.ds(i * ROW_UNITS + u, PAGE_SIZE // CMP_STRIDE, stride=CMP_STRIDE * ROW_UNITS), :]
                 for src in srcs], axis=0)
        return srcs[u][pl.ds(i, LANES, stride=CMP_STRIDE), :]

    for u in range(n_units):
        kv = u // KV_NSA
        for half, dst in ((0, a_ref), (1, b_ref)):
            pieces = [chunk_rows(i, u) + pe_ref[kv, half * CMP_STRIDE + i:half * CMP_STRIDE + i + 1, :]
                      for i in range(CMP_STRIDE)]
            x = jnp.concatenate(pieces, axis=1).astype(BF16)
            dst[u, :, pl.ds(col, LANES)] = _dot_nt(w1_ref[kv, half], x)

    @pl.when(s_id == pl.num_programs(1) - 1)
    def _():
        n_chunks = a_ref.shape[2]
        for u in range(n_units):
            kv = u // KV_NSA
            y = a_ref[u] + pltpu.roll(b_ref[u], n_chunks - 1, axis=1)
            o_ref[u] = jnp.dot(w2_ref[kv], _gelu_tanh(y).astype(BF16), preferred_element_type=F32).astype(BF16)


def _compress_weights(pe, w1, w2):
    half = CMP_STRIDE * HEAD_DIM
    w1t = jnp.stack([jnp.stack([w1[kv, :half].T, w1[kv, half:].T]) for kv in range(2)]).astype(BF16)
    w2t = jnp.transpose(w2, (0, 2, 1)).astype(BF16)
    return pe, w1t, w2t


def compress_prompt(src, pe, w1t, w2t, n_batch, T):
    n_chunks = T // CMP_STRIDE
    assert n_chunks == LANES
    n_units = 2 * KV_NSA
    kern = functools.partial(_compress_body, n_src=n_units, paged=False)
    full3 = lambda b, s: (0, 0, 0)
    unit_spec = lambda u: pl.BlockSpec((T, HEAD_DIM), lambda b, s: (b, u))
    return pl.pallas_call(
        kern,
        grid=(n_batch, 1),
        in_specs=[unit_spec(u) for u in range(n_units)] + [
                  pl.BlockSpec(pe.shape, full3),
                  pl.BlockSpec(w1t.shape, lambda b, s: (0, 0, 0, 0)),
                  pl.BlockSpec(w2t.shape, full3)],
        out_specs=pl.BlockSpec((None, 2 * KV_NSA, HEAD_DIM, n_chunks), lambda b, s: (b, 0, 0, 0)),
        out_shape=jax.ShapeDtypeStruct((n_batch, 2 * KV_NSA, HEAD_DIM, n_chunks), BF16),
        scratch_shapes=[pltpu.VMEM((2 * KV_NSA, HEAD_DIM, n_chunks), F32),
                        pltpu.VMEM((2 * KV_NSA, HEAD_DIM, n_chunks), F32)],
        compiler_params=_cparams(2),
        name="nsa_compress_prompt",
    )(*([src] * n_units), pe, w1t, w2t)


def compress_paged(pt_flat, cache, page0, pe, w1t, w2t, n_batch):
    chunks_per_page = PAGE_SIZE // CMP_STRIDE
    n_chunks = PAST_LEN // CMP_STRIDE
    assert chunks_per_page * CMP_PAGES_PER_STEP == LANES
    kern = functools.partial(_compress_body, n_src=CMP_PAGES_PER_STEP, paged=True)
    grid_spec = pltpu.PrefetchScalarGridSpec(
        num_scalar_prefetch=1,
        grid=(n_batch, N_PAGES // CMP_PAGES_PER_STEP),
        in_specs=_page_specs(page0, CMP_PAGES_PER_STEP) + [
            pl.BlockSpec(pe.shape, lambda b, s, pt: (0, 0, 0)),
            pl.BlockSpec(w1t.shape, lambda b, s, pt: (0, 0, 0, 0)),
            pl.BlockSpec(w2t.shape, lambda b, s, pt: (0, 0, 0))],
        out_specs=pl.BlockSpec((None, 2 * KV_NSA, HEAD_DIM, n_chunks), lambda b, s, pt: (b, 0, 0, 0)),
        scratch_shapes=[pltpu.VMEM((2 * KV_NSA, HEAD_DIM, n_chunks), F32),
                        pltpu.VMEM((2 * KV_NSA, HEAD_DIM, n_chunks), F32)],
    )
    return pl.pallas_call(
        kern,
        grid_spec=grid_spec,
        out_shape=jax.ShapeDtypeStruct((n_batch, 2 * KV_NSA, HEAD_DIM, n_chunks), BF16),
        compiler_params=_cparams(2),
        name="nsa_compress_paged",
    )(pt_flat, *([cache] * CMP_PAGES_PER_STEP), pe, w1t, w2t)


def _block_keys(score_t, cur):
    jj = lax.broadcasted_iota(jnp.int32, score_t.shape, 0)
    valid = jj <= cur
    forced = (jj == 0) | (jj == cur) | (jj == cur - 1)
    return jnp.where(valid, jnp.where(forced, jnp.inf, score_t), -jnp.inf), valid


def _select_blocks(score_t, cur, n_sel):
    key, valid = _block_keys(score_t, cur)
    jj = lax.broadcasted_iota(jnp.int32, score_t.shape, 0)
    rank = jnp.zeros(score_t.shape, F32)
    for i in range(n_sel):
        ki = key[i:i + 1, :]
        beats = (ki > key) | ((ki == key) & (jj > i))
        rank = rank + jnp.where(beats, 1.0, 0.0)
    return jnp.where(valid & (rank < float(min(SEL_TOPK, n_sel))), 1.0, 0.0)


def _cover_matrix(n_sel_rows, n_cmp_cols, n_cmp, n_sel):
    cs = np.arange(n_cmp_cols) * CMP_STRIDE
    ss = np.arange(n_sel_rows) * SEL_BLOCK
    cover = (cs[None, :] < ss[:, None] + SEL_BLOCK) & (cs[None, :] + CMP_BLOCK > ss[:, None])
    cover &= (np.arange(n_cmp_cols)[None, :] < n_cmp) & (np.arange(n_sel_rows)[:, None] < n_sel)
    return jnp.asarray(cover.astype(np.float32))


def _nsa_prompt_body(q_ref, cmp_ref, ks_ref, vs_ref, kw_ref, vw_ref, gate_ref, cover_ref, expand_ref, o_ref,
                     *, hg, tq, n_win, n_cmp):
    qi = pl.program_id(2)
    T = ks_ref.shape[0]
    n_cmp_pad = cmp_ref.shape[2]
    n_sel = cover_ref.shape[0]
    q = _stack_heads(q_ref[...], range(hg))
    q0 = qi * tq

    s = (jnp.dot(q, cmp_ref[0], preferred_element_type=F32) * ATTN_SCALE).reshape(hg, tq, n_cmp_pad)
    qpos_c = q0 + lax.broadcasted_iota(jnp.int32, (tq, n_cmp_pad), 0)
    nn = lax.broadcasted_iota(jnp.int32, (tq, n_cmp_pad), 1)
    p_cmp = _softmax_rows(s, ((nn * CMP_STRIDE + (CMP_BLOCK - 1) <= qpos_c) & (nn < n_cmp))[None])
    o_cmp = _dot_nt(p_cmp.astype(BF16).reshape(hg * tq, n_cmp_pad), cmp_ref[1])

    p_sum = jnp.sum(p_cmp, axis=0)
    score_t = _dot_nt(cover_ref[...], p_sum, precision=lax.Precision.HIGHEST)
    cur = (q0 + lax.broadcasted_iota(jnp.int32, (n_sel, tq), 1)) // SEL_BLOCK
    sel_q = _select_blocks(score_t, cur, n_sel).T.astype(BF16)

    start = pl.multiple_of(jnp.maximum(q0 + tq - n_win, 0), tq)
    kw = kw_ref[pl.ds(start, n_win), :]
    vw = vw_ref[pl.ds(start, n_win), :]
    dlt = (q0 + lax.broadcasted_iota(jnp.int32, (tq, n_win), 0)) - (
        start + lax.broadcasted_iota(jnp.int32, (tq, n_win), 1))
    mask_win = ((dlt >= 0) & (dlt <= WINDOW))[None]
    s = _scores(q, kw).reshape(hg, tq, n_win)
    p_win, inv_win = _exp_rows(s, mask_win)
    o_win = jnp.dot(p_win.astype(BF16).reshape(hg * tq, n_win), vw, preferred_element_type=F32)
    o_win = o_win * inv_win.reshape(hg * tq, 1)

    gate = jax.nn.sigmoid(gate_ref[...])

    def selected_and_store(n_keys):
        sel_keys = jnp.dot(sel_q, expand_ref[:, :n_keys], preferred_element_type=F32)
        qpos = q0 + lax.broadcasted_iota(jnp.int32, (tq, n_keys), 0)
        kpos = lax.broadcasted_iota(jnp.int32, (tq, n_keys), 1)
        mask_slc = ((sel_keys > 0.5) & (kpos <= qpos))[None]
        s_slc = _scores(q, ks_ref[:n_keys, :]).reshape(hg, tq, n_keys)
        p_slc, inv = _exp_rows(s_slc, mask_slc)
        o_slc = jnp.dot(p_slc.astype(BF16).reshape(hg * tq, n_keys), vs_ref[:n_keys, :], preferred_element_type=F32)
        o_slc = o_slc * inv.reshape(hg * tq, 1)
        for h in range(hg):
            rows = slice(h * tq, (h + 1) * tq)
            o = (gate[:, 3 * h:3 * h + 1] * o_cmp[rows] + gate[:, 3 * h + 1:3 * h + 2] * o_slc[rows]
                 + gate[:, 3 * h + 2:3 * h + 3] * o_win[rows])
            o_ref[:, h * HEAD_DIM:(h + 1) * HEAD_DIM] = o.astype(o_ref.dtype)

    _for_causal_extent(qi, tq, T, selected_and_store)


def nsa_attention_prompt(q, kv, cmp_t, gates, n_batch, T, *, tq=Q_TILE):
    hg = H_NSA // KV_NSA
    G = KV_NSA
    nq = T // tq
    n_cmp_pad = cmp_t.shape[3]
    n_cmp = (T - CMP_BLOCK) // CMP_STRIDE + 1
    n_sel = -(-T // SEL_BLOCK)
    cover_t = _cover_matrix(n_sel, n_cmp_pad, n_cmp, n_sel)
    expand = jnp.asarray((np.arange(T)[None, :] // SEL_BLOCK == np.arange(n_sel)[:, None]).astype(np.float32),
                         dtype=BF16)
    n_win = min(WINDOW + tq, T)
    kern = functools.partial(_nsa_prompt_body, hg=hg, tq=tq, n_win=n_win, n_cmp=n_cmp)
    unit = lambda j: (lambda b, g, i: (b, j * G + g))
    return pl.pallas_call(
        kern,
        grid=(n_batch, G, nq),
        in_specs=[pl.BlockSpec((tq, hg * HEAD_DIM), lambda b, g, i: (b * nq + i, g)),
                  pl.BlockSpec((None, 2, None, HEAD_DIM, n_cmp_pad), lambda b, g, i: (b, 0, g, 0, 0)),
                  pl.BlockSpec((T, HEAD_DIM), unit(2)),
                  pl.BlockSpec((T, HEAD_DIM), unit(3)),
                  pl.BlockSpec((T, HEAD_DIM), unit(4)),
                  pl.BlockSpec((T, HEAD_DIM), unit(5)),
                  pl.BlockSpec((None, tq, hg * 3), lambda b, g, i: (g, b * nq + i, 0)),
                  pl.BlockSpec((n_sel, n_cmp_pad), lambda b, g, i: (0, 0)),
                  pl.BlockSpec((n_sel, T), lambda b, g, i: (0, 0))],
        out_specs=pl.BlockSpec((tq, hg * HEAD_DIM), lambda b, g, i: (b * nq + i, g)),
        out_shape=jax.ShapeDtypeStruct((n_batch * T, W_NSA), BF16),
        compiler_params=_cparams(3),
        name="nsa_attention_prompt",
    )(q, cmp_t.reshape(n_batch, 2, G, HEAD_DIM, n_cmp_pad), kv, kv, kv, kv, gates, cover_t, expand)


def _nsa_sample_body(pt_ref, *refs, hg, n_cmp, n_sel):
    pages = refs[:PAGES_PER_STEP]
    (q_ref, cmp_ref, new_ref, win_ref, gate_ref, cover_ref, expand_ref, o_ref,
     sel_ref, key_ref, ocmp_ref, m_ref, l_ref, acc_ref) = refs[PAGES_PER_STEP:]
    s_id = pl.program_id(1)
    G = KV_NSA
    n_rows = hg * TOK_PAD
    n_cmp_pad = cmp_ref.shape[3]
    n_sel_pad = cover_ref.shape[0]
    cur = PAST_LEN // SEL_BLOCK
    blocks_per_step = PAGES_PER_STEP * PAGE_SIZE // SEL_BLOCK

    @pl.when(s_id == 0)
    def _():
        _init_state(m_ref, l_ref, acc_ref)
        nn = lax.broadcasted_iota(jnp.int32, (n_rows, n_cmp_pad), 1)
        for g in range(G):
            s = jnp.dot(q_ref[g], cmp_ref[0, g], preferred_element_type=F32) * ATTN_SCALE
            p = _softmax_rows(s, nn < n_cmp)
            ocmp_ref[g] = _dot_nt(p.astype(BF16), cmp_ref[1, g])
            p_sum = jnp.sum(p.reshape(hg, TOK_PAD, n_cmp_pad), axis=0)
            p_sum = jnp.concatenate([p_sum, jnp.zeros((LANES - TOK_PAD, n_cmp_pad), F32)], axis=0)
            score_t = _dot_nt(cover_ref[...], p_sum, precision=lax.Precision.HIGHEST)
            key, valid = _block_keys(score_t, cur)
            key_ref[...] = key
            jj = lax.broadcasted_iota(jnp.int32, key.shape, 0)

            def rank_step(i, rank):
                ki = key_ref[pl.ds(i, 1), :]
                kk = key_ref[...]
                beats = (ki > kk) | ((ki == kk) & (jj > i))
                return rank + jnp.where(beats, 1.0, 0.0)

            rank = lax.fori_loop(0, n_sel, rank_step, jnp.zeros(key.shape, F32))
            sel_ref[g] = jnp.where(valid & (jj < n_sel) & (rank < float(min(SEL_TOPK, n_sel))), 1.0, 0.0)

    row0 = pl.multiple_of(s_id * blocks_per_step, blocks_per_step)
    for g in range(G):
        sel_blk = sel_ref[g, pl.ds(row0, blocks_per_step), :]
        tok_keys = jnp.dot(sel_blk.T[:TOK_PAD].astype(BF16), expand_ref[...], preferred_element_type=F32)
        mask = jnp.concatenate([tok_keys] * hg, axis=0) > 0.5
        s = _scores(q_ref[g], _page_unit(pages, 2 * G + g))
        state = _online_step(_load_state(m_ref, l_ref, acc_ref, g), s, mask, _page_unit(pages, 3 * G + g))
        _store_state(m_ref, l_ref, acc_ref, g, state)

    @pl.when(s_id == pl.num_programs(1) - 1)
    def _():
        new = new_ref[...]
        mask_new = _new_key_mask(n_rows)
        t_row = _token_of_row((n_rows, WINDOW))
        i_key = lax.broadcasted_iota(jnp.int32, (n_rows, WINDOW), 1)
        mask_win = i_key >= t_row
        unit = lambda j, g: new[:, (j * G + g) * LANES:(j * G + g + 1) * LANES]
        for g in range(G):
            q = q_ref[g]
            o_slc = _online_finish(_online_step(_load_state(m_ref, l_ref, acc_ref, g), _scores(q, unit(2, g)),
                                                mask_new, unit(3, g)))
            kw = win_ref[pl.ds(g, WINDOW, stride=2 * G), :].astype(BF16)
            vw = win_ref[pl.ds(G + g, WINDOW, stride=2 * G), :].astype(BF16)
            init = (jnp.full((n_rows, 1), NEG_BIG, F32), jnp.zeros((n_rows, 1), F32),
                    jnp.zeros((n_rows, HEAD_DIM), F32))
            state = _online_step(init, _scores(q, kw), mask_win, vw)
            o_win = _online_finish(_online_step(state, _scores(q, unit(4, g)), mask_new, unit(5, g)))
            gate = jax.nn.sigmoid(gate_ref[g])
            o_ref[g] = gate[:, 0:1] * ocmp_ref[g] + gate[:, 1:2] * o_slc + gate[:, 2:3] * o_win


def nsa_attention_sample(pt_flat, cache, page0, q, cmp_t, new_kv, win, win0, gates):
    Bs = q.shape[0]
    G = KV_NSA
    hg = H_NSA // KV_NSA
    n_rows = hg * TOK_PAD
    n_k = PAST_LEN + DEC_SEQ
    n_cmp = (n_k - CMP_BLOCK) // CMP_STRIDE + 1
    n_cmp_pad = cmp_t.shape[3]
    n_sel = -(-n_k // SEL_BLOCK)
    blocks_per_step = PAGES_PER_STEP * PAGE_SIZE // SEL_BLOCK
    n_sel_pad = -(-n_sel // blocks_per_step) * blocks_per_step
    cover_t = _cover_matrix(n_sel_pad, n_cmp_pad, n_cmp, n_sel)
    keys_per_step = PAGES_PER_STEP * PAGE_SIZE
    expand = jnp.asarray((np.arange(keys_per_step)[None, :] // SEL_BLOCK
                          == np.arange(blocks_per_step)[:, None]).astype(np.float32), dtype=BF16)
    kern = functools.partial(_nsa_sample_body, hg=hg, n_cmp=n_cmp, n_sel=n_sel)
    grid_spec = pltpu.PrefetchScalarGridSpec(
        num_scalar_prefetch=1,
        grid=(Bs, N_PAGES // PAGES_PER_STEP),
        in_specs=_page_specs(page0, PAGES_PER_STEP) + [
            pl.BlockSpec((None, G, n_rows, HEAD_DIM), lambda b, s, *_: (b, 0, 0, 0)),
            pl.BlockSpec((None, 2, G, HEAD_DIM, n_cmp_pad), lambda b, s, *_: (b, 0, 0, 0, 0)),
            pl.BlockSpec((None, NEW_PAD, new_kv.shape[2]), lambda b, s, *_: (b, 0, 0)),
            pl.BlockSpec((WINDOW * 2 * G, LANES), lambda b, s, *_: (win0 + b, 0)),
            pl.BlockSpec((None, G, n_rows, 3), lambda b, s, *_: (b, 0, 0, 0)),
            pl.BlockSpec((n_sel_pad, n_cmp_pad), lambda b, s, *_: (0, 0)),
            pl.BlockSpec((blocks_per_step, keys_per_step), lambda b, s, *_: (0, 0))],
        out_specs=pl.BlockSpec((None, G, n_rows, HEAD_DIM), lambda b, s, *_: (b, 0, 0, 0)),
        scratch_shapes=[pltpu.VMEM((G, n_sel_pad, LANES), F32), pltpu.VMEM((n_sel_pad, LANES), F32),
                        pltpu.VMEM((G, n_rows, HEAD_DIM), F32),
                        pltpu.VMEM((G, n_rows, LANES), F32), pltpu.VMEM((G, n_rows, LANES), F32),
                        pltpu.VMEM((G, n_rows, HEAD_DIM), F32)],
    )
    return pl.pallas_call(
        kern,
        grid_spec=grid_spec,
        out_shape=jax.ShapeDtypeStruct((Bs, G, n_rows, HEAD_DIM), F32),
        compiler_params=_cparams(2),
        name="nsa_attention_sample",
    )(pt_flat, *([cache] * PAGES_PER_STEP), q, cmp_t.reshape(Bs, 2, G, HEAD_DIM, n_cmp_pad), new_kv, win, gates,
      cover_t, expand)


def _expert_changed(te_ref, i):
    prev = te_ref[jnp.maximum(i - 1, 0)]
    return (i == 0) | (te_ref[i] != prev)


def _moe_up_body(te_ref, nu_ref, x_ref, wg_ref, wu_ref, bg_ref, bu_ref, o_ref, wgb, wub):
    i = pl.program_id(1)

    @pl.when(_expert_changed(te_ref, i))
    def _():
        _cast_weight_tile(wg_ref, wgb)
        _cast_weight_tile(wu_ref, wub)

    @pl.when(i < nu_ref[0])
    def _():
        x = x_ref[...]
        g = jnp.dot(x, wgb[...], preferred_element_type=F32) + bg_ref[...]
        u = jnp.dot(x, wub[...], preferred_element_type=F32) + bu_ref[...]
        g = jnp.minimum(g, SWIGLU_LIMIT)
        u = jnp.clip(u, -SWIGLU_LIMIT, SWIGLU_LIMIT)
        o_ref[...] = ((u + 1.0) * (g * jax.nn.sigmoid(SWIGLU_ALPHA * g))).astype(o_ref.dtype)


def moe_up(tile_expert, n_used, x_sorted, w_gate_up, b_gate_up, layer, *, tn=512):
    R, D = x_sorted.shape
    de = w_gate_up.shape[3] // 2
    tn = min(tn, de)
    nj = de // tn
    tm = MOE_TILE
    grid_spec = pltpu.PrefetchScalarGridSpec(
        num_scalar_prefetch=2,
        grid=(nj, R // tm),
        in_specs=[pl.BlockSpec((tm, D), lambda j, i, te, nu: (i, 0)),
                  pl.BlockSpec((None, None, D, tn), lambda j, i, te, nu: (layer, te[i], 0, j)),
                  pl.BlockSpec((None, None, D, tn), lambda j, i, te, nu: (layer, te[i], 0, j + nj)),
                  pl.BlockSpec((None, None, 1, tn), lambda j, i, te, nu: (layer, te[i], 0, j)),
                  pl.BlockSpec((None, None, 1, tn), lambda j, i, te, nu: (layer, te[i], 0, j + nj))],
        out_specs=pl.BlockSpec((tm, tn), lambda j, i, te, nu: (i, j)),
        scratch_shapes=[pltpu.VMEM((D, tn), BF16), pltpu.VMEM((D, tn), BF16)],
    )
    return pl.pallas_call(
        _moe_up_body,
        grid_spec=grid_spec,
        out_shape=jax.ShapeDtypeStruct((R, de), BF16),
        compiler_params=_cparams(2, V7X_VMEM_LIMIT_LARGE_BYTES),
        name="moe_up",
    )(tile_expert, n_used, x_sorted, w_gate_up, w_gate_up, b_gate_up, b_gate_up)


def _moe_down_body(te_ref, nu_ref, a_ref, w_ref, b_ref, o_ref, wb):
    i = pl.program_id(1)

    @pl.when(_expert_changed(te_ref, i))
    def _():
        _cast_weight_tile(w_ref, wb)

    @pl.when(i < nu_ref[0])
    def _():
        o_ref[...] = (jnp.dot(a_ref[...], wb[...], preferred_element_type=F32) + b_ref[...]).astype(o_ref.dtype)


def moe_down(tile_expert, n_used, act, w_down, b_down, layer, *, tn=1024):
    R, de = act.shape
    D = w_down.shape[3]
    tn = min(tn, D)
    tm = MOE_TILE
    grid_spec = pltpu.PrefetchScalarGridSpec(
        num_scalar_prefetch=2,
        grid=(D // tn, R // tm),
        in_specs=[pl.BlockSpec((tm, de), lambda j, i, te, nu: (i, 0)),
                  pl.BlockSpec((None, None, de, tn), lambda j, i, te, nu: (layer, te[i], 0, j)),
                  pl.BlockSpec((None, None, 1, tn), lambda j, i, te, nu: (layer, te[i], 0, j))],
        out_specs=pl.BlockSpec((tm, tn), lambda j, i, te, nu: (i, j)),
        scratch_shapes=[pltpu.VMEM((de, tn), BF16)],
    )
    return pl.pallas_call(
        _moe_down_body,
        grid_spec=grid_spec,
        out_shape=jax.ShapeDtypeStruct((R, D), BF16),
        compiler_params=_cparams(2),
        name="moe_down",
    )(tile_expert, n_used, act, w_down, b_down)


RANK_TILE = 256


def _rank_body(e_ref, tri_ref, within_ref, counts_ref, carry_ref):
    @pl.when(pl.program_id(0) == 0)
    def _():
        carry_ref[...] = jnp.zeros(carry_ref.shape, F32)

    e = e_ref[0]
    onehot = jnp.where(e == lax.broadcasted_iota(jnp.int32, (N_EXPERTS, RANK_TILE), 0), 1.0, 0.0)
    incl = jnp.dot(onehot.astype(BF16), tri_ref[...], preferred_element_type=F32)
    carry = carry_ref[...]
    pos = jnp.sum(onehot * (incl - 1.0 + carry[:, :1]), axis=0, keepdims=True)
    within_ref[0] = pos.astype(jnp.int32)
    carry_ref[...] = carry + jnp.sum(onehot, axis=1, keepdims=True)
    counts_ref[...] = carry_ref[...]


def expert_ranks(e_flat):
    n = e_flat.shape[0]
    n_blocks = -(-n // RANK_TILE)
    e_pad = jnp.pad(e_flat.astype(jnp.int32), (0, n_blocks * RANK_TILE - n), constant_values=-1)
    tri = jnp.asarray(np.triu(np.ones((RANK_TILE, RANK_TILE), np.float32)), dtype=BF16)
    within, counts = pl.pallas_call(
        _rank_body,
        grid=(n_blocks,),
        in_specs=[pl.BlockSpec((1, 1, RANK_TILE), lambda i: (i, 0, 0)),
                  pl.BlockSpec((RANK_TILE, RANK_TILE), lambda i: (0, 0))],
        out_specs=[pl.BlockSpec((1, 1, RANK_TILE), lambda i: (i, 0, 0)),
                   pl.BlockSpec((N_EXPERTS, LANES), lambda i: (0, 0))],
        out_shape=[jax.ShapeDtypeStruct((n_blocks, 1, RANK_TILE), jnp.int32),
                   jax.ShapeDtypeStruct((N_EXPERTS, LANES), F32)],
        scratch_shapes=[pltpu.VMEM((N_EXPERTS, LANES), F32)],
        compiler_params=_cparams(1),
        name="moe_expert_ranks",
    )(e_pad.reshape(n_blocks, 1, RANK_TILE), tri)
    return within.reshape(-1)[:n], counts[:, 0].astype(jnp.int32)


def moe_ffn(h_bf, w_router, b_router, w_gate_up, b_gate_up, w_down, b_down, layer):
    n_tok, D = h_bf.shape
    E = w_router.shape[2]
    tm = MOE_TILE
    logits = matmul_bias(h_bf, w_router, b_router[layer][None, :], tm=512, tn=E, layer=layer, name="router")
    top_v, top_i = lax.top_k(logits, TOP_K)
    gates = jax.nn.softmax(top_v, axis=-1)
    n_assign = n_tok * TOP_K
    e_flat = top_i.reshape(-1)
    within, counts = expert_ranks(e_flat)
    padded = (counts + tm - 1) // tm * tm
    pad_end = jnp.cumsum(padded)
    dest = (pad_end - padded)[e_flat] + within
    n_tiles = -(-(n_assign + E * (tm - 1)) // tm)
    tok_flat = jnp.arange(n_assign, dtype=jnp.int32) // TOP_K
    row_tok = jnp.full((n_tiles * tm,), n_tok, jnp.int32).at[dest].set(tok_flat)
    tile_expert = jnp.minimum(jnp.searchsorted(pad_end, jnp.arange(n_tiles) * tm, side='right'),
                              E - 1).astype(jnp.int32)
    n_used = (pad_end[-1] // tm).astype(jnp.int32).reshape(1)
    x_pad = jnp.concatenate([h_bf, jnp.zeros((1, D), BF16)], axis=0)
    x_sorted = x_pad[row_tok]
    act = moe_up(tile_expert, n_used, x_sorted, w_gate_up, b_gate_up[:, :, None, :], layer)
    y = moe_down(tile_expert, n_used, act, w_down, b_down[:, :, None, :], layer)
    n_pad = -(-n_tok // COMBINE_TILE) * COMBINE_TILE
    dest_k = jnp.pad(dest.reshape(n_tok, TOP_K).T, ((0, 0), (0, n_pad - n_tok)))
    return y[dest_k], gates


def _rope_tables(pos):
    half = HEAD_DIM // 2
    inv = ROPE_THETA ** (-jnp.arange(half, dtype=F32) / half)
    ang = pos.astype(F32)[:, None] * inv[None, :]
    cos, sin = jnp.cos(ang), jnp.sin(ang)
    return jnp.concatenate([cos, cos], -1), jnp.concatenate([-sin, sin], -1)


def _unit_flags(pattern):
    return jnp.asarray(np.repeat(np.asarray(pattern, np.float32), LANES)[None, :])


def _diff_scalars(diff_lambda_l, l):
    lam_init = 0.8 - 0.6 * math.exp(-0.3 * l)
    dl = diff_lambda_l.astype(F32)
    lam = jnp.exp(jnp.sum(dl[0] * dl[1])) - jnp.exp(jnp.sum(dl[2] * dl[3])) + lam_init
    return jnp.stack([lam, jnp.asarray(1.0 - lam_init, F32)]).astype(F32)


def _sample_q_rows(q, n_groups, hg, n_comp):
    Bs, Ts, _ = q.shape
    q = q.reshape(Bs, Ts, n_groups, hg, n_comp, HEAD_DIM)
    q = jnp.pad(q, ((0, 0), (0, TOK_PAD - Ts), (0, 0), (0, 0), (0, 0), (0, 0)))
    q = q.transpose(0, 2, 4, 3, 1, 5).reshape(Bs, n_groups, n_comp, hg * TOK_PAD, HEAD_DIM)
    return q[:, :, 0] if n_comp == 1 else q


def _sample_out_rows(o, hg, Ts):
    Bs, G, _, dv = o.shape
    o = o.reshape(Bs, G, hg, TOK_PAD, dv)[:, :, :, :Ts]
    return o.transpose(0, 3, 1, 2, 4).reshape(Bs * Ts, G * hg * dv)


def _pad_new_rows(a):
    return jnp.pad(a, ((0, 0), (0, NEW_PAD - a.shape[1]), (0, 0)))


def kernel(x_prompt, x_sample, cache_diff_kv, cache_nsa_kv, cache_nsa_win, cache_fox_kv, cache_fox_logf, page_table, w_in, b_in, diff_lambda, diff_subln, nsa_cmp_pe, nsa_cmp_w1, nsa_cmp_w2, w_branch_diff, w_branch_nsa, w_branch_fox, w_out, ln1_g, ln1_b, w_router, b_router, w_gate_up, b_gate_up, w_down, b_down, ln2_g, ln2_b):
    B, T, D = x_prompt.shape
    Bs, Ts, _ = x_sample.shape
    n_p = B * T
    n_s = Bs * Ts
    n_pool = cache_diff_kv.shape[1]
    x = jnp.concatenate([x_prompt.reshape(n_p, D), x_sample.reshape(n_s, D)], axis=0)
    x_bf = x.astype(BF16)
    pos = jnp.concatenate([jnp.tile(jnp.arange(T, dtype=jnp.int32), B),
                           jnp.tile(PAST_LEN + jnp.arange(Ts, dtype=jnp.int32), Bs)])
    cos, sin = _rope_tables(pos)
    pt_flat = page_table.reshape(-1).astype(jnp.int32)
    zero_bias = jnp.zeros((1, D), F32)
    flag_all = lambda n: _unit_flags([1] * n)
    flag_diff_kv = _unit_flags([1] * (KV_DIFF * 2) + [0] * (KV_DIFF * 2))
    flag_nsa_kv = _unit_flags(([1] * KV_NSA + [0] * KV_NSA) * 3)
    hg_d, hg_n, hg_f = H_DIFF // KV_DIFF, H_NSA // KV_NSA, H_FOX // KV_FOX
    o_qd, o_kvd, o_qn, o_kvn, o_gn, o_qf, o_kvf, o_ff, o_gm = IN_OFFS[:9]
    nsa_rows, fox_rows = _cache_rows(cache_nsa_kv), _cache_rows(cache_fox_kv)
    diff_units = cache_diff_kv.reshape(cache_diff_kv.shape[:4] + (KV_DIFF, 2, LANES))
    diff_rows = _cache_rows(jnp.swapaxes(diff_units, 4, 5))
    win_rows = cache_nsa_win.reshape(-1, LANES)
    assert cache_nsa_win.shape[2] == WINDOW

    st_p, st_s = [], []
    for l in range(DEPTH):
        bl = b_in[l][None, :]
        _, qd = project(x_bf, w_in, bl, o_qd, W_DIFF, flag_all(H_DIFF * 2), cos, sin, with_f32=False, layer=l,
                        name="proj_diff_q")
        kvd_f, kvd = project(x_bf, w_in, bl, o_kvd, IN_SIZES[1], flag_diff_kv, cos, sin, with_f32=True, layer=l,
                             name="proj_diff_kv")
        _, qn = project(x_bf, w_in, bl, o_qn, W_NSA, flag_all(H_NSA), cos, sin, with_f32=False, layer=l,
                        name="proj_nsa_q")
        kvn_f, kvn = project(x_bf, w_in, bl, o_kvn, IN_SIZES[3], flag_nsa_kv, cos, sin, with_f32=True, layer=l,
                             name="proj_nsa_kv")
        w_fox = realign_columns(w_in, l, o_qf, o_ff - o_qf, F32)
        b_fox = bl[:, o_qf:o_ff]
        _, qf = project(x_bf, w_fox, b_fox, 0, W_FOX, None, cos, sin, with_f32=False, name="proj_fox_q")
        kvf_f, kvf = project(x_bf, w_fox, b_fox, W_FOX, IN_SIZES[6], None, cos, sin, with_f32=True,
                             name="proj_fox_kv")
        u_gn, u_ff = o_gn // LANES * LANES, o_ff // LANES * LANES
        z_gn, _ = project(x_bf, w_in, bl, u_gn, LANES, None, cos, sin, with_f32=True, layer=l, tn=LANES,
                          name="proj_nsa_gate")
        z_ff, _ = project(x_bf, w_in, bl, u_ff, LANES, None, cos, sin, with_f32=True, layer=l, tn=LANES,
                          name="proj_fox_forget")
        zg_n = z_gn[:, o_gn - u_gn:o_gn - u_gn + IN_SIZES[4]]
        logf = jax.nn.log_sigmoid(z_ff[:, o_ff - u_ff:o_ff - u_ff + IN_SIZES[7]])
        scalars = _diff_scalars(diff_lambda[l], l)
        subln_g = diff_subln[l][None, :]
        pe, w1t, w2t = _compress_weights(nsa_cmp_pe[l], nsa_cmp_w1[l], nsa_cmp_w2[l])

        o_d_p = diff_attention_prompt(qd, kvd, scalars, subln_g, B, T)
        cmp_p = compress_prompt(kvn_f, pe, w1t, w2t, B, T)
        gates_n = zg_n.reshape(-1, KV_NSA, hg_n * 3).transpose(1, 0, 2)
        o_n_p = nsa_attention_prompt(qn, kvn, cmp_p, gates_n, B, T)
        c_p = jnp.cumsum(logf[:n_p].reshape(B, T, H_FOX), axis=1)
        c_col = jnp.pad(c_p.reshape(n_p, KV_FOX, hg_f), ((0, n_s), (0, 0), (0, 0))).transpose(1, 0, 2)
        c_row = c_p.reshape(B, T, KV_FOX, hg_f).transpose(0, 2, 3, 1)
        o_f_p = fox_attention_prompt(qf, kvf, c_col, c_row, B, T)

        rows_s = lambda a: a[n_p:].reshape(Bs, Ts, a.shape[1])
        page0 = l * n_pool
        o = diff_attention_sample(pt_flat, scalars, diff_rows, page0,
                                  _sample_q_rows(rows_s(qd), KV_DIFF, hg_d, 2), _pad_new_rows(rows_s(kvd)), subln_g)
        o_d_s = _sample_out_rows(o, hg_d, Ts)
        cmp_s = compress_paged(pt_flat, nsa_rows, page0, pe, w1t, w2t, Bs)
        gates_s = jnp.pad(rows_s(zg_n).reshape(Bs, Ts, KV_NSA, hg_n, 3), ((0, 0), (0, TOK_PAD - Ts), (0, 0), (0, 0), (0, 0)))
        gates_s = gates_s.transpose(0, 2, 3, 1, 4).reshape(Bs, KV_NSA, hg_n * TOK_PAD, 3)
        o = nsa_attention_sample(pt_flat, nsa_rows, page0,
                                 _sample_q_rows(rows_s(qn), KV_NSA, hg_n, 1), cmp_s, _pad_new_rows(rows_s(kvn)),
                                 win_rows, l * Bs, gates_s)
        o_n_s = _sample_out_rows(o, hg_n, Ts)
        lf_past = cache_fox_logf[l][page_table].reshape(Bs, PAST_LEN, H_FOX).astype(F32)
        suffix = jnp.flip(jnp.cumsum(jnp.flip(lf_past, 1), axis=1), 1)
        ck_past = jnp.concatenate([-suffix[:, 1:], jnp.zeros_like(suffix[:, :1])], axis=1)
        ck_past = ck_past.reshape(Bs, PAST_LEN, KV_FOX, hg_f).transpose(0, 2, 3, 1)
        c_new = jnp.cumsum(rows_s(logf), axis=1)
        c_new_g = c_new.reshape(Bs, Ts, KV_FOX, hg_f).transpose(0, 2, 3, 1)
        cq_col = jnp.pad(c_new_g, ((0, 0), (0, 0), (0, 0), (0, TOK_PAD - Ts))).reshape(Bs, KV_FOX, hg_f * TOK_PAD, 1)
        ck_new = jnp.pad(c_new_g, ((0, 0), (0, 0), (0, 0), (0, NEW_PAD - Ts)))
        o = fox_attention_sample(pt_flat, fox_rows, page0,
                                 _sample_q_rows(rows_s(qf), KV_FOX, hg_f, 1), ck_past, cq_col, ck_new,
                                 _pad_new_rows(rows_s(kvf)))
        o_f_s = _sample_out_rows(o, hg_f, Ts)

        new_d, new_n, new_w = kvd_f, kvn_f[:, :4 * KV_NSA * HEAD_DIM], kvn_f[:, 4 * KV_NSA * HEAD_DIM:]
        win_p = new_w[:n_p].reshape(B, T, 2, KV_NSA, HEAD_DIM)[:, T - min(WINDOW, T):]
        win_s = jnp.concatenate([cache_nsa_win[l], new_w[n_p:].reshape(Bs, Ts, 2, KV_NSA, HEAD_DIM)],
                                axis=1)[:, -cache_nsa_win.shape[2]:]
        st_p.append((new_d[:n_p].reshape(B, T, 2, KV_DIFF, 2 * HEAD_DIM), new_n[:n_p].reshape(B, T, 4, KV_NSA, HEAD_DIM),
                     win_p, kvf_f[:n_p].reshape(B, T, 2, KV_FOX, HEAD_DIM), logf[:n_p].reshape(B, T, H_FOX)))
        st_s.append((new_d[n_p:].reshape(Bs, Ts, 2, KV_DIFF, 2 * HEAD_DIM), new_n[n_p:].reshape(Bs, Ts, 4, KV_NSA, HEAD_DIM),
                     win_s, kvf_f[n_p:].reshape(Bs, Ts, 2, KV_FOX, HEAD_DIM), logf[n_p:].reshape(Bs, Ts, H_FOX)))

        o_d = jnp.concatenate([o_d_p, o_d_s.astype(BF16)], axis=0)
        o_n = jnp.concatenate([o_n_p, o_n_s.astype(BF16)], axis=0)
        o_f = jnp.concatenate([o_f_p, o_f_s.astype(BF16)], axis=0)
        w_gate = realign_columns(w_in, l, o_gm, 3 * D, BF16)
        b_gate = bl[:, o_gm:].reshape(3, 1, D)
        merged = merge_branches(x_bf, o_d, o_n, o_f, w_gate, b_gate, w_branch_diff, w_branch_nsa, w_branch_fox, l)
        y = matmul_bias(merged, w_out, zero_bias, tm=512, tn=512, layer=l, name="out_proj")
        h, h_bf = residual_layer_norm(x, y, ln1_g[l][None, :], ln1_b[l][None, :])
        y_rows, gates = moe_ffn(h_bf, w_router, b_router, w_gate_up, b_gate_up, w_down, b_down, l)
        x, x_bf = combine_layer_norm(h, y_rows, gates, ln2_g[l][None, :], ln2_b[l][None, :])
    y_prompt = x[:n_p].reshape(B, T, D)
    y_sample = x[n_p:].reshape(Bs, Ts, D)
    stack = lambda sts, k: jnp.stack([s[k] for s in sts], 0)
    return (y_prompt, y_sample,
            stack(st_p, 0), stack(st_p, 1), stack(st_p, 2), stack(st_p, 3), stack(st_p, 4),
            stack(st_s, 0), stack(st_s, 1), stack(st_s, 2), stack(st_s, 3), stack(st_s, 4))
```

```python
import functools
import math

import numpy as np
import jax
import jax.numpy as jnp
from jax import lax
from jax.experimental import pallas as pl
from jax.experimental.pallas import tpu as pltpu

D_MODEL = 4096
BATCH = 4
SEQ = 2048
DEPTH = 2
DEC_BATCH = 8
DEC_SEQ = 4
PAST_LEN = 16384
PAGE_SIZE = 128

HEAD_DIM = 128
ROPE_THETA = 10000.0
H_DIFF = 4
KV_DIFF = 2
H_NSA = 12
KV_NSA = 2
H_FOX = 12
KV_FOX = 4
W_DIFF = H_DIFF * 2 * HEAD_DIM
W_NSA = H_NSA * HEAD_DIM
W_FOX = H_FOX * HEAD_DIM
CMP_BLOCK = 32
CMP_STRIDE = 16
SEL_BLOCK = 64
SEL_TOPK = 16
WINDOW = 512
N_EXPERTS = 32
TOP_K = 4
D_EXPERT = D_MODEL // 2
SWIGLU_ALPHA = 1.702
SWIGLU_LIMIT = 7.0
LN_EPS = 1e-5
DEEPNORM_ALPHA = (2 * DEPTH) ** 0.25
ATTN_SCALE = HEAD_DIM ** -0.5
IN_SIZES = (
    H_DIFF * 2 * HEAD_DIM,
    2 * KV_DIFF * 2 * HEAD_DIM,
    H_NSA * HEAD_DIM,
    6 * KV_NSA * HEAD_DIM,
    3 * H_NSA,
    H_FOX * HEAD_DIM,
    2 * KV_FOX * HEAD_DIM,
    H_FOX,
    3 * D_MODEL,
)
N_IN = sum(IN_SIZES)
IN_OFFS = tuple(int(v) for v in np.cumsum((0,) + IN_SIZES))
N_PAGES = PAST_LEN // PAGE_SIZE

F32 = jnp.float32
BF16 = jnp.bfloat16
NEG_BIG = -1e30
LANES = 128
SUBLANES = 8

V7X_VMEM_LIMIT_BYTES = 48 * 1024 * 1024
Q_TILE = 128
CAUSAL_STEP = 512
V7X_VMEM_LIMIT_LARGE_BYTES = 56 * 1024 * 1024
MOE_TILE = 512
TOK_PAD = SUBLANES
NEW_PAD = LANES
ROW_UNITS = 8
PAGES_PER_STEP = 16
CMP_PAGES_PER_STEP = 16

assert DEC_SEQ <= TOK_PAD and HEAD_DIM == LANES and PAGE_SIZE % SEL_BLOCK == 0
assert N_PAGES % PAGES_PER_STEP == 0 and N_PAGES % CMP_PAGES_PER_STEP == 0
assert CMP_BLOCK == 2 * CMP_STRIDE and PAGE_SIZE % CMP_STRIDE == 0 and SEQ % CMP_STRIDE == 0
assert ((PAST_LEN + DEC_SEQ - CMP_BLOCK) // CMP_STRIDE) * CMP_STRIDE + CMP_BLOCK <= PAST_LEN
assert PAST_LEN % SEL_BLOCK == 0 and DEC_SEQ <= SEL_BLOCK and WINDOW <= PAST_LEN and SEQ >= WINDOW


def _cparams(n_axes, vmem_limit_bytes=V7X_VMEM_LIMIT_BYTES):
    return pltpu.CompilerParams(dimension_semantics=("arbitrary",) * n_axes,
                                vmem_limit_bytes=vmem_limit_bytes)


def _for_causal_extent(qi, tq, n_keys_total, fn):
    step = min(CAUSAL_STEP, n_keys_total)
    need = (qi * tq + tq + step - 1) // step
    for c in range(1, n_keys_total // step + 1):
        @pl.when(need == c)
        def _():
            fn(c * step)


def _cast_weight_tile(w_ref, wbf_ref, rows_per_chunk=256):
    n_chunks = w_ref.shape[0] // rows_per_chunk

    def body(c, carry):
        r = pl.multiple_of(c * rows_per_chunk, rows_per_chunk)
        wbf_ref[pl.ds(r, rows_per_chunk), :] = w_ref[pl.ds(r, rows_per_chunk), :].astype(BF16)
        return carry

    lax.fori_loop(0, n_chunks, body, 0)


def _mm_body(x_ref, w_ref, b_ref, o_ref, wbf_ref):
    @pl.when(pl.program_id(1) == 0)
    def _():
        _cast_weight_tile(w_ref, wbf_ref)

    acc = jnp.dot(x_ref[...], wbf_ref[...], preferred_element_type=F32)
    o_ref[...] = (acc + b_ref[...]).astype(o_ref.dtype)


def _weight_spec(w, layer, k_rows, tn, col_block):
    if w.ndim == 2:
        return pl.BlockSpec((k_rows, tn), lambda j, i: (0, col_block(j)))
    return pl.BlockSpec((None, k_rows, tn), lambda j, i: (layer, 0, col_block(j)))


def matmul_bias(x, w, b, *, tm, tn, layer=0, out_dtype=F32, name="matmul_bias"):
    M, K = x.shape
    N = w.shape[-1]
    tn = min(tn, N)
    tm = min(tm, M)
    return pl.pallas_call(
        _mm_body,
        grid=(pl.cdiv(N, tn), pl.cdiv(M, tm)),
        in_specs=[pl.BlockSpec((tm, K), lambda j, i: (i, 0)),
                  _weight_spec(w, layer, K, tn, lambda j: j),
                  pl.BlockSpec((1, tn), lambda j, i: (0, j))],
        out_specs=pl.BlockSpec((tm, tn), lambda j, i: (i, j)),
        out_shape=jax.ShapeDtypeStruct((M, N), out_dtype),
        scratch_shapes=[pltpu.VMEM((K, tn), BF16)],
        compiler_params=_cparams(2),
        name=name,
    )(x, w, b)


def _proj_body(x_ref, w_ref, b_ref, flag_ref, cos_ref, sin_ref, *rest, with_f32, use_rope):
    if with_f32:
        of_ref, ob_ref, wbf_ref = rest
    else:
        ob_ref, wbf_ref = rest

    @pl.when(pl.program_id(1) == 0)
    def _():
        _cast_weight_tile(w_ref, wbf_ref)

    acc = jnp.dot(x_ref[...], wbf_ref[...], preferred_element_type=F32) + b_ref[...]
    for u in range(acc.shape[1] // LANES):
        cols = slice(u * LANES, (u + 1) * LANES)
        z = acc[:, cols]
        if use_rope:
            f = flag_ref[:, cols]
            cos = 1.0 + f * (cos_ref[...] - 1.0)
            sin = f * sin_ref[...]
            z = z * cos + pltpu.roll(z, HEAD_DIM // 2, axis=1) * sin
        if with_f32:
            of_ref[:, cols] = z
        ob_ref[:, cols] = z.astype(BF16)


def project(x, w, b, col0, n_cols, rope_flag, cos, sin, *, with_f32, name, layer=0, tm=512, tn=512):
    M, K = x.shape
    assert col0 % tn == 0 and n_cols % tn == 0
    j0 = col0 // tn
    use_rope = rope_flag is not None
    if not use_rope:
        rope_flag = jnp.zeros((1, n_cols), F32)
    kern = functools.partial(_proj_body, with_f32=with_f32, use_rope=use_rope)
    out_block = pl.BlockSpec((tm, tn), lambda j, i: (i, j))
    outs = pl.pallas_call(
        kern,
        grid=(n_cols // tn, pl.cdiv(M, tm)),
        in_specs=[pl.BlockSpec((tm, K), lambda j, i: (i, 0)),
                  _weight_spec(w, layer, K, tn, lambda j: j + j0),
                  pl.BlockSpec((1, tn), lambda j, i: (0, j + j0)),
                  pl.BlockSpec((1, tn), lambda j, i: (0, j)),
                  pl.BlockSpec((tm, LANES), lambda j, i: (i, 0)),
                  pl.BlockSpec((tm, LANES), lambda j, i: (i, 0))],
        out_specs=[out_block, out_block] if with_f32 else [out_block],
        out_shape=([jax.ShapeDtypeStruct((M, n_cols), F32)] if with_f32 else [])
        + [jax.ShapeDtypeStruct((M, n_cols), BF16)],
        scratch_shapes=[pltpu.VMEM((K, tn), BF16)],
        compiler_params=_cparams(2),
        name=name,
    )(x, w, b, rope_flag, cos, sin)
    return (outs[0], outs[1]) if with_f32 else (None, outs[0])


def _realign_body(a_ref, b_ref, o_ref, *, shift):
    w = jnp.concatenate([a_ref[...], b_ref[...]], axis=1)
    width = w.shape[1]
    o_ref[...] = pltpu.roll(w, width - shift, axis=1)[:, :o_ref.shape[1]].astype(o_ref.dtype)


def realign_columns(w, layer, col0, n_cols, out_dtype, *, tr=512, tn=512):
    _, K, _ = w.shape
    tr = min(tr, K)
    aligned = col0 // tn * tn
    shift = col0 - aligned
    assert shift < LANES and n_cols % tn == 0 and K % tr == 0
    ja, jb, units = aligned // tn, aligned // LANES, tn // LANES
    return pl.pallas_call(
        functools.partial(_realign_body, shift=shift),
        grid=(n_cols // tn, K // tr),
        in_specs=[pl.BlockSpec((None, tr, tn), lambda j, r: (layer, r, ja + j)),
                  pl.BlockSpec((None, tr, LANES), lambda j, r: (layer, r, jb + (j + 1) * units))],
        out_specs=pl.BlockSpec((tr, tn), lambda j, r: (r, j)),
        out_shape=jax.ShapeDtypeStruct((K, n_cols), out_dtype),
        compiler_params=_cparams(2),
        name="realign_columns",
    )(w, w)


def _merge_body(x_ref, od_ref, on_ref, of_ref, wg0_ref, wg1_ref, wg2_ref, bg_ref, wd_ref, wn_ref, wf_ref, o_ref,
                wdb, wnb, wfb):
    @pl.when(pl.program_id(1) == 0)
    def _():
        _cast_weight_tile(wd_ref, wdb)
        _cast_weight_tile(wn_ref, wnb)
        _cast_weight_tile(wf_ref, wfb)

    x = x_ref[...]
    acc = None
    for k, (wg_k, o_k, w_k) in enumerate(((wg0_ref, od_ref, wdb), (wg1_ref, on_ref, wnb), (wg2_ref, of_ref, wfb))):
        gate = jax.nn.sigmoid(jnp.dot(x, wg_k[...], preferred_element_type=F32) + bg_ref[k])
        term = gate * jnp.dot(o_k[...], w_k[...], preferred_element_type=F32)
        acc = term if acc is None else acc + term
    o_ref[...] = acc.astype(o_ref.dtype)


def merge_branches(x, o_d, o_n, o_f, w_gate, b_gate, w_d, w_n, w_f, layer, *, tm=256, tn=256):
    M, D = x.shape
    tn = min(tn, D)
    nj = D // tn
    gate_spec = lambda k: pl.BlockSpec((D, tn), lambda j, i: (0, k * nj + j))
    return pl.pallas_call(
        _merge_body,
        grid=(nj, pl.cdiv(M, tm)),
        in_specs=[pl.BlockSpec((tm, D), lambda j, i: (i, 0)),
                  pl.BlockSpec((tm, W_DIFF), lambda j, i: (i, 0)),
                  pl.BlockSpec((tm, W_NSA), lambda j, i: (i, 0)),
                  pl.BlockSpec((tm, W_FOX), lambda j, i: (i, 0)),
                  gate_spec(0), gate_spec(1), gate_spec(2),
                  pl.BlockSpec((3, 1, tn), lambda j, i: (0, 0, j)),
                  _weight_spec(w_d, layer, W_DIFF, tn, lambda j: j),
                  _weight_spec(w_n, layer, W_NSA, tn, lambda j: j),
                  _weight_spec(w_f, layer, W_FOX, tn, lambda j: j)],
        out_specs=pl.BlockSpec((tm, tn), lambda j, i: (i, j)),
        out_shape=jax.ShapeDtypeStruct((M, D), BF16),
        scratch_shapes=[pltpu.VMEM((W_DIFF, tn), BF16), pltpu.VMEM((W_NSA, tn), BF16),
                        pltpu.VMEM((W_FOX, tn), BF16)],
        compiler_params=_cparams(2),
        name="merge_branches",
    )(x, o_d, o_n, o_f, w_gate, w_gate, w_gate, b_gate, w_d, w_n, w_f)


def _ln_body(x_ref, y_ref, g_ref, b_ref, o_ref, obf_ref):
    v = DEEPNORM_ALPHA * x_ref[...] + y_ref[...]
    mu = jnp.mean(v, -1, keepdims=True)
    c = v - mu
    var = jnp.mean(c * c, -1, keepdims=True)
    out = c * lax.rsqrt(var + LN_EPS) * g_ref[...] + b_ref[...]
    o_ref[...] = out
    obf_ref[...] = out.astype(BF16)


def residual_layer_norm(x, y, g, b, *, tm=256):
    M, D = x.shape
    return pl.pallas_call(
        _ln_body,
        grid=(pl.cdiv(M, tm),),
        in_specs=[pl.BlockSpec((tm, D), lambda i: (i, 0)),
                  pl.BlockSpec((tm, D), lambda i: (i, 0)),
                  pl.BlockSpec((1, D), lambda i: (0, 0)),
                  pl.BlockSpec((1, D), lambda i: (0, 0))],
        out_specs=[pl.BlockSpec((tm, D), lambda i: (i, 0)),
                   pl.BlockSpec((tm, D), lambda i: (i, 0))],
        out_shape=[jax.ShapeDtypeStruct((M, D), F32), jax.ShapeDtypeStruct((M, D), BF16)],
        compiler_params=_cparams(1),
        name="residual_layer_norm",
    )(x, y, g, b)


def _combine_ln_body(x_ref, *refs):
    y_refs = refs[:TOP_K]
    gate_ref, g_ref, b_ref, o_ref, obf_ref = refs[TOP_K:]
    gate = gate_ref[...]
    f = None
    for k in range(TOP_K):
        term = gate[:, k:k + 1] * y_refs[k][...].astype(F32)
        f = term if f is None else f + term
    v = DEEPNORM_ALPHA * x_ref[...] + f
    mu = jnp.mean(v, -1, keepdims=True)
    c = v - mu
    var = jnp.mean(c * c, -1, keepdims=True)
    out = c * lax.rsqrt(var + LN_EPS) * g_ref[...] + b_ref[...]
    o_ref[...] = out
    obf_ref[...] = out.astype(BF16)


COMBINE_TILE = 256


def combine_layer_norm(x, y_picks, gates, g, b):
    M, D = x.shape
    tm = COMBINE_TILE
    pick_spec = lambda k: pl.BlockSpec((None, tm, D), lambda i: (k, i, 0))
    return pl.pallas_call(
        _combine_ln_body,
        grid=(pl.cdiv(M, tm),),
        in_specs=[pl.BlockSpec((tm, D), lambda i: (i, 0))] + [pick_spec(k) for k in range(TOP_K)] + [
                  pl.BlockSpec((tm, TOP_K), lambda i: (i, 0)),
                  pl.BlockSpec((1, D), lambda i: (0, 0)),
                  pl.BlockSpec((1, D), lambda i: (0, 0))],
        out_specs=[pl.BlockSpec((tm, D), lambda i: (i, 0)),
                   pl.BlockSpec((tm, D), lambda i: (i, 0))],
        out_shape=[jax.ShapeDtypeStruct((M, D), F32), jax.ShapeDtypeStruct((M, D), BF16)],
        compiler_params=_cparams(1),
        name="moe_combine_layer_norm",
    )(x, *([y_picks] * TOP_K), gates, g, b)


def _softmax_rows(s, mask):
    s = jnp.where(mask, s, NEG_BIG)
    m = jnp.max(s, -1, keepdims=True)
    p = jnp.where(mask, jnp.exp(s - m), 0.0)
    d = jnp.sum(p, -1, keepdims=True)
    return p * (1.0 / jnp.where(d > 0, d, 1.0))


def _exp_rows(s, mask):
    s = jnp.where(mask, s, NEG_BIG)
    p = jnp.exp(s - jnp.max(s, -1, keepdims=True))
    return p, 1.0 / jnp.sum(p, -1, keepdims=True)


def _dot_nt(a, b, precision=None):
    return lax.dot_general(a, b, (((1,), (1,)), ((), ())), preferred_element_type=F32, precision=precision)


def _scores(q, k):
    return _dot_nt(q, k) * ATTN_SCALE


def _stack_heads(q, units):
    return jnp.concatenate([q[:, u * LANES:(u + 1) * LANES] for u in units], axis=0)


def _online_step(state, s, mask, v):
    m, l, acc = state
    if mask is not None:
        s = jnp.where(mask, s, NEG_BIG)
    m_new = jnp.maximum(m, jnp.max(s, -1, keepdims=True))
    p = jnp.exp(s - m_new)
    if mask is not None:
        p = jnp.where(mask, p, 0.0)
    alpha = jnp.exp(m - m_new)
    l_new = alpha * l + jnp.sum(p, -1, keepdims=True)
    acc_new = alpha * acc + jnp.dot(p.astype(BF16), v, preferred_element_type=F32)
    return m_new, l_new, acc_new


def _online_finish(state):
    _, l, acc = state
    return acc * (1.0 / jnp.where(l > 0, l, 1.0))


def _load_state(m_ref, l_ref, acc_ref, idx):
    return m_ref[idx][:, :1], l_ref[idx][:, :1], acc_ref[idx]


def _store_state(m_ref, l_ref, acc_ref, idx, state):
    m, l, acc = state
    m_ref[idx] = jnp.broadcast_to(m, m_ref.shape[1:])
    l_ref[idx] = jnp.broadcast_to(l, l_ref.shape[1:])
    acc_ref[idx] = acc


def _init_state(m_ref, l_ref, acc_ref):
    m_ref[...] = jnp.full(m_ref.shape, NEG_BIG, F32)
    l_ref[...] = jnp.zeros(l_ref.shape, F32)
    acc_ref[...] = jnp.zeros(acc_ref.shape, F32)


def _page_unit(page_refs, unit):
    return jnp.concatenate([r[pl.ds(unit, PAGE_SIZE, stride=ROW_UNITS), :] for r in page_refs], axis=0).astype(BF16)


def _token_of_row(shape):
    return lax.broadcasted_iota(jnp.int32, shape, 0) % TOK_PAD


def _new_key_mask(n_rows):
    t = _token_of_row((n_rows, NEW_PAD))
    j = lax.broadcasted_iota(jnp.int32, (n_rows, NEW_PAD), 1)
    return (j <= t) & (j < DEC_SEQ)


def _subln(o, g_ref, post):
    return o * lax.rsqrt(jnp.mean(o * o, -1, keepdims=True) + LN_EPS) * g_ref[...] * post


def _diff_prompt_body(sc_ref, q_ref, k_ref, v_ref, g_ref, o_ref, *, hg, tq):
    qi = pl.program_id(2)
    lam = sc_ref[0]
    post = sc_ref[1]
    q = q_ref[...]

    def attend(T):
        qpos = qi * tq + lax.broadcasted_iota(jnp.int32, (tq, T), 0)
        kpos = lax.broadcasted_iota(jnp.int32, (tq, T), 1)
        mask = (kpos <= qpos)[None]

        def attend_component(c):
            qc = _stack_heads(q, [h * 2 + c for h in range(hg)])
            s = _scores(qc, k_ref[:T, c * LANES:(c + 1) * LANES]).reshape(hg, tq, T)
            p, inv = _exp_rows(s, mask)
            pv = jnp.dot(p.astype(BF16).reshape(hg * tq, T), v_ref[:T, :], preferred_element_type=F32)
            return pv * inv.reshape(hg * tq, 1)

        o = _subln(attend_component(0) - lam * attend_component(1), g_ref, post)
        for h in range(hg):
            o_ref[:, h * 2 * HEAD_DIM:(h + 1) * 2 * HEAD_DIM] = o[h * tq:(h + 1) * tq].astype(o_ref.dtype)

    _for_causal_extent(qi, tq, k_ref.shape[0], attend)


def diff_attention_prompt(q, kv, scalars, subln_g, n_batch, T, *, tq=Q_TILE):
    hg = H_DIFF // KV_DIFF
    d2 = 2 * HEAD_DIM
    nq = T // tq
    kern = functools.partial(_diff_prompt_body, hg=hg, tq=tq)
    return pl.pallas_call(
        kern,
        grid=(n_batch, KV_DIFF, nq),
        in_specs=[pl.BlockSpec(memory_space=pltpu.SMEM),
                  pl.BlockSpec((tq, hg * d2), lambda b, g, i: (b * nq + i, g)),
                  pl.BlockSpec((T, d2), lambda b, g, i: (b, g)),
                  pl.BlockSpec((T, d2), lambda b, g, i: (b, KV_DIFF + g)),
                  pl.BlockSpec((1, d2), lambda b, g, i: (0, 0))],
        out_specs=pl.BlockSpec((tq, hg * d2), lambda b, g, i: (b * nq + i, g)),
        out_shape=jax.ShapeDtypeStruct((n_batch * T, W_DIFF), BF16),
        compiler_params=_cparams(3),
        name="diff_attention_prompt",
    )(scalars, q, kv, kv, subln_g)


def _diff_sample_body(pt_ref, *refs, hg):
    pages = refs[:PAGES_PER_STEP]
    sc_ref, q_ref, new_ref, g_ref, o_ref, m_ref, l_ref, acc_ref = refs[PAGES_PER_STEP:]
    s_id = pl.program_id(1)
    n_rows = hg * TOK_PAD

    @pl.when(s_id == 0)
    def _():
        _init_state(m_ref, l_ref, acc_ref)

    for g in range(KV_DIFF):
        v = jnp.concatenate([_page_unit(pages, 2 * KV_DIFF + g), _page_unit(pages, 3 * KV_DIFF + g)], axis=1)
        for c in range(2):
            idx = g * 2 + c
            s = _scores(q_ref[g, c], _page_unit(pages, c * KV_DIFF + g))
            state = _online_step(_load_state(m_ref, l_ref, acc_ref, idx), s, None, v)
            _store_state(m_ref, l_ref, acc_ref, idx, state)

    @pl.when(s_id == pl.num_programs(1) - 1)
    def _():
        lam = sc_ref[0]
        post = sc_ref[1]
        mask = _new_key_mask(n_rows)
        new = new_ref[...]
        for g in range(KV_DIFF):
            v = new[:, (KV_DIFF + g) * 2 * LANES:(KV_DIFF + g + 1) * 2 * LANES]
            outs = []
            for c in range(2):
                idx = g * 2 + c
                k = new[:, idx * LANES:(idx + 1) * LANES]
                state = _online_step(_load_state(m_ref, l_ref, acc_ref, idx), _scores(q_ref[g, c], k), mask, v)
                outs.append(_online_finish(state))
            o_ref[g] = _subln(outs[0] - lam * outs[1], g_ref, post)


def _cache_rows(cache):
    assert math.prod(cache.shape[3:]) == ROW_UNITS * LANES and cache.shape[2] == PAGE_SIZE
    return cache.reshape(-1, LANES)


def _page_specs(page0, n_pages_per_step):
    def spec(k):
        return pl.BlockSpec((PAGE_SIZE * ROW_UNITS, LANES),
                            lambda b, s, pt, *_: (page0 + pt[b * N_PAGES + s * n_pages_per_step + k], 0))
    return [spec(k) for k in range(n_pages_per_step)]


def diff_attention_sample(pt_flat, scalars, cache, page0, q, new_kv, subln_g):
    Bs = q.shape[0]
    hg = H_DIFF // KV_DIFF
    n_rows = hg * TOK_PAD
    width = new_kv.shape[2]
    d2 = 2 * HEAD_DIM
    grid_spec = pltpu.PrefetchScalarGridSpec(
        num_scalar_prefetch=1,
        grid=(Bs, N_PAGES // PAGES_PER_STEP),
        in_specs=_page_specs(page0, PAGES_PER_STEP) + [
            pl.BlockSpec(memory_space=pltpu.SMEM),
            pl.BlockSpec((None, KV_DIFF, 2, n_rows, HEAD_DIM), lambda b, s, *_: (b, 0, 0, 0, 0)),
            pl.BlockSpec((None, NEW_PAD, width), lambda b, s, *_: (b, 0, 0)),
            pl.BlockSpec((1, d2), lambda b, s, *_: (0, 0))],
        out_specs=pl.BlockSpec((None, KV_DIFF, n_rows, d2), lambda b, s, *_: (b, 0, 0, 0)),
        scratch_shapes=[pltpu.VMEM((KV_DIFF * 2, n_rows, LANES), F32), pltpu.VMEM((KV_DIFF * 2, n_rows, LANES), F32),
                        pltpu.VMEM((KV_DIFF * 2, n_rows, d2), F32)],
    )
    return pl.pallas_call(
        functools.partial(_diff_sample_body, hg=hg),
        grid_spec=grid_spec,
        out_shape=jax.ShapeDtypeStruct((Bs, KV_DIFF, n_rows, d2), F32),
        compiler_params=_cparams(2),
        name="diff_attention_sample",
    )(pt_flat, *([cache] * PAGES_PER_STEP), scalars, q, new_kv, subln_g)


def _fox_prompt_body(q_ref, k_ref, v_ref, cq_ref, ck_ref, o_ref, *, hg, tq):
    qi = pl.program_id(2)
    q = _stack_heads(q_ref[...], range(hg))
    cq = cq_ref[...]

    def attend(T):
        qpos = qi * tq + lax.broadcasted_iota(jnp.int32, (tq, T), 0)
        kpos = lax.broadcasted_iota(jnp.int32, (tq, T), 1)
        mask = (kpos <= qpos)[None]
        bias = jnp.concatenate([cq[:, h:h + 1] - ck_ref[h:h + 1, :T] for h in range(hg)], axis=0)
        s = (_scores(q, k_ref[:T, :]) + bias).reshape(hg, tq, T)
        p, inv = _exp_rows(s, mask)
        o = jnp.dot(p.astype(BF16).reshape(hg * tq, T), v_ref[:T, :], preferred_element_type=F32)
        o = o * inv.reshape(hg * tq, 1)
        for h in range(hg):
            o_ref[:, h * HEAD_DIM:(h + 1) * HEAD_DIM] = o[h * tq:(h + 1) * tq].astype(o_ref.dtype)

    _for_causal_extent(qi, tq, k_ref.shape[0], attend)


def fox_attention_prompt(q, kv, c_col, c_row, n_batch, T, *, tq=Q_TILE):
    hg = H_FOX // KV_FOX
    nq = T // tq
    kern = functools.partial(_fox_prompt_body, hg=hg, tq=tq)
    return pl.pallas_call(
        kern,
        grid=(n_batch, KV_FOX, nq),
        in_specs=[pl.BlockSpec((tq, hg * HEAD_DIM), lambda b, g, i: (b * nq + i, g)),
                  pl.BlockSpec((T, HEAD_DIM), lambda b, g, i: (b, g)),
                  pl.BlockSpec((T, HEAD_DIM), lambda b, g, i: (b, KV_FOX + g)),
                  pl.BlockSpec((None, tq, hg), lambda b, g, i: (g, b * nq + i, 0)),
                  pl.BlockSpec((None, None, hg, T), lambda b, g, i: (b, g, 0, 0))],
        out_specs=pl.BlockSpec((tq, hg * HEAD_DIM), lambda b, g, i: (b * nq + i, g)),
        out_shape=jax.ShapeDtypeStruct((n_batch * T, W_FOX), BF16),
        compiler_params=_cparams(3),
        name="fox_attention_prompt",
    )(q, kv, kv, c_col, c_row)


def _head_rows(c, hg):
    return jnp.concatenate([jnp.broadcast_to(c[h:h + 1, :], (TOK_PAD, c.shape[1])) for h in range(hg)], axis=0)


def _fox_sample_body(pt_ref, *refs, hg):
    pages = refs[:PAGES_PER_STEP]
    q_ref, ckp_ref, cq_ref, ckn_ref, new_ref, o_ref, m_ref, l_ref, acc_ref = refs[PAGES_PER_STEP:]
    s_id = pl.program_id(1)
    n_rows = hg * TOK_PAD

    @pl.when(s_id == 0)
    def _():
        _init_state(m_ref, l_ref, acc_ref)

    for g in range(KV_FOX):
        bias = cq_ref[g] - _head_rows(ckp_ref[g], hg)
        s = _scores(q_ref[g], _page_unit(pages, g)) + bias
        state = _online_step(_load_state(m_ref, l_ref, acc_ref, g), s, None, _page_unit(pages, KV_FOX + g))
        _store_state(m_ref, l_ref, acc_ref, g, state)

    @pl.when(s_id == pl.num_programs(1) - 1)
    def _():
        mask = _new_key_mask(n_rows)
        new = new_ref[...]
        for g in range(KV_FOX):
            bias = cq_ref[g] - _head_rows(ckn_ref[g], hg)
            s = _scores(q_ref[g], new[:, g * LANES:(g + 1) * LANES]) + bias
            v = new[:, (KV_FOX + g) * LANES:(KV_FOX + g + 1) * LANES]
            o_ref[g] = _online_finish(_online_step(_load_state(m_ref, l_ref, acc_ref, g), s, mask, v))


def fox_attention_sample(pt_flat, cache, page0, q, ck_past, cq_col, ck_new, new_kv):
    Bs = q.shape[0]
    hg = H_FOX // KV_FOX
    n_rows = hg * TOK_PAD
    width = new_kv.shape[2]
    keys_per_step = PAGES_PER_STEP * PAGE_SIZE
    grid_spec = pltpu.PrefetchScalarGridSpec(
        num_scalar_prefetch=1,
        grid=(Bs, N_PAGES // PAGES_PER_STEP),
        in_specs=_page_specs(page0, PAGES_PER_STEP) + [
            pl.BlockSpec((None, KV_FOX, n_rows, HEAD_DIM), lambda b, s, *_: (b, 0, 0, 0)),
            pl.BlockSpec((None, KV_FOX, hg, keys_per_step), lambda b, s, *_: (b, 0, 0, s)),
            pl.BlockSpec((None, KV_FOX, n_rows, 1), lambda b, s, *_: (b, 0, 0, 0)),
            pl.BlockSpec((None, KV_FOX, hg, NEW_PAD), lambda b, s, *_: (b, 0, 0, 0)),
            pl.BlockSpec((None, NEW_PAD, width), lambda b, s, *_: (b, 0, 0))],
        out_specs=pl.BlockSpec((None, KV_FOX, n_rows, HEAD_DIM), lambda b, s, *_: (b, 0, 0, 0)),
        scratch_shapes=[pltpu.VMEM((KV_FOX, n_rows, LANES), F32), pltpu.VMEM((KV_FOX, n_rows, LANES), F32),
                        pltpu.VMEM((KV_FOX, n_rows, HEAD_DIM), F32)],
    )
    return pl.pallas_call(
        functools.partial(_fox_sample_body, hg=hg),
        grid_spec=grid_spec,
        out_shape=jax.ShapeDtypeStruct((Bs, KV_FOX, n_rows, HEAD_DIM), F32),
        compiler_params=_cparams(2),
        name="fox_attention_sample",
    )(pt_flat, *([cache] * PAGES_PER_STEP), q, ck_past, cq_col, ck_new, new_kv)


def _gelu_tanh(x):
    return 0.5 * x * (1.0 + jnp.tanh(math.sqrt(2.0 / math.pi) * (x + 0.044715 * x * x * x)))


def _compress_body(*refs, n_src, paged):
    if paged:
        refs = refs[1:]
    srcs = refs[:n_src]
    pe_ref, w1_ref, w2_ref, o_ref, a_ref, b_ref = refs[n_src:]
    s_id = pl.program_id(1)
    n_units = 2 * KV_NSA
    col = pl.multiple_of(s_id * LANES, LANES)

    def chunk_rows(i, u):
        if paged:
            return jnp.concatenate(
                [src[pl.ds(i * ROW_UNITS + u, PAGE_SIZE // CMP_STRIDE, stride=CMP_STRIDE * ROW_UNITS), :]
                 for src in srcs], axis=0)
        return srcs[u][pl.ds(i, LANES, stride=CMP_STRIDE), :]

    for u in range(n_units):
        kv = u // KV_NSA
        for half, dst in ((0, a_ref), (1, b_ref)):
            pieces = [chunk_rows(i, u) + pe_ref[kv, half * CMP_STRIDE + i:half * CMP_STRIDE + i + 1, :]
                      for i in range(CMP_STRIDE)]
            x = jnp.concatenate(pieces, axis=1).astype(BF16)
            dst[u, :, pl.ds(col, LANES)] = _dot_nt(w1_ref[kv, half], x)

    @pl.when(s_id == pl.num_programs(1) - 1)
    def _():
        n_chunks = a_ref.shape[2]
        for u in range(n_units):
            kv = u // KV_NSA
            y = a_ref[u] + pltpu.roll(b_ref[u], n_chunks - 1, axis=1)
            o_ref[u] = jnp.dot(w2_ref[kv], _gelu_tanh(y).astype(BF16), preferred_element_type=F32).astype(BF16)


def _compress_weights(pe, w1, w2):
    half = CMP_STRIDE * HEAD_DIM
    w1t = jnp.stack([jnp.stack([w1[kv, :half].T, w1[kv, half:].T]) for kv in range(2)]).astype(BF16)
    w2t = jnp.transpose(w2, (0, 2, 1)).astype(BF16)
    return pe, w1t, w2t


def compress_prompt(src, pe, w1t, w2t, n_batch, T):
    n_chunks = T // CMP_STRIDE
    assert n_chunks == LANES
    n_units = 2 * KV_NSA
    kern = functools.partial(_compress_body, n_src=n_units, paged=False)
    full3 = lambda b, s: (0, 0, 0)
    unit_spec = lambda u: pl.BlockSpec((T, HEAD_DIM), lambda b, s: (b, u))
    return pl.pallas_call(
        kern,
        grid=(n_batch, 1),
        in_specs=[unit_spec(u) for u in range(n_units)] + [
                  pl.BlockSpec(pe.shape, full3),
                  pl.BlockSpec(w1t.shape, lambda b, s: (0, 0, 0, 0)),
                  pl.BlockSpec(w2t.shape, full3)],
        out_specs=pl.BlockSpec((None, 2 * KV_NSA, HEAD_DIM, n_chunks), lambda b, s: (b, 0, 0, 0)),
        out_shape=jax.ShapeDtypeStruct((n_batch, 2 * KV_NSA, HEAD_DIM, n_chunks), BF16),
        scratch_shapes=[pltpu.VMEM((2 * KV_NSA, HEAD_DIM, n_chunks), F32),
                        pltpu.VMEM((2 * KV_NSA, HEAD_DIM, n_chunks), F32)],
        compiler_params=_cparams(2),
        name="nsa_compress_prompt",
    )(*([src] * n_units), pe, w1t, w2t)


def compress_paged(pt_flat, cache, page0, pe, w1t, w2t, n_batch):
    chunks_per_page = PAGE_SIZE // CMP_STRIDE
    n_chunks = PAST_LEN // CMP_STRIDE
    assert chunks_per_page * CMP_PAGES_PER_STEP == LANES
    kern = functools.partial(_compress_body, n_src=CMP_PAGES_PER_STEP, paged=True)
    grid_spec = pltpu.PrefetchScalarGridSpec(
        num_scalar_prefetch=1,
        grid=(n_batch, N_PAGES // CMP_PAGES_PER_STEP),
        in_specs=_page_specs(page0, CMP_PAGES_PER_STEP) + [
            pl.BlockSpec(pe.shape, lambda b, s, pt: (0, 0, 0)),
            pl.BlockSpec(w1t.shape, lambda b, s, pt: (0, 0, 0, 0)),
            pl.BlockSpec(w2t.shape, lambda b, s, pt: (0, 0, 0))],
        out_specs=pl.BlockSpec((None, 2 * KV_NSA, HEAD_DIM, n_chunks), lambda b, s, pt: (b, 0, 0, 0)),
        scratch_shapes=[pltpu.VMEM((2 * KV_NSA, HEAD_DIM, n_chunks), F32),
                        pltpu.VMEM((2 * KV_NSA, HEAD_DIM, n_chunks), F32)],
    )
    return pl.pallas_call(
        kern,
        grid_spec=grid_spec,
        out_shape=jax.ShapeDtypeStruct((n_batch, 2 * KV_NSA, HEAD_DIM, n_chunks), BF16),
        compiler_params=_cparams(2),
        name="nsa_compress_paged",
    )(pt_flat, *([cache] * CMP_PAGES_PER_STEP), pe, w1t, w2t)


def _block_keys(score_t, cur):
    jj = lax.broadcasted_iota(jnp.int32, score_t.shape, 0)
    valid = jj <= cur
    forced = (jj == 0) | (jj == cur) | (jj == cur - 1)
    return jnp.where(valid, jnp.where(forced, jnp.inf, score_t), -jnp.inf), valid


def _select_blocks(score_t, cur, n_sel):
    key, valid = _block_keys(score_t, cur)
    jj = lax.broadcasted_iota(jnp.int32, score_t.shape, 0)
    rank = jnp.zeros(score_t.shape, F32)
    for i in range(n_sel):
        ki = key[i:i + 1, :]
        beats = (ki > key) | ((ki == key) & (jj > i))
        rank = rank + jnp.where(beats, 1.0, 0.0)
    return jnp.where(valid & (rank < float(min(SEL_TOPK, n_sel))), 1.0, 0.0)


def _cover_matrix(n_sel_rows, n_cmp_cols, n_cmp, n_sel):
    cs = np.arange(n_cmp_cols) * CMP_STRIDE
    ss = np.arange(n_sel_rows) * SEL_BLOCK
    cover = (cs[None, :] < ss[:, None] + SEL_BLOCK) & (cs[None, :] + CMP_BLOCK > ss[:, None])
    cover &= (np.arange(n_cmp_cols)[None, :] < n_cmp) & (np.arange(n_sel_rows)[:, None] < n_sel)
    return jnp.asarray(cover.astype(np.float32))


def _nsa_prompt_body(q_ref, cmp_ref, ks_ref, vs_ref, kw_ref, vw_ref, gate_ref, cover_ref, expand_ref, o_ref,
                     *, hg, tq, n_win, n_cmp):
    qi = pl.program_id(2)
    T = ks_ref.shape[0]
    n_cmp_pad = cmp_ref.shape[2]
    n_sel = cover_ref.shape[0]
    q = _stack_heads(q_ref[...], range(hg))
    q0 = qi * tq

    s = (jnp.dot(q, cmp_ref[0], preferred_element_type=F32) * ATTN_SCALE).reshape(hg, tq, n_cmp_pad)
    qpos_c = q0 + lax.broadcasted_iota(jnp.int32, (tq, n_cmp_pad), 0)
    nn = lax.broadcasted_iota(jnp.int32, (tq, n_cmp_pad), 1)
    p_cmp = _softmax_rows(s, ((nn * CMP_STRIDE + (CMP_BLOCK - 1) <= qpos_c) & (nn < n_cmp))[None])
    o_cmp = _dot_nt(p_cmp.astype(BF16).reshape(hg * tq, n_cmp_pad), cmp_ref[1])

    p_sum = jnp.sum(p_cmp, axis=0)
    score_t = _dot_nt(cover_ref[...], p_sum, precision=lax.Precision.HIGHEST)
    cur = (q0 + lax.broadcasted_iota(jnp.int32, (n_sel, tq), 1)) // SEL_BLOCK
    sel_q = _select_blocks(score_t, cur, n_sel).T.astype(BF16)

    start = pl.multiple_of(jnp.maximum(q0 + tq - n_win, 0), tq)
    kw = kw_ref[pl.ds(start, n_win), :]
    vw = vw_ref[pl.ds(start, n_win), :]
    dlt = (q0 + lax.broadcasted_iota(jnp.int32, (tq, n_win), 0)) - (
        start + lax.broadcasted_iota(jnp.int32, (tq, n_win), 1))
    mask_win = ((dlt >= 0) & (dlt <= WINDOW))[None]
    s = _scores(q, kw).reshape(hg, tq, n_win)
    p_win, inv_win = _exp_rows(s, mask_win)
    o_win = jnp.dot(p_win.astype(BF16).reshape(hg * tq, n_win), vw, preferred_element_type=F32)
    o_win = o_win * inv_win.reshape(hg * tq, 1)

    gate = jax.nn.sigmoid(gate_ref[...])

    def selected_and_store(n_keys):
        sel_keys = jnp.dot(sel_q, expand_ref[:, :n_keys], preferred_element_type=F32)
        qpos = q0 + lax.broadcasted_iota(jnp.int32, (tq, n_keys), 0)
        kpos = lax.broadcasted_iota(jnp.int32, (tq, n_keys), 1)
        mask_slc = ((sel_keys > 0.5) & (kpos <= qpos))[None]
        s_slc = _scores(q, ks_ref[:n_keys, :]).reshape(hg, tq, n_keys)
        p_slc, inv = _exp_rows(s_slc, mask_slc)
        o_slc = jnp.dot(p_slc.astype(BF16).reshape(hg * tq, n_keys), vs_ref[:n_keys, :], preferred_element_type=F32)
        o_slc = o_slc * inv.reshape(hg * tq, 1)
        for h in range(hg):
            rows = slice(h * tq, (h + 1) * tq)
            o = (gate[:, 3 * h:3 * h + 1] * o_cmp[rows] + gate[:, 3 * h + 1:3 * h + 2] * o_slc[rows]
                 + gate[:, 3 * h + 2:3 * h + 3] * o_win[rows])
            o_ref[:, h * HEAD_DIM:(h + 1) * HEAD_DIM] = o.astype(o_ref.dtype)

    _for_causal_extent(qi, tq, T, selected_and_store)


def nsa_attention_prompt(q, kv, cmp_t, gates, n_batch, T, *, tq=Q_TILE):
    hg = H_NSA // KV_NSA
    G = KV_NSA
    nq = T // tq
    n_cmp_pad = cmp_t.shape[3]
    n_cmp = (T - CMP_BLOCK) // CMP_STRIDE + 1
    n_sel = -(-T // SEL_BLOCK)
    cover_t = _cover_matrix(n_sel, n_cmp_pad, n_cmp, n_sel)
    expand = jnp.asarray((np.arange(T)[None, :] // SEL_BLOCK == np.arange(n_sel)[:, None]).astype(np.float32),
                         dtype=BF16)
    n_win = min(WINDOW + tq, T)
    kern = functools.partial(_nsa_prompt_body, hg=hg, tq=tq, n_win=n_win, n_cmp=n_cmp)
    unit = lambda j: (lambda b, g, i: (b, j * G + g))
    return pl.pallas_call(
        kern,
        grid=(n_batch, G, nq),
        in_specs=[pl.BlockSpec((tq, hg * HEAD_DIM), lambda b, g, i: (b * nq + i, g)),
                  pl.BlockSpec((None, 2, None, HEAD_DIM, n_cmp_pad), lambda b, g, i: (b, 0, g, 0, 0)),
                  pl.BlockSpec((T, HEAD_DIM), unit(2)),
                  pl.BlockSpec((T, HEAD_DIM), unit(3)),
                  pl.BlockSpec((T, HEAD_DIM), unit(4)),
                  pl.BlockSpec((T, HEAD_DIM), unit(5)),
                  pl.BlockSpec((None, tq, hg * 3), lambda b, g, i: (g, b * nq + i, 0)),
                  pl.BlockSpec((n_sel, n_cmp_pad), lambda b, g, i: (0, 0)),
                  pl.BlockSpec((n_sel, T), lambda b, g, i: (0, 0))],
        out_specs=pl.BlockSpec((tq, hg * HEAD_DIM), lambda b, g, i: (b * nq + i, g)),
        out_shape=jax.ShapeDtypeStruct((n_batch * T, W_NSA), BF16),
        compiler_params=_cparams(3),
        name="nsa_attention_prompt",
    )(q, cmp_t.reshape(n_batch, 2, G, HEAD_DIM, n_cmp_pad), kv, kv, kv, kv, gates, cover_t, expand)


def _nsa_sample_body(pt_ref, *refs, hg, n_cmp, n_sel):
    pages = refs[:PAGES_PER_STEP]
    (q_ref, cmp_ref, new_ref, win_ref, gate_ref, cover_ref, expand_ref, o_ref,
     sel_ref, key_ref, ocmp_ref, m_ref, l_ref, acc_ref) = refs[PAGES_PER_STEP:]
    s_id = pl.program_id(1)
    G = KV_NSA
    n_rows = hg * TOK_PAD
    n_cmp_pad = cmp_ref.shape[3]
    n_sel_pad = cover_ref.shape[0]
    cur = PAST_LEN // SEL_BLOCK
    blocks_per_step = PAGES_PER_STEP * PAGE_SIZE // SEL_BLOCK

    @pl.when(s_id == 0)
    def _():
        _init_state(m_ref, l_ref, acc_ref)
        nn = lax.broadcasted_iota(jnp.int32, (n_rows, n_cmp_pad), 1)
        for g in range(G):
            s = jnp.dot(q_ref[g], cmp_ref[0, g], preferred_element_type=F32) * ATTN_SCALE
            p = _softmax_rows(s, nn < n_cmp)
            ocmp_ref[g] = _dot_nt(p.astype(BF16), cmp_ref[1, g])
            p_sum = jnp.sum(p.reshape(hg, TOK_PAD, n_cmp_pad), axis=0)
            p_sum = jnp.concatenate([p_sum, jnp.zeros((LANES - TOK_PAD, n_cmp_pad), F32)], axis=0)
            score_t = _dot_nt(cover_ref[...], p_sum, precision=lax.Precision.HIGHEST)
            key, valid = _block_keys(score_t, cur)
            key_ref[...] = key
            jj = lax.broadcasted_iota(jnp.int32, key.shape, 0)

            def rank_step(i, rank):
                ki = key_ref[pl.ds(i, 1), :]
                kk = key_ref[...]
                beats = (ki > kk) | ((ki == kk) & (jj > i))
                return rank + jnp.where(beats, 1.0, 0.0)

            rank = lax.fori_loop(0, n_sel, rank_step, jnp.zeros(key.shape, F32))
            sel_ref[g] = jnp.where(valid & (jj < n_sel) & (rank < float(min(SEL_TOPK, n_sel))), 1.0, 0.0)

    row0 = pl.multiple_of(s_id * blocks_per_step, blocks_per_step)
    for g in range(G):
        sel_blk = sel_ref[g, pl.ds(row0, blocks_per_step), :]
        tok_keys = jnp.dot(sel_blk.T[:TOK_PAD].astype(BF16), expand_ref[...], preferred_element_type=F32)
        mask = jnp.concatenate([tok_keys] * hg, axis=0) > 0.5
        s = _scores(q_ref[g], _page_unit(pages, 2 * G + g))
        state = _online_step(_load_state(m_ref, l_ref, acc_ref, g), s, mask, _page_unit(pages, 3 * G + g))
        _store_state(m_ref, l_ref, acc_ref, g, state)

    @pl.when(s_id == pl.num_programs(1) - 1)
    def _():
        new = new_ref[...]
        mask_new = _new_key_mask(n_rows)
        t_row = _token_of_row((n_rows, WINDOW))
        i_key = lax.broadcasted_iota(jnp.int32, (n_rows, WINDOW), 1)
        mask_win = i_key >= t_row
        unit = lambda j, g: new[:, (j * G + g) * LANES:(j * G + g + 1) * LANES]
        for g in range(G):
            q = q_ref[g]
            o_slc = _online_finish(_online_step(_load_state(m_ref, l_ref, acc_ref, g), _scores(q, unit(2, g)),
                                                mask_new, unit(3, g)))
            kw = win_ref[pl.ds(g, WINDOW, stride=2 * G), :].astype(BF16)
            vw = win_ref[pl.ds(G + g, WINDOW, stride=2 * G), :].astype(BF16)
            init = (jnp.full((n_rows, 1), NEG_BIG, F32), jnp.zeros((n_rows, 1), F32),
                    jnp.zeros((n_rows, HEAD_DIM), F32))
            state = _online_step(init, _scores(q, kw), mask_win, vw)
            o_win = _online_finish(_online_step(state, _scores(q, unit(4, g)), mask_new, unit(5, g)))
            gate = jax.nn.sigmoid(gate_ref[g])
            o_ref[g] = gate[:, 0:1] * ocmp_ref[g] + gate[:, 1:2] * o_slc + gate[:, 2:3] * o_win


def nsa_attention_sample(pt_flat, cache, page0, q, cmp_t, new_kv, win, win0, gates):
    Bs = q.shape[0]
    G = KV_NSA
    hg = H_NSA // KV_NSA
    n_rows = hg * TOK_PAD
    n_k = PAST_LEN + DEC_SEQ
    n_cmp = (n_k - CMP_BLOCK) // CMP_STRIDE + 1
    n_cmp_pad = cmp_t.shape[3]
    n_sel = -(-n_k // SEL_BLOCK)
    blocks_per_step = PAGES_PER_STEP * PAGE_SIZE // SEL_BLOCK
    n_sel_pad = -(-n_sel // blocks_per_step) * blocks_per_step
    cover_t = _cover_matrix(n_sel_pad, n_cmp_pad, n_cmp, n_sel)
    keys_per_step = PAGES_PER_STEP * PAGE_SIZE
    expand = jnp.asarray((np.arange(keys_per_step)[None, :] // SEL_BLOCK
                          == np.arange(blocks_per_step)[:, None]).astype(np.float32), dtype=BF16)
    kern = functools.partial(_nsa_sample_body, hg=hg, n_cmp=n_cmp, n_sel=n_sel)
    grid_spec = pltpu.PrefetchScalarGridSpec(
        num_scalar_prefetch=1,
        grid=(Bs, N_PAGES // PAGES_PER_STEP),
        in_specs=_page_specs(page0, PAGES_PER_STEP) + [
            pl.BlockSpec((None, G, n_rows, HEAD_DIM), lambda b, s, *_: (b, 0, 0, 0)),
            pl.BlockSpec((None, 2, G, HEAD_DIM, n_cmp_pad), lambda b, s, *_: (b, 0, 0, 0, 0)),
            pl.BlockSpec((None, NEW_PAD, new_kv.shape[2]), lambda b, s, *_: (b, 0, 0)),
            pl.BlockSpec((WINDOW * 2 * G, LANES), lambda b, s, *_: (win0 + b, 0)),
            pl.BlockSpec((None, G, n_rows, 3), lambda b, s, *_: (b, 0, 0, 0)),
            pl.BlockSpec((n_sel_pad, n_cmp_pad), lambda b, s, *_: (0, 0)),
            pl.BlockSpec((blocks_per_step, keys_per_step), lambda b, s, *_: (0, 0))],
        out_specs=pl.BlockSpec((None, G, n_rows, HEAD_DIM), lambda b, s, *_: (b, 0, 0, 0)),
        scratch_shapes=[pltpu.VMEM((G, n_sel_pad, LANES), F32), pltpu.VMEM((n_sel_pad, LANES), F32),
                        pltpu.VMEM((G, n_rows, HEAD_DIM), F32),
                        pltpu.VMEM((G, n_rows, LANES), F32), pltpu.VMEM((G, n_rows, LANES), F32),
                        pltpu.VMEM((G, n_rows, HEAD_DIM), F32)],
    )
    return pl.pallas_call(
        kern,
        grid_spec=grid_spec,
        out_shape=jax.ShapeDtypeStruct((Bs, G, n_rows, HEAD_DIM), F32),
        compiler_params=_cparams(2),
        name="nsa_attention_sample",
    )(pt_flat, *([cache] * PAGES_PER_STEP), q, cmp_t.reshape(Bs, 2, G, HEAD_DIM, n_cmp_pad), new_kv, win, gates,
      cover_t, expand)


def _expert_changed(te_ref, i):
    prev = te_ref[jnp.maximum(i - 1, 0)]
    return (i == 0) | (te_ref[i] != prev)


def _moe_up_body(te_ref, nu_ref, x_ref, wg_ref, wu_ref, bg_ref, bu_ref, o_ref, wgb, wub):
    i = pl.program_id(1)

    @pl.when(_expert_changed(te_ref, i))
    def _():
        _cast_weight_tile(wg_ref, wgb)
        _cast_weight_tile(wu_ref, wub)

    @pl.when(i < nu_ref[0])
    def _():
        x = x_ref[...]
        g = jnp.dot(x, wgb[...], preferred_element_type=F32) + bg_ref[...]
        u = jnp.dot(x, wub[...], preferred_element_type=F32) + bu_ref[...]
        g = jnp.minimum(g, SWIGLU_LIMIT)
        u = jnp.clip(u, -SWIGLU_LIMIT, SWIGLU_LIMIT)
        o_ref[...] = ((u + 1.0) * (g * jax.nn.sigmoid(SWIGLU_ALPHA * g))).astype(o_ref.dtype)


def moe_up(tile_expert, n_used, x_sorted, w_gate_up, b_gate_up, layer, *, tn=512):
    R, D = x_sorted.shape
    de = w_gate_up.shape[3] // 2
    tn = min(tn, de)
    nj = de // tn
    tm = MOE_TILE
    grid_spec = pltpu.PrefetchScalarGridSpec(
        num_scalar_prefetch=2,
        grid=(nj, R // tm),
        in_specs=[pl.BlockSpec((tm, D), lambda j, i, te, nu: (i, 0)),
                  pl.BlockSpec((None, None, D, tn), lambda j, i, te, nu: (layer, te[i], 0, j)),
                  pl.BlockSpec((None, None, D, tn), lambda j, i, te, nu: (layer, te[i], 0, j + nj)),
                  pl.BlockSpec((None, None, 1, tn), lambda j, i, te, nu: (layer, te[i], 0, j)),
                  pl.BlockSpec((None, None, 1, tn), lambda j, i, te, nu: (layer, te[i], 0, j + nj))],
        out_specs=pl.BlockSpec((tm, tn), lambda j, i, te, nu: (i, j)),
        scratch_shapes=[pltpu.VMEM((D, tn), BF16), pltpu.VMEM((D, tn), BF16)],
    )
    return pl.pallas_call(
        _moe_up_body,
        grid_spec=grid_spec,
        out_shape=jax.ShapeDtypeStruct((R, de), BF16),
        compiler_params=_cparams(2, V7X_VMEM_LIMIT_LARGE_BYTES),
        name="moe_up",
    )(tile_expert, n_used, x_sorted, w_gate_up, w_gate_up, b_gate_up, b_gate_up)


def _moe_down_body(te_ref, nu_ref, a_ref, w_ref, b_ref, o_ref, wb):
    i = pl.program_id(1)

    @pl.when(_expert_changed(te_ref, i))
    def _():
        _cast_weight_tile(w_ref, wb)

    @pl.when(i < nu_ref[0])
    def _():
        o_ref[...] = (jnp.dot(a_ref[...], wb[...], preferred_element_type=F32) + b_ref[...]).astype(o_ref.dtype)


def moe_down(tile_expert, n_used, act, w_down, b_down, layer, *, tn=1024):
    R, de = act.shape
    D = w_down.shape[3]
    tn = min(tn, D)
    tm = MOE_TILE
    grid_spec = pltpu.PrefetchScalarGridSpec(
        num_scalar_prefetch=2,
        grid=(D // tn, R // tm),
        in_specs=[pl.BlockSpec((tm, de), lambda j, i, te, nu: (i, 0)),
                  pl.BlockSpec((None, None, de, tn), lambda j, i, te, nu: (layer, te[i], 0, j)),
                  pl.BlockSpec((None, None, 1, tn), lambda j, i, te, nu: (layer, te[i], 0, j))],
        out_specs=pl.BlockSpec((tm, tn), lambda j, i, te, nu: (i, j)),
        scratch_shapes=[pltpu.VMEM((de, tn), BF16)],
    )
    return pl.pallas_call(
        _moe_down_body,
        grid_spec=grid_spec,
        out_shape=jax.ShapeDtypeStruct((R, D), BF16),
        compiler_params=_cparams(2),
        name="moe_down",
    )(tile_expert, n_used, act, w_down, b_down)


RANK_TILE = 256


def _rank_body(e_ref, tri_ref, within_ref, counts_ref, carry_ref):
    @pl.when(pl.program_id(0) == 0)
    def _():
        carry_ref[...] = jnp.zeros(carry_ref.shape, F32)

    e = e_ref[0]
    onehot = jnp.where(e == lax.broadcasted_iota(jnp.int32, (N_EXPERTS, RANK_TILE), 0), 1.0, 0.0)
    incl = jnp.dot(onehot.astype(BF16), tri_ref[...], preferred_element_type=F32)
    carry = carry_ref[...]
    pos = jnp.sum(onehot * (incl - 1.0 + carry[:, :1]), axis=0, keepdims=True)
    within_ref[0] = pos.astype(jnp.int32)
    carry_ref[...] = carry + jnp.sum(onehot, axis=1, keepdims=True)
    counts_ref[...] = carry_ref[...]


def expert_ranks(e_flat):
    n = e_flat.shape[0]
    n_blocks = -(-n // RANK_TILE)
    e_pad = jnp.pad(e_flat.astype(jnp.int32), (0, n_blocks * RANK_TILE - n), constant_values=-1)
    tri = jnp.asarray(np.triu(np.ones((RANK_TILE, RANK_TILE), np.float32)), dtype=BF16)
    within, counts = pl.pallas_call(
        _rank_body,
        grid=(n_blocks,),
        in_specs=[pl.BlockSpec((1, 1, RANK_TILE), lambda i: (i, 0, 0)),
                  pl.BlockSpec((RANK_TILE, RANK_TILE), lambda i: (0, 0))],
        out_specs=[pl.BlockSpec((1, 1, RANK_TILE), lambda i: (i, 0, 0)),
                   pl.BlockSpec((N_EXPERTS, LANES), lambda i: (0, 0))],
        out_shape=[jax.ShapeDtypeStruct((n_blocks, 1, RANK_TILE), jnp.int32),
                   jax.ShapeDtypeStruct((N_EXPERTS, LANES), F32)],
        scratch_shapes=[pltpu.VMEM((N_EXPERTS, LANES), F32)],
        compiler_params=_cparams(1),
        name="moe_expert_ranks",
    )(e_pad.reshape(n_blocks, 1, RANK_TILE), tri)
    return within.reshape(-1)[:n], counts[:, 0].astype(jnp.int32)


def moe_ffn(h_bf, w_router, b_router, w_gate_up, b_gate_up, w_down, b_down, layer):
    n_tok, D = h_bf.shape
    E = w_router.shape[2]
    tm = MOE_TILE
    logits = matmul_bias(h_bf, w_router, b_router[layer][None, :], tm=512, tn=E, layer=layer, name="router")
    top_v, top_i = lax.top_k(logits, TOP_K)
    gates = jax.nn.softmax(top_v, axis=-1)
    n_assign = n_tok * TOP_K
    e_flat = top_i.reshape(-1)
    within, counts = expert_ranks(e_flat)
    padded = (counts + tm - 1) // tm * tm
    pad_end = jnp.cumsum(padded)
    dest = (pad_end - padded)[e_flat] + within
    n_tiles = -(-(n_assign + E * (tm - 1)) // tm)
    tok_flat = jnp.arange(n_assign, dtype=jnp.int32) // TOP_K
    row_tok = jnp.full((n_tiles * tm,), n_tok, jnp.int32).at[dest].set(tok_flat)
    tile_expert = jnp.minimum(jnp.searchsorted(pad_end, jnp.arange(n_tiles) * tm, side='right'),
                              E - 1).astype(jnp.int32)
    n_used = (pad_end[-1] // tm).astype(jnp.int32).reshape(1)
    x_pad = jnp.concatenate([h_bf, jnp.zeros((1, D), BF16)], axis=0)
    x_sorted = x_pad[row_tok]
    act = moe_up(tile_expert, n_used, x_sorted, w_gate_up, b_gate_up[:, :, None, :], layer)
    y = moe_down(tile_expert, n_used, act, w_down, b_down[:, :, None, :], layer)
    n_pad = -(-n_tok // COMBINE_TILE) * COMBINE_TILE
    dest_k = jnp.pad(dest.reshape(n_tok, TOP_K).T, ((0, 0), (0, n_pad - n_tok)))
    return y[dest_k], gates


def _rope_tables(pos):
    half = HEAD_DIM // 2
    inv = ROPE_THETA ** (-jnp.arange(half, dtype=F32) / half)
    ang = pos.astype(F32)[:, None] * inv[None, :]
    cos, sin = jnp.cos(ang), jnp.sin(ang)
    return jnp.concatenate([cos, cos], -1), jnp.concatenate([-sin, sin], -1)


def _unit_flags(pattern):
    return jnp.asarray(np.repeat(np.asarray(pattern, np.float32), LANES)[None, :])


def _diff_scalars(diff_lambda_l, l):
    lam_init = 0.8 - 0.6 * math.exp(-0.3 * l)
    dl = diff_lambda_l.astype(F32)
    lam = jnp.exp(jnp.sum(dl[0] * dl[1])) - jnp.exp(jnp.sum(dl[2] * dl[3])) + lam_init
    return jnp.stack([lam, jnp.asarray(1.0 - lam_init, F32)]).astype(F32)


def _sample_q_rows(q, n_groups, hg, n_comp):
    Bs, Ts, _ = q.shape
    q = q.reshape(Bs, Ts, n_groups, hg, n_comp, HEAD_DIM)
    q = jnp.pad(q, ((0, 0), (0, TOK_PAD - Ts), (0, 0), (0, 0), (0, 0), (0, 0)))
    q = q.transpose(0, 2, 4, 3, 1, 5).reshape(Bs, n_groups, n_comp, hg * TOK_PAD, HEAD_DIM)
    return q[:, :, 0] if n_comp == 1 else q


def _sample_out_rows(o, hg, Ts):
    Bs, G, _, dv = o.shape
    o = o.reshape(Bs, G, hg, TOK_PAD, dv)[:, :, :, :Ts]
    return o.transpose(0, 3, 1, 2, 4).reshape(Bs * Ts, G * hg * dv)


def _pad_new_rows(a):
    return jnp.pad(a, ((0, 0), (0, NEW_PAD - a.shape[1]), (0, 0)))


def kernel(x_prompt, x_sample, cache_diff_kv, cache_nsa_kv, cache_nsa_win, cache_fox_kv, cache_fox_logf, page_table, w_in, b_in, diff_lambda, diff_subln, nsa_cmp_pe, nsa_cmp_w1, nsa_cmp_w2, w_branch_diff, w_branch_nsa, w_branch_fox, w_out, ln1_g, ln1_b, w_router, b_router, w_gate_up, b_gate_up, w_down, b_down, ln2_g, ln2_b):
    B, T, D = x_prompt.shape
    Bs, Ts, _ = x_sample.shape
    n_p = B * T
    n_s = Bs * Ts
    n_pool = cache_diff_kv.shape[1]
    x = jnp.concatenate([x_prompt.reshape(n_p, D), x_sample.reshape(n_s, D)], axis=0)
    x_bf = x.astype(BF16)
    pos = jnp.concatenate([jnp.tile(jnp.arange(T, dtype=jnp.int32), B),
                           jnp.tile(PAST_LEN + jnp.arange(Ts, dtype=jnp.int32), Bs)])
    cos, sin = _rope_tables(pos)
    pt_flat = page_table.reshape(-1).astype(jnp.int32)
    zero_bias = jnp.zeros((1, D), F32)
    flag_all = lambda n: _unit_flags([1] * n)
    flag_diff_kv = _unit_flags([1] * (KV_DIFF * 2) + [0] * (KV_DIFF * 2))
    flag_nsa_kv = _unit_flags(([1] * KV_NSA + [0] * KV_NSA) * 3)
    hg_d, hg_n, hg_f = H_DIFF // KV_DIFF, H_NSA // KV_NSA, H_FOX // KV_FOX
    o_qd, o_kvd, o_qn, o_kvn, o_gn, o_qf, o_kvf, o_ff, o_gm = IN_OFFS[:9]
    nsa_rows, fox_rows = _cache_rows(cache_nsa_kv), _cache_rows(cache_fox_kv)
    diff_units = cache_diff_kv.reshape(cache_diff_kv.shape[:4] + (KV_DIFF, 2, LANES))
    diff_rows = _cache_rows(jnp.swapaxes(diff_units, 4, 5))
    win_rows = cache_nsa_win.reshape(-1, LANES)
    assert cache_nsa_win.shape[2] == WINDOW

    st_p, st_s = [], []
    for l in range(DEPTH):
        bl = b_in[l][None, :]
        _, qd = project(x_bf, w_in, bl, o_qd, W_DIFF, flag_all(H_DIFF * 2), cos, sin, with_f32=False, layer=l,
                        name="proj_diff_q")
        kvd_f, kvd = project(x_bf, w_in, bl, o_kvd, IN_SIZES[1], flag_diff_kv, cos, sin, with_f32=True, layer=l,
                             name="proj_diff_kv")
        _, qn = project(x_bf, w_in, bl, o_qn, W_NSA, flag_all(H_NSA), cos, sin, with_f32=False, layer=l,
                        name="proj_nsa_q")
        kvn_f, kvn = project(x_bf, w_in, bl, o_kvn, IN_SIZES[3], flag_nsa_kv, cos, sin, with_f32=True, layer=l,
                             name="proj_nsa_kv")
        w_fox = realign_columns(w_in, l, o_qf, o_ff - o_qf, F32)
        b_fox = bl[:, o_qf:o_ff]
        _, qf = project(x_bf, w_fox, b_fox, 0, W_FOX, None, cos, sin, with_f32=False, name="proj_fox_q")
        kvf_f, kvf = project(x_bf, w_fox, b_fox, W_FOX, IN_SIZES[6], None, cos, sin, with_f32=True,
                             name="proj_fox_kv")
        u_gn, u_ff = o_gn // LANES * LANES, o_ff // LANES * LANES
        z_gn, _ = project(x_bf, w_in, bl, u_gn, LANES, None, cos, sin, with_f32=True, layer=l, tn=LANES,
                          name="proj_nsa_gate")
        z_ff, _ = project(x_bf, w_in, bl, u_ff, LANES, None, cos, sin, with_f32=True, layer=l, tn=LANES,
                          name="proj_fox_forget")
        zg_n = z_gn[:, o_gn - u_gn:o_gn - u_gn + IN_SIZES[4]]
        logf = jax.nn.log_sigmoid(z_ff[:, o_ff - u_ff:o_ff - u_ff + IN_SIZES[7]])
        scalars = _diff_scalars(diff_lambda[l], l)
        subln_g = diff_subln[l][None, :]
        pe, w1t, w2t = _compress_weights(nsa_cmp_pe[l], nsa_cmp_w1[l], nsa_cmp_w2[l])

        o_d_p = diff_attention_prompt(qd, kvd, scalars, subln_g, B, T)
        cmp_p = compress_prompt(kvn_f, pe, w1t, w2t, B, T)
        gates_n = zg_n.reshape(-1, KV_NSA, hg_n * 3).transpose(1, 0, 2)
        o_n_p = nsa_attention_prompt(qn, kvn, cmp_p, gates_n, B, T)
        c_p = jnp.cumsum(logf[:n_p].reshape(B, T, H_FOX), axis=1)
        c_col = jnp.pad(c_p.reshape(n_p, KV_FOX, hg_f), ((0, n_s), (0, 0), (0, 0))).transpose(1, 0, 2)
        c_row = c_p.reshape(B, T, KV_FOX, hg_f).transpose(0, 2, 3, 1)
        o_f_p = fox_attention_prompt(qf, kvf, c_col, c_row, B, T)

        rows_s = lambda a: a[n_p:].reshape(Bs, Ts, a.shape[1])
        page0 = l * n_pool
        o = diff_attention_sample(pt_flat, scalars, diff_rows, page0,
                                  _sample_q_rows(rows_s(qd), KV_DIFF, hg_d, 2), _pad_new_rows(rows_s(kvd)), subln_g)
        o_d_s = _sample_out_rows(o, hg_d, Ts)
        cmp_s = compress_paged(pt_flat, nsa_rows, page0, pe, w1t, w2t, Bs)
        gates_s = jnp.pad(rows_s(zg_n).reshape(Bs, Ts, KV_NSA, hg_n, 3), ((0, 0), (0, TOK_PAD - Ts), (0, 0), (0, 0), (0, 0)))
        gates_s = gates_s.transpose(0, 2, 3, 1, 4).reshape(Bs, KV_NSA, hg_n * TOK_PAD, 3)
        o = nsa_attention_sample(pt_flat, nsa_rows, page0,
                                 _sample_q_rows(rows_s(qn), KV_NSA, hg_n, 1), cmp_s, _pad_new_rows(rows_s(kvn)),
                                 win_rows, l * Bs, gates_s)
        o_n_s = _sample_out_rows(o, hg_n, Ts)
        lf_past = cache_fox_logf[l][page_table].astype(F32)
        later = lambda n: jnp.asarray(np.triu(np.ones((n, n), np.float32), 1))
        in_page = jnp.einsum('ji,bpih->bpjh', later(PAGE_SIZE), lf_past, precision=lax.Precision.HIGHEST)
        pages_after = jnp.einsum('qp,bph->bqh', later(N_PAGES), jnp.sum(lf_past, axis=2),
                                 precision=lax.Precision.HIGHEST)
        ck_past = -(in_page + pages_after[:, :, None, :]).reshape(Bs, PAST_LEN, H_FOX)
        ck_past = ck_past.reshape(Bs, PAST_LEN, KV_FOX, hg_f).transpose(0, 2, 3, 1)
        c_new = jnp.cumsum(rows_s(logf), axis=1)
        c_new_g = c_new.reshape(Bs, Ts, KV_FOX, hg_f).transpose(0, 2, 3, 1)
        cq_col = jnp.pad(c_new_g, ((0, 0), (0, 0), (0, 0), (0, TOK_PAD - Ts))).reshape(Bs, KV_FOX, hg_f * TOK_PAD, 1)
        ck_new = jnp.pad(c_new_g, ((0, 0), (0, 0), (0, 0), (0, NEW_PAD - Ts)))
        o = fox_attention_sample(pt_flat, fox_rows, page0,
                                 _sample_q_rows(rows_s(qf), KV_FOX, hg_f, 1), ck_past, cq_col, ck_new,
                                 _pad_new_rows(rows_s(kvf)))
        o_f_s = _sample_out_rows(o, hg_f, Ts)

        new_d, new_n, new_w = kvd_f, kvn_f[:, :4 * KV_NSA * HEAD_DIM], kvn_f[:, 4 * KV_NSA * HEAD_DIM:]
        win_p = new_w[:n_p].reshape(B, T, 2, KV_NSA, HEAD_DIM)[:, T - min(WINDOW, T):]
        win_s = jnp.concatenate([cache_nsa_win[l], new_w[n_p:].reshape(Bs, Ts, 2, KV_NSA, HEAD_DIM)],
                                axis=1)[:, -cache_nsa_win.shape[2]:]
        st_p.append((new_d[:n_p].reshape(B, T, 2, KV_DIFF, 2 * HEAD_DIM), new_n[:n_p].reshape(B, T, 4, KV_NSA, HEAD_DIM),
                     win_p, kvf_f[:n_p].reshape(B, T, 2, KV_FOX, HEAD_DIM), logf[:n_p].reshape(B, T, H_FOX)))
        st_s.append((new_d[n_p:].reshape(Bs, Ts, 2, KV_DIFF, 2 * HEAD_DIM), new_n[n_p:].reshape(Bs, Ts, 4, KV_NSA, HEAD_DIM),
                     win_s, kvf_f[n_p:].reshape(Bs, Ts, 2, KV_FOX, HEAD_DIM), logf[n_p:].reshape(Bs, Ts, H_FOX)))

        o_d = jnp.concatenate([o_d_p, o_d_s.astype(BF16)], axis=0)
        o_n = jnp.concatenate([o_n_p, o_n_s.astype(BF16)], axis=0)
        o_f = jnp.concatenate([o_f_p, o_f_s.astype(BF16)], axis=0)
        w_gate = realign_columns(w_in, l, o_gm, 3 * D, BF16)
        b_gate = bl[:, o_gm:].reshape(3, 1, D)
        merged = merge_branches(x_bf, o_d, o_n, o_f, w_gate, b_gate, w_branch_diff, w_branch_nsa, w_branch_fox, l)
        y = matmul_bias(merged, w_out, zero_bias, tm=512, tn=512, layer=l, name="out_proj")
        h, h_bf = residual_layer_norm(x, y, ln1_g[l][None, :], ln1_b[l][None, :])
        y_rows, gates = moe_ffn(h_bf, w_router, b_router, w_gate_up, b_gate_up, w_down, b_down, l)
        x, x_bf = combine_layer_norm(h, y_rows, gates, ln2_g[l][None, :], ln2_b[l][None, :])
    y_prompt = x[:n_p].reshape(B, T, D)
    y_sample = x[n_p:].reshape(Bs, Ts, D)
    stack = lambda sts, k: jnp.stack([s[k] for s in sts], 0)
    return (y_prompt, y_sample,
            stack(st_p, 0), stack(st_p, 1), stack(st_p, 2), stack(st_p, 3), stack(st_p, 4),
            stack(st_s, 0), stack(st_s, 1), stack(st_s, 2), stack(st_s, 3), stack(st_s, 4))
```

```python
import functools
import math

import numpy as np
import jax
import jax.numpy as jnp
from jax import lax
from jax.experimental import pallas as pl
from jax.experimental.pallas import tpu as pltpu

D_MODEL = 4096
BATCH = 4
SEQ = 2048
DEPTH = 2
DEC_BATCH = 8
DEC_SEQ = 4
PAST_LEN = 16384
PAGE_SIZE = 128

HEAD_DIM = 128
ROPE_THETA = 10000.0
H_DIFF = 4
KV_DIFF = 2
H_NSA = 12
KV_NSA = 2
H_FOX = 12
KV_FOX = 4
W_DIFF = H_DIFF * 2 * HEAD_DIM
W_NSA = H_NSA * HEAD_DIM
W_FOX = H_FOX * HEAD_DIM
CMP_BLOCK = 32
CMP_STRIDE = 16
SEL_BLOCK = 64
SEL_TOPK = 16
WINDOW = 512
N_EXPERTS = 32
TOP_K = 4
D_EXPERT = D_MODEL // 2
SWIGLU_ALPHA = 1.702
SWIGLU_LIMIT = 7.0
LN_EPS = 1e-5
DEEPNORM_ALPHA = (2 * DEPTH) ** 0.25
ATTN_SCALE = HEAD_DIM ** -0.5
IN_SIZES = (
    H_DIFF * 2 * HEAD_DIM,
    2 * KV_DIFF * 2 * HEAD_DIM,
    H_NSA * HEAD_DIM,
    6 * KV_NSA * HEAD_DIM,
    3 * H_NSA,
    H_FOX * HEAD_DIM,
    2 * KV_FOX * HEAD_DIM,
    H_FOX,
    3 * D_MODEL,
)
N_IN = sum(IN_SIZES)
IN_OFFS = tuple(int(v) for v in np.cumsum((0,) + IN_SIZES))
N_PAGES = PAST_LEN // PAGE_SIZE

F32 = jnp.float32
BF16 = jnp.bfloat16
NEG_BIG = -1e30
LANES = 128
SUBLANES = 8

V7X_VMEM_LIMIT_BYTES = 48 * 1024 * 1024
Q_TILE = 128
CAUSAL_STEP = 512
V7X_VMEM_LIMIT_LARGE_BYTES = 56 * 1024 * 1024
MOE_TILE = 256
TOK_PAD = SUBLANES
NEW_PAD = LANES
ROW_UNITS = 8
PAGES_PER_STEP = 16
CMP_PAGES_PER_STEP = 16

assert DEC_SEQ <= TOK_PAD and HEAD_DIM == LANES and PAGE_SIZE % SEL_BLOCK == 0
assert N_PAGES % PAGES_PER_STEP == 0 and N_PAGES % CMP_PAGES_PER_STEP == 0
assert CMP_BLOCK == 2 * CMP_STRIDE and PAGE_SIZE % CMP_STRIDE == 0 and SEQ % CMP_STRIDE == 0
assert ((PAST_LEN + DEC_SEQ - CMP_BLOCK) // CMP_STRIDE) * CMP_STRIDE + CMP_BLOCK <= PAST_LEN
assert PAST_LEN % SEL_BLOCK == 0 and DEC_SEQ <= SEL_BLOCK and WINDOW <= PAST_LEN and SEQ >= WINDOW


def _cparams(n_axes, vmem_limit_bytes=V7X_VMEM_LIMIT_BYTES):
    return pltpu.CompilerParams(dimension_semantics=("arbitrary",) * n_axes,
                                vmem_limit_bytes=vmem_limit_bytes)


def _for_causal_extent(qi, tq, n_keys_total, fn):
    step = min(CAUSAL_STEP, n_keys_total)
    need = (qi * tq + tq + step - 1) // step
    for c in range(1, n_keys_total // step + 1):
        @pl.when(need == c)
        def _():
            fn(c * step)


def _cast_weight_tile(w_ref, wbf_ref, rows_per_chunk=256):
    n_chunks = w_ref.shape[0] // rows_per_chunk

    def body(c, carry):
        r = pl.multiple_of(c * rows_per_chunk, rows_per_chunk)
        wbf_ref[pl.ds(r, rows_per_chunk), :] = w_ref[pl.ds(r, rows_per_chunk), :].astype(BF16)
        return carry

    lax.fori_loop(0, n_chunks, body, 0)


def _mm_body(x_ref, w_ref, b_ref, o_ref, wbf_ref):
    @pl.when(pl.program_id(1) == 0)
    def _():
        _cast_weight_tile(w_ref, wbf_ref)

    acc = jnp.dot(x_ref[...], wbf_ref[...], preferred_element_type=F32)
    o_ref[...] = (acc + b_ref[...]).astype(o_ref.dtype)


def _weight_spec(w, layer, k_rows, tn, col_block):
    if w.ndim == 2:
        return pl.BlockSpec((k_rows, tn), lambda j, i: (0, col_block(j)))
    return pl.BlockSpec((None, k_rows, tn), lambda j, i: (layer, 0, col_block(j)))


def matmul_bias(x, w, b, *, tm, tn, layer=0, out_dtype=F32, name="matmul_bias"):
    M, K = x.shape
    N = w.shape[-1]
    tn = min(tn, N)
    tm = min(tm, M)
    return pl.pallas_call(
        _mm_body,
        grid=(pl.cdiv(N, tn), pl.cdiv(M, tm)),
        in_specs=[pl.BlockSpec((tm, K), lambda j, i: (i, 0)),
                  _weight_spec(w, layer, K, tn, lambda j: j),
                  pl.BlockSpec((1, tn), lambda j, i: (0, j))],
        out_specs=pl.BlockSpec((tm, tn), lambda j, i: (i, j)),
        out_shape=jax.ShapeDtypeStruct((M, N), out_dtype),
        scratch_shapes=[pltpu.VMEM((K, tn), BF16)],
        compiler_params=_cparams(2),
        name=name,
    )(x, w, b)


def _proj_body(x_ref, w_ref, b_ref, flag_ref, cos_ref, sin_ref, *rest, with_f32, use_rope):
    if with_f32:
        of_ref, ob_ref, wbf_ref = rest
    else:
        ob_ref, wbf_ref = rest

    @pl.when(pl.program_id(1) == 0)
    def _():
        _cast_weight_tile(w_ref, wbf_ref)

    acc = jnp.dot(x_ref[...], wbf_ref[...], preferred_element_type=F32) + b_ref[...]
    for u in range(acc.shape[1] // LANES):
        cols = slice(u * LANES, (u + 1) * LANES)
        z = acc[:, cols]
        if use_rope:
            f = flag_ref[:, cols]
            cos = 1.0 + f * (cos_ref[...] - 1.0)
            sin = f * sin_ref[...]
            z = z * cos + pltpu.roll(z, HEAD_DIM // 2, axis=1) * sin
        if with_f32:
            of_ref[:, cols] = z
        ob_ref[:, cols] = z.astype(BF16)


def project(x, w, b, col0, n_cols, rope_flag, cos, sin, *, with_f32, name, layer=0, tm=512, tn=512):
    M, K = x.shape
    assert col0 % tn == 0 and n_cols % tn == 0
    j0 = col0 // tn
    use_rope = rope_flag is not None
    if not use_rope:
        rope_flag = jnp.zeros((1, n_cols), F32)
    kern = functools.partial(_proj_body, with_f32=with_f32, use_rope=use_rope)
    out_block = pl.BlockSpec((tm, tn), lambda j, i: (i, j))
    outs = pl.pallas_call(
        kern,
        grid=(n_cols // tn, pl.cdiv(M, tm)),
        in_specs=[pl.BlockSpec((tm, K), lambda j, i: (i, 0)),
                  _weight_spec(w, layer, K, tn, lambda j: j + j0),
                  pl.BlockSpec((1, tn), lambda j, i: (0, j + j0)),
                  pl.BlockSpec((1, tn), lambda j, i: (0, j)),
                  pl.BlockSpec((tm, LANES), lambda j, i: (i, 0)),
                  pl.BlockSpec((tm, LANES), lambda j, i: (i, 0))],
        out_specs=[out_block, out_block] if with_f32 else [out_block],
        out_shape=([jax.ShapeDtypeStruct((M, n_cols), F32)] if with_f32 else [])
        + [jax.ShapeDtypeStruct((M, n_cols), BF16)],
        scratch_shapes=[pltpu.VMEM((K, tn), BF16)],
        compiler_params=_cparams(2),
        name=name,
    )(x, w, b, rope_flag, cos, sin)
    return (outs[0], outs[1]) if with_f32 else (None, outs[0])


def _realign_body(a_ref, b_ref, o_ref, *, shift):
    w = jnp.concatenate([a_ref[...], b_ref[...]], axis=1)
    width = w.shape[1]
    o_ref[...] = pltpu.roll(w, width - shift, axis=1)[:, :o_ref.shape[1]].astype(o_ref.dtype)


def realign_columns(w, layer, col0, n_cols, out_dtype, *, tr=512, tn=512):
    _, K, _ = w.shape
    tr = min(tr, K)
    aligned = col0 // tn * tn
    shift = col0 - aligned
    assert shift < LANES and n_cols % tn == 0 and K % tr == 0
    ja, jb, units = aligned // tn, aligned // LANES, tn // LANES
    return pl.pallas_call(
        functools.partial(_realign_body, shift=shift),
        grid=(n_cols // tn, K // tr),
        in_specs=[pl.BlockSpec((None, tr, tn), lambda j, r: (layer, r, ja + j)),
                  pl.BlockSpec((None, tr, LANES), lambda j, r: (layer, r, jb + (j + 1) * units))],
        out_specs=pl.BlockSpec((tr, tn), lambda j, r: (r, j)),
        out_shape=jax.ShapeDtypeStruct((K, n_cols), out_dtype),
        compiler_params=_cparams(2),
        name="realign_columns",
    )(w, w)


def _merge_body(x_ref, od_ref, on_ref, of_ref, wg0_ref, wg1_ref, wg2_ref, bg_ref, wd_ref, wn_ref, wf_ref, o_ref,
                wdb, wnb, wfb):
    @pl.when(pl.program_id(1) == 0)
    def _():
        _cast_weight_tile(wd_ref, wdb)
        _cast_weight_tile(wn_ref, wnb)
        _cast_weight_tile(wf_ref, wfb)

    x = x_ref[...]
    acc = None
    for k, (wg_k, o_k, w_k) in enumerate(((wg0_ref, od_ref, wdb), (wg1_ref, on_ref, wnb), (wg2_ref, of_ref, wfb))):
        gate = jax.nn.sigmoid(jnp.dot(x, wg_k[...], preferred_element_type=F32) + bg_ref[k])
        term = gate * jnp.dot(o_k[...], w_k[...], preferred_element_type=F32)
        acc = term if acc is None else acc + term
    o_ref[...] = acc.astype(o_ref.dtype)


def merge_branches(x, o_d, o_n, o_f, w_gate, b_gate, w_d, w_n, w_f, layer, *, tm=256, tn=256):
    M, D = x.shape
    tn = min(tn, D)
    nj = D // tn
    gate_spec = lambda k: pl.BlockSpec((D, tn), lambda j, i: (0, k * nj + j))
    return pl.pallas_call(
        _merge_body,
        grid=(nj, pl.cdiv(M, tm)),
        in_specs=[pl.BlockSpec((tm, D), lambda j, i: (i, 0)),
                  pl.BlockSpec((tm, W_DIFF), lambda j, i: (i, 0)),
                  pl.BlockSpec((tm, W_NSA), lambda j, i: (i, 0)),
                  pl.BlockSpec((tm, W_FOX), lambda j, i: (i, 0)),
                  gate_spec(0), gate_spec(1), gate_spec(2),
                  pl.BlockSpec((3, 1, tn), lambda j, i: (0, 0, j)),
                  _weight_spec(w_d, layer, W_DIFF, tn, lambda j: j),
                  _weight_spec(w_n, layer, W_NSA, tn, lambda j: j),
                  _weight_spec(w_f, layer, W_FOX, tn, lambda j: j)],
        out_specs=pl.BlockSpec((tm, tn), lambda j, i: (i, j)),
        out_shape=jax.ShapeDtypeStruct((M, D), BF16),
        scratch_shapes=[pltpu.VMEM((W_DIFF, tn), BF16), pltpu.VMEM((W_NSA, tn), BF16),
                        pltpu.VMEM((W_FOX, tn), BF16)],
        compiler_params=_cparams(2),
        name="merge_branches",
    )(x, o_d, o_n, o_f, w_gate, w_gate, w_gate, b_gate, w_d, w_n, w_f)


def _ln_body(x_ref, y_ref, g_ref, b_ref, o_ref, obf_ref):
    v = DEEPNORM_ALPHA * x_ref[...] + y_ref[...]
    mu = jnp.mean(v, -1, keepdims=True)
    c = v - mu
    var = jnp.mean(c * c, -1, keepdims=True)
    out = c * lax.rsqrt(var + LN_EPS) * g_ref[...] + b_ref[...]
    o_ref[...] = out
    obf_ref[...] = out.astype(BF16)


def residual_layer_norm(x, y, g, b, *, tm=256):
    M, D = x.shape
    return pl.pallas_call(
        _ln_body,
        grid=(pl.cdiv(M, tm),),
        in_specs=[pl.BlockSpec((tm, D), lambda i: (i, 0)),
                  pl.BlockSpec((tm, D), lambda i: (i, 0)),
                  pl.BlockSpec((1, D), lambda i: (0, 0)),
                  pl.BlockSpec((1, D), lambda i: (0, 0))],
        out_specs=[pl.BlockSpec((tm, D), lambda i: (i, 0)),
                   pl.BlockSpec((tm, D), lambda i: (i, 0))],
        out_shape=[jax.ShapeDtypeStruct((M, D), F32), jax.ShapeDtypeStruct((M, D), BF16)],
        compiler_params=_cparams(1),
        name="residual_layer_norm",
    )(x, y, g, b)


def _combine_ln_body(x_ref, *refs):
    y_refs = refs[:TOP_K]
    gate_ref, g_ref, b_ref, o_ref, obf_ref = refs[TOP_K:]
    gate = gate_ref[...]
    f = None
    for k in range(TOP_K):
        term = gate[:, k:k + 1] * y_refs[k][...].astype(F32)
        f = term if f is None else f + term
    v = DEEPNORM_ALPHA * x_ref[...] + f
    mu = jnp.mean(v, -1, keepdims=True)
    c = v - mu
    var = jnp.mean(c * c, -1, keepdims=True)
    out = c * lax.rsqrt(var + LN_EPS) * g_ref[...] + b_ref[...]
    o_ref[...] = out
    obf_ref[...] = out.astype(BF16)


COMBINE_TILE = 256


def combine_layer_norm(x, y_picks, gates, g, b):
    M, D = x.shape
    tm = COMBINE_TILE
    pick_spec = lambda k: pl.BlockSpec((None, tm, D), lambda i: (k, i, 0))
    return pl.pallas_call(
        _combine_ln_body,
        grid=(pl.cdiv(M, tm),),
        in_specs=[pl.BlockSpec((tm, D), lambda i: (i, 0))] + [pick_spec(k) for k in range(TOP_K)] + [
                  pl.BlockSpec((tm, TOP_K), lambda i: (i, 0)),
                  pl.BlockSpec((1, D), lambda i: (0, 0)),
                  pl.BlockSpec((1, D), lambda i: (0, 0))],
        out_specs=[pl.BlockSpec((tm, D), lambda i: (i, 0)),
                   pl.BlockSpec((tm, D), lambda i: (i, 0))],
        out_shape=[jax.ShapeDtypeStruct((M, D), F32), jax.ShapeDtypeStruct((M, D), BF16)],
        compiler_params=_cparams(1),
        name="moe_combine_layer_norm",
    )(x, *([y_picks] * TOP_K), gates, g, b)


def _softmax_rows(s, mask):
    s = jnp.where(mask, s, NEG_BIG)
    m = jnp.max(s, -1, keepdims=True)
    p = jnp.where(mask, jnp.exp(s - m), 0.0)
    d = jnp.sum(p, -1, keepdims=True)
    return p * (1.0 / jnp.where(d > 0, d, 1.0))


def _exp_rows(s, mask):
    s = jnp.where(mask, s, NEG_BIG)
    p = jnp.exp(s - jnp.max(s, -1, keepdims=True))
    return p, 1.0 / jnp.sum(p, -1, keepdims=True)


def _dot_nt(a, b, precision=None):
    return lax.dot_general(a, b, (((1,), (1,)), ((), ())), preferred_element_type=F32, precision=precision)


def _scores(q, k):
    return _dot_nt(q, k) * ATTN_SCALE


def _stack_heads(q, units):
    return jnp.concatenate([q[:, u * LANES:(u + 1) * LANES] for u in units], axis=0)


def _online_step(state, s, mask, v):
    m, l, acc = state
    if mask is not None:
        s = jnp.where(mask, s, NEG_BIG)
    m_new = jnp.maximum(m, jnp.max(s, -1, keepdims=True))
    p = jnp.exp(s - m_new)
    if mask is not None:
        p = jnp.where(mask, p, 0.0)
    alpha = jnp.exp(m - m_new)
    l_new = alpha * l + jnp.sum(p, -1, keepdims=True)
    acc_new = alpha * acc + jnp.dot(p.astype(BF16), v, preferred_element_type=F32)
    return m_new, l_new, acc_new


def _online_finish(state):
    _, l, acc = state
    return acc * (1.0 / jnp.where(l > 0, l, 1.0))


def _load_state(m_ref, l_ref, acc_ref, idx):
    return m_ref[idx][:, :1], l_ref[idx][:, :1], acc_ref[idx]


def _store_state(m_ref, l_ref, acc_ref, idx, state):
    m, l, acc = state
    m_ref[idx] = jnp.broadcast_to(m, m_ref.shape[1:])
    l_ref[idx] = jnp.broadcast_to(l, l_ref.shape[1:])
    acc_ref[idx] = acc


def _init_state(m_ref, l_ref, acc_ref):
    m_ref[...] = jnp.full(m_ref.shape, NEG_BIG, F32)
    l_ref[...] = jnp.zeros(l_ref.shape, F32)
    acc_ref[...] = jnp.zeros(acc_ref.shape, F32)


def _page_unit(page_refs, unit):
    return jnp.concatenate([r[pl.ds(unit, PAGE_SIZE, stride=ROW_UNITS), :] for r in page_refs], axis=0).astype(BF16)


def _token_of_row(shape):
    return lax.broadcasted_iota(jnp.int32, shape, 0) % TOK_PAD


def _new_key_mask(n_rows):
    t = _token_of_row((n_rows, NEW_PAD))
    j = lax.broadcasted_iota(jnp.int32, (n_rows, NEW_PAD), 1)
    return (j <= t) & (j < DEC_SEQ)


def _subln(o, g_ref, post):
    return o * lax.rsqrt(jnp.mean(o * o, -1, keepdims=True) + LN_EPS) * g_ref[...] * post


def _diff_prompt_body(sc_ref, q_ref, k_ref, v_ref, g_ref, o_ref, *, hg, tq):
    qi = pl.program_id(2)
    lam = sc_ref[0]
    post = sc_ref[1]
    q = q_ref[...]

    def attend(T):
        qpos = qi * tq + lax.broadcasted_iota(jnp.int32, (tq, T), 0)
        kpos = lax.broadcasted_iota(jnp.int32, (tq, T), 1)
        mask = (kpos <= qpos)[None]

        def attend_component(c):
            qc = _stack_heads(q, [h * 2 + c for h in range(hg)])
            s = _scores(qc, k_ref[:T, c * LANES:(c + 1) * LANES]).reshape(hg, tq, T)
            p, inv = _exp_rows(s, mask)
            pv = jnp.dot(p.astype(BF16).reshape(hg * tq, T), v_ref[:T, :], preferred_element_type=F32)
            return pv * inv.reshape(hg * tq, 1)

        o = _subln(attend_component(0) - lam * attend_component(1), g_ref, post)
        for h in range(hg):
            o_ref[:, h * 2 * HEAD_DIM:(h + 1) * 2 * HEAD_DIM] = o[h * tq:(h + 1) * tq].astype(o_ref.dtype)

    _for_causal_extent(qi, tq, k_ref.shape[0], attend)


def diff_attention_prompt(q, kv, scalars, subln_g, n_batch, T, *, tq=Q_TILE):
    hg = H_DIFF // KV_DIFF
    d2 = 2 * HEAD_DIM
    nq = T // tq
    kern = functools.partial(_diff_prompt_body, hg=hg, tq=tq)
    return pl.pallas_call(
        kern,
        grid=(n_batch, KV_DIFF, nq),
        in_specs=[pl.BlockSpec(memory_space=pltpu.SMEM),
                  pl.BlockSpec((tq, hg * d2), lambda b, g, i: (b * nq + i, g)),
                  pl.BlockSpec((T, d2), lambda b, g, i: (b, g)),
                  pl.BlockSpec((T, d2), lambda b, g, i: (b, KV_DIFF + g)),
                  pl.BlockSpec((1, d2), lambda b, g, i: (0, 0))],
        out_specs=pl.BlockSpec((tq, hg * d2), lambda b, g, i: (b * nq + i, g)),
        out_shape=jax.ShapeDtypeStruct((n_batch * T, W_DIFF), BF16),
        compiler_params=_cparams(3),
        name="diff_attention_prompt",
    )(scalars, q, kv, kv, subln_g)


def _diff_sample_body(pt_ref, *refs, hg):
    pages = refs[:PAGES_PER_STEP]
    sc_ref, q_ref, new_ref, g_ref, o_ref, m_ref, l_ref, acc_ref = refs[PAGES_PER_STEP:]
    s_id = pl.program_id(1)
    n_rows = hg * TOK_PAD

    @pl.when(s_id == 0)
    def _():
        _init_state(m_ref, l_ref, acc_ref)

    for g in range(KV_DIFF):
        v = jnp.concatenate([_page_unit(pages, 2 * KV_DIFF + g), _page_unit(pages, 3 * KV_DIFF + g)], axis=1)
        for c in range(2):
            idx = g * 2 + c
            s = _scores(q_ref[g, c], _page_unit(pages, c * KV_DIFF + g))
            state = _online_step(_load_state(m_ref, l_ref, acc_ref, idx), s, None, v)
            _store_state(m_ref, l_ref, acc_ref, idx, state)

    @pl.when(s_id == pl.num_programs(1) - 1)
    def _():
        lam = sc_ref[0]
        post = sc_ref[1]
        mask = _new_key_mask(n_rows)
        new = new_ref[...]
        for g in range(KV_DIFF):
            v = new[:, (KV_DIFF + g) * 2 * LANES:(KV_DIFF + g + 1) * 2 * LANES]
            outs = []
            for c in range(2):
                idx = g * 2 + c
                k = new[:, idx * LANES:(idx + 1) * LANES]
                state = _online_step(_load_state(m_ref, l_ref, acc_ref, idx), _scores(q_ref[g, c], k), mask, v)
                outs.append(_online_finish(state))
            o_ref[g] = _subln(outs[0] - lam * outs[1], g_ref, post)


def _cache_rows(cache):
    assert math.prod(cache.shape[3:]) == ROW_UNITS * LANES and cache.shape[2] == PAGE_SIZE
    return cache.reshape(-1, LANES)


def _page_specs(page0, n_pages_per_step):
    def spec(k):
        return pl.BlockSpec((PAGE_SIZE * ROW_UNITS, LANES),
                            lambda b, s, pt, *_: (page0 + pt[b * N_PAGES + s * n_pages_per_step + k], 0))
    return [spec(k) for k in range(n_pages_per_step)]


def diff_attention_sample(pt_flat, scalars, cache, page0, q, new_kv, subln_g):
    Bs = q.shape[0]
    hg = H_DIFF // KV_DIFF
    n_rows = hg * TOK_PAD
    width = new_kv.shape[2]
    d2 = 2 * HEAD_DIM
    grid_spec = pltpu.PrefetchScalarGridSpec(
        num_scalar_prefetch=1,
        grid=(Bs, N_PAGES // PAGES_PER_STEP),
        in_specs=_page_specs(page0, PAGES_PER_STEP) + [
            pl.BlockSpec(memory_space=pltpu.SMEM),
            pl.BlockSpec((None, KV_DIFF, 2, n_rows, HEAD_DIM), lambda b, s, *_: (b, 0, 0, 0, 0)),
            pl.BlockSpec((None, NEW_PAD, width), lambda b, s, *_: (b, 0, 0)),
            pl.BlockSpec((1, d2), lambda b, s, *_: (0, 0))],
        out_specs=pl.BlockSpec((None, KV_DIFF, n_rows, d2), lambda b, s, *_: (b, 0, 0, 0)),
        scratch_shapes=[pltpu.VMEM((KV_DIFF * 2, n_rows, LANES), F32), pltpu.VMEM((KV_DIFF * 2, n_rows, LANES), F32),
                        pltpu.VMEM((KV_DIFF * 2, n_rows, d2), F32)],
    )
    return pl.pallas_call(
        functools.partial(_diff_sample_body, hg=hg),
        grid_spec=grid_spec,
        out_shape=jax.ShapeDtypeStruct((Bs, KV_DIFF, n_rows, d2), F32),
        compiler_params=_cparams(2),
        name="diff_attention_sample",
    )(pt_flat, *([cache] * PAGES_PER_STEP), scalars, q, new_kv, subln_g)


def _fox_prompt_body(q_ref, k_ref, v_ref, cq_ref, ck_ref, o_ref, *, hg, tq):
    qi = pl.program_id(2)
    q = _stack_heads(q_ref[...], range(hg))
    cq = cq_ref[...]

    def attend(T):
        qpos = qi * tq + lax.broadcasted_iota(jnp.int32, (tq, T), 0)
        kpos = lax.broadcasted_iota(jnp.int32, (tq, T), 1)
        mask = (kpos <= qpos)[None]
        bias = jnp.concatenate([cq[:, h:h + 1] - ck_ref[h:h + 1, :T] for h in range(hg)], axis=0)
        s = (_scores(q, k_ref[:T, :]) + bias).reshape(hg, tq, T)
        p, inv = _exp_rows(s, mask)
        o = jnp.dot(p.astype(BF16).reshape(hg * tq, T), v_ref[:T, :], preferred_element_type=F32)
        o = o * inv.reshape(hg * tq, 1)
        for h in range(hg):
            o_ref[:, h * HEAD_DIM:(h + 1) * HEAD_DIM] = o[h * tq:(h + 1) * tq].astype(o_ref.dtype)

    _for_causal_extent(qi, tq, k_ref.shape[0], attend)


def fox_attention_prompt(q, kv, c_col, c_row, n_batch, T, *, tq=Q_TILE):
    hg = H_FOX // KV_FOX
    nq = T // tq
    kern = functools.partial(_fox_prompt_body, hg=hg, tq=tq)
    return pl.pallas_call(
        kern,
        grid=(n_batch, KV_FOX, nq),
        in_specs=[pl.BlockSpec((tq, hg * HEAD_DIM), lambda b, g, i: (b * nq + i, g)),
                  pl.BlockSpec((T, HEAD_DIM), lambda b, g, i: (b, g)),
                  pl.BlockSpec((T, HEAD_DIM), lambda b, g, i: (b, KV_FOX + g)),
                  pl.BlockSpec((None, tq, hg), lambda b, g, i: (g, b * nq + i, 0)),
                  pl.BlockSpec((None, None, hg, T), lambda b, g, i: (b, g, 0, 0))],
        out_specs=pl.BlockSpec((tq, hg * HEAD_DIM), lambda b, g, i: (b * nq + i, g)),
        out_shape=jax.ShapeDtypeStruct((n_batch * T, W_FOX), BF16),
        compiler_params=_cparams(3),
        name="fox_attention_prompt",
    )(q, kv, kv, c_col, c_row)


def _head_rows(c, hg):
    return jnp.concatenate([jnp.broadcast_to(c[h:h + 1, :], (TOK_PAD, c.shape[1])) for h in range(hg)], axis=0)


def _fox_sample_body(pt_ref, *refs, hg):
    pages = refs[:PAGES_PER_STEP]
    q_ref, ckp_ref, cq_ref, ckn_ref, new_ref, o_ref, m_ref, l_ref, acc_ref = refs[PAGES_PER_STEP:]
    s_id = pl.program_id(1)
    n_rows = hg * TOK_PAD

    @pl.when(s_id == 0)
    def _():
        _init_state(m_ref, l_ref, acc_ref)

    for g in range(KV_FOX):
        bias = cq_ref[g] - _head_rows(ckp_ref[g], hg)
        s = _scores(q_ref[g], _page_unit(pages, g)) + bias
        state = _online_step(_load_state(m_ref, l_ref, acc_ref, g), s, None, _page_unit(pages, KV_FOX + g))
        _store_state(m_ref, l_ref, acc_ref, g, state)

    @pl.when(s_id == pl.num_programs(1) - 1)
    def _():
        mask = _new_key_mask(n_rows)
        new = new_ref[...]
        for g in range(KV_FOX):
            bias = cq_ref[g] - _head_rows(ckn_ref[g], hg)
            s = _scores(q_ref[g], new[:, g * LANES:(g + 1) * LANES]) + bias
            v = new[:, (KV_FOX + g) * LANES:(KV_FOX + g + 1) * LANES]
            o_ref[g] = _online_finish(_online_step(_load_state(m_ref, l_ref, acc_ref, g), s, mask, v))


def fox_attention_sample(pt_flat, cache, page0, q, ck_past, cq_col, ck_new, new_kv):
    Bs = q.shape[0]
    hg = H_FOX // KV_FOX
    n_rows = hg * TOK_PAD
    width = new_kv.shape[2]
    keys_per_step = PAGES_PER_STEP * PAGE_SIZE
    grid_spec = pltpu.PrefetchScalarGridSpec(
        num_scalar_prefetch=1,
        grid=(Bs, N_PAGES // PAGES_PER_STEP),
        in_specs=_page_specs(page0, PAGES_PER_STEP) + [
            pl.BlockSpec((None, KV_FOX, n_rows, HEAD_DIM), lambda b, s, *_: (b, 0, 0, 0)),
            pl.BlockSpec((None, KV_FOX, hg, keys_per_step), lambda b, s, *_: (b, 0, 0, s)),
            pl.BlockSpec((None, KV_FOX, n_rows, 1), lambda b, s, *_: (b, 0, 0, 0)),
            pl.BlockSpec((None, KV_FOX, hg, NEW_PAD), lambda b, s, *_: (b, 0, 0, 0)),
            pl.BlockSpec((None, NEW_PAD, width), lambda b, s, *_: (b, 0, 0))],
        out_specs=pl.BlockSpec((None, KV_FOX, n_rows, HEAD_DIM), lambda b, s, *_: (b, 0, 0, 0)),
        scratch_shapes=[pltpu.VMEM((KV_FOX, n_rows, LANES), F32), pltpu.VMEM((KV_FOX, n_rows, LANES), F32),
                        pltpu.VMEM((KV_FOX, n_rows, HEAD_DIM), F32)],
    )
    return pl.pallas_call(
        functools.partial(_fox_sample_body, hg=hg),
        grid_spec=grid_spec,
        out_shape=jax.ShapeDtypeStruct((Bs, KV_FOX, n_rows, HEAD_DIM), F32),
        compiler_params=_cparams(2),
        name="fox_attention_sample",
    )(pt_flat, *([cache] * PAGES_PER_STEP), q, ck_past, cq_col, ck_new, new_kv)


def _gelu_tanh(x):
    return 0.5 * x * (1.0 + jnp.tanh(math.sqrt(2.0 / math.pi) * (x + 0.044715 * x * x * x)))


def _compress_body(*refs, n_src, paged):
    if paged:
        refs = refs[1:]
    srcs = refs[:n_src]
    pe_ref, w1_ref, w2_ref, o_ref, a_ref, b_ref = refs[n_src:]
    s_id = pl.program_id(1)
    n_units = 2 * KV_NSA
    col = pl.multiple_of(s_id * LANES, LANES)

    def chunk_rows(i, u):
        if paged:
            return jnp.concatenate(
                [src[pl.ds(i * ROW_UNITS + u, PAGE_SIZE // CMP_STRIDE, stride=CMP_STRIDE * ROW_UNITS), :]
                 for src in srcs], axis=0)
        return srcs[u][pl.ds(i, LANES, stride=CMP_STRIDE), :]

    for u in range(n_units):
        kv = u // KV_NSA
        for half, dst in ((0, a_ref), (1, b_ref)):
            pieces = [chunk_rows(i, u) + pe_ref[kv, half * CMP_STRIDE + i:half * CMP_STRIDE + i + 1, :]
                      for i in range(CMP_STRIDE)]
            x = jnp.concatenate(pieces, axis=1).astype(BF16)
            dst[u, :, pl.ds(col, LANES)] = _dot_nt(w1_ref[kv, half], x)

    @pl.when(s_id == pl.num_programs(1) - 1)
    def _():
        n_chunks = a_ref.shape[2]
        for u in range(n_units):
            kv = u // KV_NSA
            y = a_ref[u] + pltpu.roll(b_ref[u], n_chunks - 1, axis=1)
            o_ref[u] = jnp.dot(w2_ref[kv], _gelu_tanh(y).astype(BF16), preferred_element_type=F32).astype(BF16)


def _compress_weights(pe, w1, w2):
    half = CMP_STRIDE * HEAD_DIM
    w1t = jnp.stack([jnp.stack([w1[kv, :half].T, w1[kv, half:].T]) for kv in range(2)]).astype(BF16)
    w2t = jnp.transpose(w2, (0, 2, 1)).astype(BF16)
    return pe, w1t, w2t


def compress_prompt(src, pe, w1t, w2t, n_batch, T):
    n_chunks = T // CMP_STRIDE
    assert n_chunks == LANES
    n_units = 2 * KV_NSA
    kern = functools.partial(_compress_body, n_src=n_units, paged=False)
    full3 = lambda b, s: (0, 0, 0)
    unit_spec = lambda u: pl.BlockSpec((T, HEAD_DIM), lambda b, s: (b, u))
    return pl.pallas_call(
        kern,
        grid=(n_batch, 1),
        in_specs=[unit_spec(u) for u in range(n_units)] + [
                  pl.BlockSpec(pe.shape, full3),
                  pl.BlockSpec(w1t.shape, lambda b, s: (0, 0, 0, 0)),
                  pl.BlockSpec(w2t.shape, full3)],
        out_specs=pl.BlockSpec((None, 2 * KV_NSA, HEAD_DIM, n_chunks), lambda b, s: (b, 0, 0, 0)),
        out_shape=jax.ShapeDtypeStruct((n_batch, 2 * KV_NSA, HEAD_DIM, n_chunks), BF16),
        scratch_shapes=[pltpu.VMEM((2 * KV_NSA, HEAD_DIM, n_chunks), F32),
                        pltpu.VMEM((2 * KV_NSA, HEAD_DIM, n_chunks), F32)],
        compiler_params=_cparams(2),
        name="nsa_compress_prompt",
    )(*([src] * n_units), pe, w1t, w2t)


def compress_paged(pt_flat, cache, page0, pe, w1t, w2t, n_batch):
    chunks_per_page = PAGE_SIZE // CMP_STRIDE
    n_chunks = PAST_LEN // CMP_STRIDE
    assert chunks_per_page * CMP_PAGES_PER_STEP == LANES
    kern = functools.partial(_compress_body, n_src=CMP_PAGES_PER_STEP, paged=True)
    grid_spec = pltpu.PrefetchScalarGridSpec(
        num_scalar_prefetch=1,
        grid=(n_batch, N_PAGES // CMP_PAGES_PER_STEP),
        in_specs=_page_specs(page0, CMP_PAGES_PER_STEP) + [
            pl.BlockSpec(pe.shape, lambda b, s, pt: (0, 0, 0)),
            pl.BlockSpec(w1t.shape, lambda b, s, pt: (0, 0, 0, 0)),
            pl.BlockSpec(w2t.shape, lambda b, s, pt: (0, 0, 0))],
        out_specs=pl.BlockSpec((None, 2 * KV_NSA, HEAD_DIM, n_chunks), lambda b, s, pt: (b, 0, 0, 0)),
        scratch_shapes=[pltpu.VMEM((2 * KV_NSA, HEAD_DIM, n_chunks), F32),
                        pltpu.VMEM((2 * KV_NSA, HEAD_DIM, n_chunks), F32)],
    )
    return pl.pallas_call(
        kern,
        grid_spec=grid_spec,
        out_shape=jax.ShapeDtypeStruct((n_batch, 2 * KV_NSA, HEAD_DIM, n_chunks), BF16),
        compiler_params=_cparams(2),
        name="nsa_compress_paged",
    )(pt_flat, *([cache] * CMP_PAGES_PER_STEP), pe, w1t, w2t)


def _block_keys(score_t, cur):
    jj = lax.broadcasted_iota(jnp.int32, score_t.shape, 0)
    valid = jj <= cur
    forced = (jj == 0) | (jj == cur) | (jj == cur - 1)
    return jnp.where(valid, jnp.where(forced, jnp.inf, score_t), -jnp.inf), valid


def _select_blocks(score_t, cur, n_sel):
    key, valid = _block_keys(score_t, cur)
    jj = lax.broadcasted_iota(jnp.int32, score_t.shape, 0)
    rank = jnp.zeros(score_t.shape, F32)
    for i in range(n_sel):
        ki = key[i:i + 1, :]
        beats = (ki > key) | ((ki == key) & (jj > i))
        rank = rank + jnp.where(beats, 1.0, 0.0)
    return jnp.where(valid & (rank < float(min(SEL_TOPK, n_sel))), 1.0, 0.0)


def _cover_matrix(n_sel_rows, n_cmp_cols, n_cmp, n_sel):
    cs = np.arange(n_cmp_cols) * CMP_STRIDE
    ss = np.arange(n_sel_rows) * SEL_BLOCK
    cover = (cs[None, :] < ss[:, None] + SEL_BLOCK) & (cs[None, :] + CMP_BLOCK > ss[:, None])
    cover &= (np.arange(n_cmp_cols)[None, :] < n_cmp) & (np.arange(n_sel_rows)[:, None] < n_sel)
    return jnp.asarray(cover.astype(np.float32))


def _nsa_prompt_body(q_ref, cmp_ref, ks_ref, vs_ref, kw_ref, vw_ref, gate_ref, cover_ref, expand_ref, o_ref,
                     *, hg, tq, n_win, n_cmp):
    qi = pl.program_id(2)
    T = ks_ref.shape[0]
    n_cmp_pad = cmp_ref.shape[2]
    n_sel = cover_ref.shape[0]
    q = _stack_heads(q_ref[...], range(hg))
    q0 = qi * tq

    s = (jnp.dot(q, cmp_ref[0], preferred_element_type=F32) * ATTN_SCALE).reshape(hg, tq, n_cmp_pad)
    qpos_c = q0 + lax.broadcasted_iota(jnp.int32, (tq, n_cmp_pad), 0)
    nn = lax.broadcasted_iota(jnp.int32, (tq, n_cmp_pad), 1)
    p_cmp = _softmax_rows(s, ((nn * CMP_STRIDE + (CMP_BLOCK - 1) <= qpos_c) & (nn < n_cmp))[None])
    o_cmp = _dot_nt(p_cmp.astype(BF16).reshape(hg * tq, n_cmp_pad), cmp_ref[1])

    p_sum = jnp.sum(p_cmp, axis=0)
    score_t = _dot_nt(cover_ref[...], p_sum, precision=lax.Precision.HIGHEST)
    cur = (q0 + lax.broadcasted_iota(jnp.int32, (n_sel, tq), 1)) // SEL_BLOCK
    sel_q = _select_blocks(score_t, cur, n_sel).T.astype(BF16)

    start = pl.multiple_of(jnp.maximum(q0 + tq - n_win, 0), tq)
    kw = kw_ref[pl.ds(start, n_win), :]
    vw = vw_ref[pl.ds(start, n_win), :]
    dlt = (q0 + lax.broadcasted_iota(jnp.int32, (tq, n_win), 0)) - (
        start + lax.broadcasted_iota(jnp.int32, (tq, n_win), 1))
    mask_win = ((dlt >= 0) & (dlt <= WINDOW))[None]
    s = _scores(q, kw).reshape(hg, tq, n_win)
    p_win, inv_win = _exp_rows(s, mask_win)
    o_win = jnp.dot(p_win.astype(BF16).reshape(hg * tq, n_win), vw, preferred_element_type=F32)
    o_win = o_win * inv_win.reshape(hg * tq, 1)

    gate = jax.nn.sigmoid(gate_ref[...])

    def selected_and_store(n_keys):
        sel_keys = jnp.dot(sel_q, expand_ref[:, :n_keys], preferred_element_type=F32)
        qpos = q0 + lax.broadcasted_iota(jnp.int32, (tq, n_keys), 0)
        kpos = lax.broadcasted_iota(jnp.int32, (tq, n_keys), 1)
        mask_slc = ((sel_keys > 0.5) & (kpos <= qpos))[None]
        s_slc = _scores(q, ks_ref[:n_keys, :]).reshape(hg, tq, n_keys)
        p_slc, inv = _exp_rows(s_slc, mask_slc)
        o_slc = jnp.dot(p_slc.astype(BF16).reshape(hg * tq, n_keys), vs_ref[:n_keys, :], preferred_element_type=F32)
        o_slc = o_slc * inv.reshape(hg * tq, 1)
        for h in range(hg):
            rows = slice(h * tq, (h + 1) * tq)
            o = (gate[:, 3 * h:3 * h + 1] * o_cmp[rows] + gate[:, 3 * h + 1:3 * h + 2] * o_slc[rows]
                 + gate[:, 3 * h + 2:3 * h + 3] * o_win[rows])
            o_ref[:, h * HEAD_DIM:(h + 1) * HEAD_DIM] = o.astype(o_ref.dtype)

    _for_causal_extent(qi, tq, T, selected_and_store)


def nsa_attention_prompt(q, kv, cmp_t, gates, n_batch, T, *, tq=Q_TILE):
    hg = H_NSA // KV_NSA
    G = KV_NSA
    nq = T // tq
    n_cmp_pad = cmp_t.shape[3]
    n_cmp = (T - CMP_BLOCK) // CMP_STRIDE + 1
    n_sel = -(-T // SEL_BLOCK)
    cover_t = _cover_matrix(n_sel, n_cmp_pad, n_cmp, n_sel)
    expand = jnp.asarray((np.arange(T)[None, :] // SEL_BLOCK == np.arange(n_sel)[:, None]).astype(np.float32),
                         dtype=BF16)
    n_win = min(WINDOW + tq, T)
    kern = functools.partial(_nsa_prompt_body, hg=hg, tq=tq, n_win=n_win, n_cmp=n_cmp)
    unit = lambda j: (lambda b, g, i: (b, j * G + g))
    return pl.pallas_call(
        kern,
        grid=(n_batch, G, nq),
        in_specs=[pl.BlockSpec((tq, hg * HEAD_DIM), lambda b, g, i: (b * nq + i, g)),
                  pl.BlockSpec((None, 2, None, HEAD_DIM, n_cmp_pad), lambda b, g, i: (b, 0, g, 0, 0)),
                  pl.BlockSpec((T, HEAD_DIM), unit(2)),
                  pl.BlockSpec((T, HEAD_DIM), unit(3)),
                  pl.BlockSpec((T, HEAD_DIM), unit(4)),
                  pl.BlockSpec((T, HEAD_DIM), unit(5)),
                  pl.BlockSpec((None, tq, hg * 3), lambda b, g, i: (g, b * nq + i, 0)),
                  pl.BlockSpec((n_sel, n_cmp_pad), lambda b, g, i: (0, 0)),
                  pl.BlockSpec((n_sel, T), lambda b, g, i: (0, 0))],
        out_specs=pl.BlockSpec((tq, hg * HEAD_DIM), lambda b, g, i: (b * nq + i, g)),
        out_shape=jax.ShapeDtypeStruct((n_batch * T, W_NSA), BF16),
        compiler_params=_cparams(3),
        name="nsa_attention_prompt",
    )(q, cmp_t.reshape(n_batch, 2, G, HEAD_DIM, n_cmp_pad), kv, kv, kv, kv, gates, cover_t, expand)


def _nsa_sample_body(pt_ref, *refs, hg, n_cmp, n_sel):
    pages = refs[:PAGES_PER_STEP]
    (q_ref, cmp_ref, new_ref, win_ref, gate_ref, cover_ref, expand_ref, o_ref,
     sel_ref, key_ref, ocmp_ref, m_ref, l_ref, acc_ref) = refs[PAGES_PER_STEP:]
    s_id = pl.program_id(1)
    G = KV_NSA
    n_rows = hg * TOK_PAD
    n_cmp_pad = cmp_ref.shape[3]
    n_sel_pad = cover_ref.shape[0]
    cur = PAST_LEN // SEL_BLOCK
    blocks_per_step = PAGES_PER_STEP * PAGE_SIZE // SEL_BLOCK

    @pl.when(s_id == 0)
    def _():
        _init_state(m_ref, l_ref, acc_ref)
        nn = lax.broadcasted_iota(jnp.int32, (n_rows, n_cmp_pad), 1)
        for g in range(G):
            s = jnp.dot(q_ref[g], cmp_ref[0, g], preferred_element_type=F32) * ATTN_SCALE
            p = _softmax_rows(s, nn < n_cmp)
            ocmp_ref[g] = _dot_nt(p.astype(BF16), cmp_ref[1, g])
            p_sum = jnp.sum(p.reshape(hg, TOK_PAD, n_cmp_pad), axis=0)
            p_sum = jnp.concatenate([p_sum, jnp.zeros((LANES - TOK_PAD, n_cmp_pad), F32)], axis=0)
            score_t = _dot_nt(cover_ref[...], p_sum, precision=lax.Precision.HIGHEST)
            key, valid = _block_keys(score_t, cur)
            key_ref[...] = key
            jj = lax.broadcasted_iota(jnp.int32, key.shape, 0)

            def rank_step(i, rank):
                ki = key_ref[pl.ds(i, 1), :]
                kk = key_ref[...]
                beats = (ki > kk) | ((ki == kk) & (jj > i))
                return rank + jnp.where(beats, 1.0, 0.0)

            rank = lax.fori_loop(0, n_sel, rank_step, jnp.zeros(key.shape, F32))
            sel_ref[g] = jnp.where(valid & (jj < n_sel) & (rank < float(min(SEL_TOPK, n_sel))), 1.0, 0.0)

    row0 = pl.multiple_of(s_id * blocks_per_step, blocks_per_step)
    for g in range(G):
        sel_blk = sel_ref[g, pl.ds(row0, blocks_per_step), :]
        tok_keys = jnp.dot(sel_blk.T[:TOK_PAD].astype(BF16), expand_ref[...], preferred_element_type=F32)
        mask = jnp.concatenate([tok_keys] * hg, axis=0) > 0.5
        s = _scores(q_ref[g], _page_unit(pages, 2 * G + g))
        state = _online_step(_load_state(m_ref, l_ref, acc_ref, g), s, mask, _page_unit(pages, 3 * G + g))
        _store_state(m_ref, l_ref, acc_ref, g, state)

    @pl.when(s_id == pl.num_programs(1) - 1)
    def _():
        new = new_ref[...]
        mask_new = _new_key_mask(n_rows)
        t_row = _token_of_row((n_rows, WINDOW))
        i_key = lax.broadcasted_iota(jnp.int32, (n_rows, WINDOW), 1)
        mask_win = i_key >= t_row
        unit = lambda j, g: new[:, (j * G + g) * LANES:(j * G + g + 1) * LANES]
        for g in range(G):
            q = q_ref[g]
            o_slc = _online_finish(_online_step(_load_state(m_ref, l_ref, acc_ref, g), _scores(q, unit(2, g)),
                                                mask_new, unit(3, g)))
            kw = win_ref[pl.ds(g, WINDOW, stride=2 * G), :].astype(BF16)
            vw = win_ref[pl.ds(G + g, WINDOW, stride=2 * G), :].astype(BF16)
            init = (jnp.full((n_rows, 1), NEG_BIG, F32), jnp.zeros((n_rows, 1), F32),
                    jnp.zeros((n_rows, HEAD_DIM), F32))
            state = _online_step(init, _scores(q, kw), mask_win, vw)
            o_win = _online_finish(_online_step(state, _scores(q, unit(4, g)), mask_new, unit(5, g)))
            gate = jax.nn.sigmoid(gate_ref[g])
            o_ref[g] = gate[:, 0:1] * ocmp_ref[g] + gate[:, 1:2] * o_slc + gate[:, 2:3] * o_win


def nsa_attention_sample(pt_flat, cache, page0, q, cmp_t, new_kv, win, win0, gates):
    Bs = q.shape[0]
    G = KV_NSA
    hg = H_NSA // KV_NSA
    n_rows = hg * TOK_PAD
    n_k = PAST_LEN + DEC_SEQ
    n_cmp = (n_k - CMP_BLOCK) // CMP_STRIDE + 1
    n_cmp_pad = cmp_t.shape[3]
    n_sel = -(-n_k // SEL_BLOCK)
    blocks_per_step = PAGES_PER_STEP * PAGE_SIZE // SEL_BLOCK
    n_sel_pad = -(-n_sel // blocks_per_step) * blocks_per_step
    cover_t = _cover_matrix(n_sel_pad, n_cmp_pad, n_cmp, n_sel)
    keys_per_step = PAGES_PER_STEP * PAGE_SIZE
    expand = jnp.asarray((np.arange(keys_per_step)[None, :] // SEL_BLOCK
                          == np.arange(blocks_per_step)[:, None]).astype(np.float32), dtype=BF16)
    kern = functools.partial(_nsa_sample_body, hg=hg, n_cmp=n_cmp, n_sel=n_sel)
    grid_spec = pltpu.PrefetchScalarGridSpec(
        num_scalar_prefetch=1,
        grid=(Bs, N_PAGES // PAGES_PER_STEP),
        in_specs=_page_specs(page0, PAGES_PER_STEP) + [
            pl.BlockSpec((None, G, n_rows, HEAD_DIM), lambda b, s, *_: (b, 0, 0, 0)),
            pl.BlockSpec((None, 2, G, HEAD_DIM, n_cmp_pad), lambda b, s, *_: (b, 0, 0, 0, 0)),
            pl.BlockSpec((None, NEW_PAD, new_kv.shape[2]), lambda b, s, *_: (b, 0, 0)),
            pl.BlockSpec((WINDOW * 2 * G, LANES), lambda b, s, *_: (win0 + b, 0)),
            pl.BlockSpec((None, G, n_rows, 3), lambda b, s, *_: (b, 0, 0, 0)),
            pl.BlockSpec((n_sel_pad, n_cmp_pad), lambda b, s, *_: (0, 0)),
            pl.BlockSpec((blocks_per_step, keys_per_step), lambda b, s, *_: (0, 0))],
        out_specs=pl.BlockSpec((None, G, n_rows, HEAD_DIM), lambda b, s, *_: (b, 0, 0, 0)),
        scratch_shapes=[pltpu.VMEM((G, n_sel_pad, LANES), F32), pltpu.VMEM((n_sel_pad, LANES), F32),
                        pltpu.VMEM((G, n_rows, HEAD_DIM), F32),
                        pltpu.VMEM((G, n_rows, LANES), F32), pltpu.VMEM((G, n_rows, LANES), F32),
                        pltpu.VMEM((G, n_rows, HEAD_DIM), F32)],
    )
    return pl.pallas_call(
        kern,
        grid_spec=grid_spec,
        out_shape=jax.ShapeDtypeStruct((Bs, G, n_rows, HEAD_DIM), F32),
        compiler_params=_cparams(2),
        name="nsa_attention_sample",
    )(pt_flat, *([cache] * PAGES_PER_STEP), q, cmp_t.reshape(Bs, 2, G, HEAD_DIM, n_cmp_pad), new_kv, win, gates,
      cover_t, expand)


def _expert_changed(te_ref, i):
    prev = te_ref[jnp.maximum(i - 1, 0)]
    return (i == 0) | (te_ref[i] != prev)


def _moe_up_body(te_ref, nu_ref, x_ref, wg_ref, wu_ref, bg_ref, bu_ref, o_ref, wgb, wub):
    i = pl.program_id(1)

    @pl.when(_expert_changed(te_ref, i))
    def _():
        _cast_weight_tile(wg_ref, wgb)
        _cast_weight_tile(wu_ref, wub)

    @pl.when(i < nu_ref[0])
    def _():
        x = x_ref[...]
        g = jnp.dot(x, wgb[...], preferred_element_type=F32) + bg_ref[...]
        u = jnp.dot(x, wub[...], preferred_element_type=F32) + bu_ref[...]
        g = jnp.minimum(g, SWIGLU_LIMIT)
        u = jnp.clip(u, -SWIGLU_LIMIT, SWIGLU_LIMIT)
        o_ref[...] = ((u + 1.0) * (g * jax.nn.sigmoid(SWIGLU_ALPHA * g))).astype(o_ref.dtype)


def moe_up(tile_expert, n_used, x_sorted, w_gate_up, b_gate_up, layer, *, tn=512):
    R, D = x_sorted.shape
    de = w_gate_up.shape[3] // 2
    tn = min(tn, de)
    nj = de // tn
    tm = MOE_TILE
    grid_spec = pltpu.PrefetchScalarGridSpec(
        num_scalar_prefetch=2,
        grid=(nj, R // tm),
        in_specs=[pl.BlockSpec((tm, D), lambda j, i, te, nu: (i, 0)),
                  pl.BlockSpec((None, None, D, tn), lambda j, i, te, nu: (layer, te[i], 0, j)),
                  pl.BlockSpec((None, None, D, tn), lambda j, i, te, nu: (layer, te[i], 0, j + nj)),
                  pl.BlockSpec((None, None, 1, tn), lambda j, i, te, nu: (layer, te[i], 0, j)),
                  pl.BlockSpec((None, None, 1, tn), lambda j, i, te, nu: (layer, te[i], 0, j + nj))],
        out_specs=pl.BlockSpec((tm, tn), lambda j, i, te, nu: (i, j)),
        scratch_shapes=[pltpu.VMEM((D, tn), BF16), pltpu.VMEM((D, tn), BF16)],
    )
    return pl.pallas_call(
        _moe_up_body,
        grid_spec=grid_spec,
        out_shape=jax.ShapeDtypeStruct((R, de), BF16),
        compiler_params=_cparams(2, V7X_VMEM_LIMIT_LARGE_BYTES),
        name="moe_up",
    )(tile_expert, n_used, x_sorted, w_gate_up, w_gate_up, b_gate_up, b_gate_up)


def _moe_down_body(te_ref, nu_ref, a_ref, w_ref, b_ref, o_ref, wb):
    i = pl.program_id(1)

    @pl.when(_expert_changed(te_ref, i))
    def _():
        _cast_weight_tile(w_ref, wb)

    @pl.when(i < nu_ref[0])
    def _():
        o_ref[...] = (jnp.dot(a_ref[...], wb[...], preferred_element_type=F32) + b_ref[...]).astype(o_ref.dtype)


def moe_down(tile_expert, n_used, act, w_down, b_down, layer, *, tn=1024):
    R, de = act.shape
    D = w_down.shape[3]
    tn = min(tn, D)
    tm = MOE_TILE
    grid_spec = pltpu.PrefetchScalarGridSpec(
        num_scalar_prefetch=2,
        grid=(D // tn, R // tm),
        in_specs=[pl.BlockSpec((tm, de), lambda j, i, te, nu: (i, 0)),
                  pl.BlockSpec((None, None, de, tn), lambda j, i, te, nu: (layer, te[i], 0, j)),
                  pl.BlockSpec((None, None, 1, tn), lambda j, i, te, nu: (layer, te[i], 0, j))],
        out_specs=pl.BlockSpec((tm, tn), lambda j, i, te, nu: (i, j)),
        scratch_shapes=[pltpu.VMEM((de, tn), BF16)],
    )
    return pl.pallas_call(
        _moe_down_body,
        grid_spec=grid_spec,
        out_shape=jax.ShapeDtypeStruct((R, D), BF16),
        compiler_params=_cparams(2),
        name="moe_down",
    )(tile_expert, n_used, act, w_down, b_down)


RANK_TILE = 256


def _rank_body(e_ref, tri_ref, within_ref, counts_ref, carry_ref):
    @pl.when(pl.program_id(0) == 0)
    def _():
        carry_ref[...] = jnp.zeros(carry_ref.shape, F32)

    e = e_ref[0]
    onehot = jnp.where(e == lax.broadcasted_iota(jnp.int32, (N_EXPERTS, RANK_TILE), 0), 1.0, 0.0)
    incl = jnp.dot(onehot.astype(BF16), tri_ref[...], preferred_element_type=F32)
    carry = carry_ref[...]
    pos = jnp.sum(onehot * (incl - 1.0 + carry[:, :1]), axis=0, keepdims=True)
    within_ref[0] = pos.astype(jnp.int32)
    carry_ref[...] = carry + jnp.sum(onehot, axis=1, keepdims=True)
    counts_ref[...] = carry_ref[...]


def expert_ranks(e_flat):
    n = e_flat.shape[0]
    n_blocks = -(-n // RANK_TILE)
    e_pad = jnp.pad(e_flat.astype(jnp.int32), (0, n_blocks * RANK_TILE - n), constant_values=-1)
    tri = jnp.asarray(np.triu(np.ones((RANK_TILE, RANK_TILE), np.float32)), dtype=BF16)
    within, counts = pl.pallas_call(
        _rank_body,
        grid=(n_blocks,),
        in_specs=[pl.BlockSpec((1, 1, RANK_TILE), lambda i: (i, 0, 0)),
                  pl.BlockSpec((RANK_TILE, RANK_TILE), lambda i: (0, 0))],
        out_specs=[pl.BlockSpec((1, 1, RANK_TILE), lambda i: (i, 0, 0)),
                   pl.BlockSpec((N_EXPERTS, LANES), lambda i: (0, 0))],
        out_shape=[jax.ShapeDtypeStruct((n_blocks, 1, RANK_TILE), jnp.int32),
                   jax.ShapeDtypeStruct((N_EXPERTS, LANES), F32)],
        scratch_shapes=[pltpu.VMEM((N_EXPERTS, LANES), F32)],
        compiler_params=_cparams(1),
        name="moe_expert_ranks",
    )(e_pad.reshape(n_blocks, 1, RANK_TILE), tri)
    return within.reshape(-1)[:n], counts[:, 0].astype(jnp.int32)


def moe_ffn(h_bf, w_router, b_router, w_gate_up, b_gate_up, w_down, b_down, layer):
    n_tok, D = h_bf.shape
    E = w_router.shape[2]
    tm = MOE_TILE
    logits = matmul_bias(h_bf, w_router, b_router[layer][None, :], tm=512, tn=E, layer=layer, name="router")
    top_v, top_i = lax.top_k(logits, TOP_K)
    gates = jax.nn.softmax(top_v, axis=-1)
    n_assign = n_tok * TOP_K
    e_flat = top_i.reshape(-1)
    within, counts = expert_ranks(e_flat)
    padded = (counts + tm - 1) // tm * tm
    pad_end = jnp.cumsum(padded)
    dest = (pad_end - padded)[e_flat] + within
    n_tiles = -(-(n_assign + E * (tm - 1)) // tm)
    tok_flat = jnp.arange(n_assign, dtype=jnp.int32) // TOP_K
    row_tok = (jnp.arange(n_tiles * tm, dtype=jnp.int32) % n_tok).at[dest].set(tok_flat)
    tile_expert = jnp.minimum(jnp.searchsorted(pad_end, jnp.arange(n_tiles) * tm, side='right'),
                              E - 1).astype(jnp.int32)
    n_used = (pad_end[-1] // tm).astype(jnp.int32).reshape(1)
    x_sorted = h_bf[row_tok]
    act = moe_up(tile_expert, n_used, x_sorted, w_gate_up, b_gate_up[:, :, None, :], layer)
    y = moe_down(tile_expert, n_used, act, w_down, b_down[:, :, None, :], layer)
    n_pad = -(-n_tok // COMBINE_TILE) * COMBINE_TILE
    dest_k = jnp.pad(dest.reshape(n_tok, TOP_K).T, ((0, 0), (0, n_pad - n_tok)))
    return y[dest_k], gates


def _rope_tables(pos):
    half = HEAD_DIM // 2
    inv = ROPE_THETA ** (-jnp.arange(half, dtype=F32) / half)
    ang = pos.astype(F32)[:, None] * inv[None, :]
    cos, sin = jnp.cos(ang), jnp.sin(ang)
    return jnp.concatenate([cos, cos], -1), jnp.concatenate([-sin, sin], -1)


def _unit_flags(pattern):
    return jnp.asarray(np.repeat(np.asarray(pattern, np.float32), LANES)[None, :])


def _diff_scalars(diff_lambda_l, l):
    lam_init = 0.8 - 0.6 * math.exp(-0.3 * l)
    dl = diff_lambda_l.astype(F32)
    lam = jnp.exp(jnp.sum(dl[0] * dl[1])) - jnp.exp(jnp.sum(dl[2] * dl[3])) + lam_init
    return jnp.stack([lam, jnp.asarray(1.0 - lam_init, F32)]).astype(F32)


def _sample_q_rows(q, n_groups, hg, n_comp):
    Bs, Ts, _ = q.shape
    q = q.reshape(Bs, Ts, n_groups, hg, n_comp, HEAD_DIM)
    q = jnp.pad(q, ((0, 0), (0, TOK_PAD - Ts), (0, 0), (0, 0), (0, 0), (0, 0)))
    q = q.transpose(0, 2, 4, 3, 1, 5).reshape(Bs, n_groups, n_comp, hg * TOK_PAD, HEAD_DIM)
    return q[:, :, 0] if n_comp == 1 else q


def _sample_out_rows(o, hg, Ts):
    Bs, G, _, dv = o.shape
    o = o.reshape(Bs, G, hg, TOK_PAD, dv)[:, :, :, :Ts]
    return o.transpose(0, 3, 1, 2, 4).reshape(Bs * Ts, G * hg * dv)


def _pad_new_rows(a):
    return jnp.pad(a, ((0, 0), (0, NEW_PAD - a.shape[1]), (0, 0)))


def kernel(x_prompt, x_sample, cache_diff_kv, cache_nsa_kv, cache_nsa_win, cache_fox_kv, cache_fox_logf, page_table, w_in, b_in, diff_lambda, diff_subln, nsa_cmp_pe, nsa_cmp_w1, nsa_cmp_w2, w_branch_diff, w_branch_nsa, w_branch_fox, w_out, ln1_g, ln1_b, w_router, b_router, w_gate_up, b_gate_up, w_down, b_down, ln2_g, ln2_b):
    B, T, D = x_prompt.shape
    Bs, Ts, _ = x_sample.shape
    n_p = B * T
    n_s = Bs * Ts
    n_pool = cache_diff_kv.shape[1]
    x = jnp.concatenate([x_prompt.reshape(n_p, D), x_sample.reshape(n_s, D)], axis=0)
    x_bf = x.astype(BF16)
    pos = jnp.concatenate([jnp.tile(jnp.arange(T, dtype=jnp.int32), B),
                           jnp.tile(PAST_LEN + jnp.arange(Ts, dtype=jnp.int32), Bs)])
    cos, sin = _rope_tables(pos)
    pt_flat = page_table.reshape(-1).astype(jnp.int32)
    zero_bias = jnp.zeros((1, D), F32)
    flag_all = lambda n: _unit_flags([1] * n)
    flag_diff_kv = _unit_flags([1] * (KV_DIFF * 2) + [0] * (KV_DIFF * 2))
    flag_nsa_kv = _unit_flags(([1] * KV_NSA + [0] * KV_NSA) * 3)
    hg_d, hg_n, hg_f = H_DIFF // KV_DIFF, H_NSA // KV_NSA, H_FOX // KV_FOX
    o_qd, o_kvd, o_qn, o_kvn, o_gn, o_qf, o_kvf, o_ff, o_gm = IN_OFFS[:9]
    nsa_rows, fox_rows = _cache_rows(cache_nsa_kv), _cache_rows(cache_fox_kv)
    diff_units = cache_diff_kv.reshape(cache_diff_kv.shape[:4] + (KV_DIFF, 2, LANES))
    diff_rows = _cache_rows(jnp.swapaxes(diff_units, 4, 5))
    win_rows = cache_nsa_win.reshape(-1, LANES)
    assert cache_nsa_win.shape[2] == WINDOW

    st_p, st_s = [], []
    for l in range(DEPTH):
        bl = b_in[l][None, :]
        _, qd = project(x_bf, w_in, bl, o_qd, W_DIFF, flag_all(H_DIFF * 2), cos, sin, with_f32=False, layer=l,
                        name="proj_diff_q")
        kvd_f, kvd = project(x_bf, w_in, bl, o_kvd, IN_SIZES[1], flag_diff_kv, cos, sin, with_f32=True, layer=l,
                             name="proj_diff_kv")
        _, qn = project(x_bf, w_in, bl, o_qn, W_NSA, flag_all(H_NSA), cos, sin, with_f32=False, layer=l,
                        name="proj_nsa_q")
        kvn_f, kvn = project(x_bf, w_in, bl, o_kvn, IN_SIZES[3], flag_nsa_kv, cos, sin, with_f32=True, layer=l,
                             name="proj_nsa_kv")
        w_fox = realign_columns(w_in, l, o_qf, o_ff - o_qf, F32)
        b_fox = bl[:, o_qf:o_ff]
        _, qf = project(x_bf, w_fox, b_fox, 0, W_FOX, None, cos, sin, with_f32=False, name="proj_fox_q")
        kvf_f, kvf = project(x_bf, w_fox, b_fox, W_FOX, IN_SIZES[6], None, cos, sin, with_f32=True,
                             name="proj_fox_kv")
        u_gn, u_ff = o_gn // LANES * LANES, o_ff // LANES * LANES
        z_gn, _ = project(x_bf, w_in, bl, u_gn, LANES, None, cos, sin, with_f32=True, layer=l, tn=LANES,
                          name="proj_nsa_gate")
        z_ff, _ = project(x_bf, w_in, bl, u_ff, LANES, None, cos, sin, with_f32=True, layer=l, tn=LANES,
                          name="proj_fox_forget")
        zg_n = z_gn[:, o_gn - u_gn:o_gn - u_gn + IN_SIZES[4]]
        logf = jax.nn.log_sigmoid(z_ff[:, o_ff - u_ff:o_ff - u_ff + IN_SIZES[7]])
        scalars = _diff_scalars(diff_lambda[l], l)
        subln_g = diff_subln[l][None, :]
        pe, w1t, w2t = _compress_weights(nsa_cmp_pe[l], nsa_cmp_w1[l], nsa_cmp_w2[l])

        o_d_p = diff_attention_prompt(qd, kvd, scalars, subln_g, B, T)
        cmp_p = compress_prompt(kvn_f, pe, w1t, w2t, B, T)
        gates_n = zg_n.reshape(-1, KV_NSA, hg_n * 3).transpose(1, 0, 2)
        o_n_p = nsa_attention_prompt(qn, kvn, cmp_p, gates_n, B, T)
        c_p = jnp.cumsum(logf[:n_p].reshape(B, T, H_FOX), axis=1)
        c_col = jnp.pad(c_p.reshape(n_p, KV_FOX, hg_f), ((0, n_s), (0, 0), (0, 0))).transpose(1, 0, 2)
        c_row = c_p.reshape(B, T, KV_FOX, hg_f).transpose(0, 2, 3, 1)
        o_f_p = fox_attention_prompt(qf, kvf, c_col, c_row, B, T)

        rows_s = lambda a: a[n_p:].reshape(Bs, Ts, a.shape[1])
        page0 = l * n_pool
        o = diff_attention_sample(pt_flat, scalars, diff_rows, page0,
                                  _sample_q_rows(rows_s(qd), KV_DIFF, hg_d, 2), _pad_new_rows(rows_s(kvd)), subln_g)
        o_d_s = _sample_out_rows(o, hg_d, Ts)
        cmp_s = compress_paged(pt_flat, nsa_rows, page0, pe, w1t, w2t, Bs)
        gates_s = jnp.pad(rows_s(zg_n).reshape(Bs, Ts, KV_NSA, hg_n, 3), ((0, 0), (0, TOK_PAD - Ts), (0, 0), (0, 0), (0, 0)))
        gates_s = gates_s.transpose(0, 2, 3, 1, 4).reshape(Bs, KV_NSA, hg_n * TOK_PAD, 3)
        o = nsa_attention_sample(pt_flat, nsa_rows, page0,
                                 _sample_q_rows(rows_s(qn), KV_NSA, hg_n, 1), cmp_s, _pad_new_rows(rows_s(kvn)),
                                 win_rows, l * Bs, gates_s)
        o_n_s = _sample_out_rows(o, hg_n, Ts)
        lf_past = cache_fox_logf[l][page_table].astype(F32)
        later = lambda n: jnp.asarray(np.triu(np.ones((n, n), np.float32), 1))
        in_page = jnp.einsum('ji,bpih->bpjh', later(PAGE_SIZE), lf_past, precision=lax.Precision.HIGHEST)
        pages_after = jnp.einsum('qp,bph->bqh', later(N_PAGES), jnp.sum(lf_past, axis=2),
                                 precision=lax.Precision.HIGHEST)
        ck_past = -(in_page + pages_after[:, :, None, :]).reshape(Bs, PAST_LEN, H_FOX)
        ck_past = ck_past.reshape(Bs, PAST_LEN, KV_FOX, hg_f).transpose(0, 2, 3, 1)
        c_new = jnp.cumsum(rows_s(logf), axis=1)
        c_new_g = c_new.reshape(Bs, Ts, KV_FOX, hg_f).transpose(0, 2, 3, 1)
        cq_col = jnp.pad(c_new_g, ((0, 0), (0, 0), (0, 0), (0, TOK_PAD - Ts))).reshape(Bs, KV_FOX, hg_f * TOK_PAD, 1)
        ck_new = jnp.pad(c_new_g, ((0, 0), (0, 0), (0, 0), (0, NEW_PAD - Ts)))
        o = fox_attention_sample(pt_flat, fox_rows, page0,
                                 _sample_q_rows(rows_s(qf), KV_FOX, hg_f, 1), ck_past, cq_col, ck_new,
                                 _pad_new_rows(rows_s(kvf)))
        o_f_s = _sample_out_rows(o, hg_f, Ts)

        new_d, new_n, new_w = kvd_f, kvn_f[:, :4 * KV_NSA * HEAD_DIM], kvn_f[:, 4 * KV_NSA * HEAD_DIM:]
        win_p = new_w[:n_p].reshape(B, T, 2, KV_NSA, HEAD_DIM)[:, T - min(WINDOW, T):]
        win_s = jnp.concatenate([cache_nsa_win[l], new_w[n_p:].reshape(Bs, Ts, 2, KV_NSA, HEAD_DIM)],
                                axis=1)[:, -cache_nsa_win.shape[2]:]
        st_p.append((new_d[:n_p].reshape(B, T, 2, KV_DIFF, 2 * HEAD_DIM), new_n[:n_p].reshape(B, T, 4, KV_NSA, HEAD_DIM),
                     win_p, kvf_f[:n_p].reshape(B, T, 2, KV_FOX, HEAD_DIM), logf[:n_p].reshape(B, T, H_FOX)))
        st_s.append((new_d[n_p:].reshape(Bs, Ts, 2, KV_DIFF, 2 * HEAD_DIM), new_n[n_p:].reshape(Bs, Ts, 4, KV_NSA, HEAD_DIM),
                     win_s, kvf_f[n_p:].reshape(Bs, Ts, 2, KV_FOX, HEAD_DIM), logf[n_p:].reshape(Bs, Ts, H_FOX)))

        o_d = jnp.concatenate([o_d_p, o_d_s.astype(BF16)], axis=0)
        o_n = jnp.concatenate([o_n_p, o_n_s.astype(BF16)], axis=0)
        o_f = jnp.concatenate([o_f_p, o_f_s.astype(BF16)], axis=0)
        w_gate = realign_columns(w_in, l, o_gm, 3 * D, BF16)
        b_gate = bl[:, o_gm:].reshape(3, 1, D)
        merged = merge_branches(x_bf, o_d, o_n, o_f, w_gate, b_gate, w_branch_diff, w_branch_nsa, w_branch_fox, l)
        y = matmul_bias(merged, w_out, zero_bias, tm=512, tn=512, layer=l, name="out_proj")
        h, h_bf = residual_layer_norm(x, y, ln1_g[l][None, :], ln1_b[l][None, :])
        y_rows, gates = moe_ffn(h_bf, w_router, b_router, w_gate_up, b_gate_up, w_down, b_down, l)
        x, x_bf = combine_layer_norm(h, y_rows, gates, ln2_g[l][None, :], ln2_b[l][None, :])
    y_prompt = x[:n_p].reshape(B, T, D)
    y_sample = x[n_p:].reshape(Bs, Ts, D)
    stack = lambda sts, k: jnp.stack([s[k] for s in sts], 0)
    return (y_prompt, y_sample,
            stack(st_p, 0), stack(st_p, 1), stack(st_p, 2), stack(st_p, 3), stack(st_p, 4),
            stack(st_s, 0), stack(st_s, 1), stack(st_s, 2), stack(st_s, 3), stack(st_s, 4))
```

```python
import functools
import math

import numpy as np
import jax
import jax.numpy as jnp
from jax import lax
from jax.experimental import pallas as pl
from jax.experimental.pallas import tpu as pltpu

D_MODEL = 4096
BATCH = 4
SEQ = 2048
DEPTH = 2
DEC_BATCH = 8
DEC_SEQ = 4
PAST_LEN = 16384
PAGE_SIZE = 128

HEAD_DIM = 128
ROPE_THETA = 10000.0
H_DIFF = 4
KV_DIFF = 2
H_NSA = 12
KV_NSA = 2
H_FOX = 12
KV_FOX = 4
W_DIFF = H_DIFF * 2 * HEAD_DIM
W_NSA = H_NSA * HEAD_DIM
W_FOX = H_FOX * HEAD_DIM
CMP_BLOCK = 32
CMP_STRIDE = 16
SEL_BLOCK = 64
SEL_TOPK = 16
WINDOW = 512
N_EXPERTS = 32
TOP_K = 4
D_EXPERT = D_MODEL // 2
SWIGLU_ALPHA = 1.702
SWIGLU_LIMIT = 7.0
LN_EPS = 1e-5
DEEPNORM_ALPHA = (2 * DEPTH) ** 0.25
ATTN_SCALE = HEAD_DIM ** -0.5
IN_SIZES = (
    H_DIFF * 2 * HEAD_DIM,
    2 * KV_DIFF * 2 * HEAD_DIM,
    H_NSA * HEAD_DIM,
    6 * KV_NSA * HEAD_DIM,
    3 * H_NSA,
    H_FOX * HEAD_DIM,
    2 * KV_FOX * HEAD_DIM,
    H_FOX,
    3 * D_MODEL,
)
N_IN = sum(IN_SIZES)
IN_OFFS = tuple(int(v) for v in np.cumsum((0,) + IN_SIZES))
N_PAGES = PAST_LEN // PAGE_SIZE

F32 = jnp.float32
BF16 = jnp.bfloat16
NEG_BIG = -1e30
LANES = 128
SUBLANES = 8

V7X_VMEM_LIMIT_BYTES = 48 * 1024 * 1024
Q_TILE = 128
CAUSAL_STEP = 512
V7X_VMEM_LIMIT_LARGE_BYTES = 56 * 1024 * 1024
MOE_TILE = 512
TOK_PAD = SUBLANES
NEW_PAD = LANES
ROW_UNITS = 8
PAGES_PER_STEP = 16
CMP_PAGES_PER_STEP = 16

assert DEC_SEQ <= TOK_PAD and HEAD_DIM == LANES and PAGE_SIZE % SEL_BLOCK == 0
assert N_PAGES % PAGES_PER_STEP == 0 and N_PAGES % CMP_PAGES_PER_STEP == 0
assert CMP_BLOCK == 2 * CMP_STRIDE and PAGE_SIZE % CMP_STRIDE == 0 and SEQ % CMP_STRIDE == 0
assert ((PAST_LEN + DEC_SEQ - CMP_BLOCK) // CMP_STRIDE) * CMP_STRIDE + CMP_BLOCK <= PAST_LEN
assert PAST_LEN % SEL_BLOCK == 0 and DEC_SEQ <= SEL_BLOCK and WINDOW <= PAST_LEN and SEQ >= WINDOW


def _cparams(n_axes, vmem_limit_bytes=V7X_VMEM_LIMIT_BYTES):
    return pltpu.CompilerParams(dimension_semantics=("arbitrary",) * n_axes,
                                vmem_limit_bytes=vmem_limit_bytes)


def _for_causal_extent(qi, tq, n_keys_total, fn):
    step = min(CAUSAL_STEP, n_keys_total)
    need = (qi * tq + tq + step - 1) // step
    for c in range(1, n_keys_total // step + 1):
        @pl.when(need == c)
        def _():
            fn(c * step)


def _cast_weight_tile(w_ref, wbf_ref, rows_per_chunk=256):
    n_chunks = w_ref.shape[0] // rows_per_chunk

    def body(c, carry):
        r = pl.multiple_of(c * rows_per_chunk, rows_per_chunk)
        wbf_ref[pl.ds(r, rows_per_chunk), :] = w_ref[pl.ds(r, rows_per_chunk), :].astype(BF16)
        return carry

    lax.fori_loop(0, n_chunks, body, 0)


def _mm_body(x_ref, w_ref, b_ref, o_ref, wbf_ref):
    @pl.when(pl.program_id(1) == 0)
    def _():
        _cast_weight_tile(w_ref, wbf_ref)

    acc = jnp.dot(x_ref[...], wbf_ref[...], preferred_element_type=F32)
    o_ref[...] = (acc + b_ref[...]).astype(o_ref.dtype)


def _weight_spec(w, layer, k_rows, tn, col_block):
    if w.ndim == 2:
        return pl.BlockSpec((k_rows, tn), lambda j, i: (0, col_block(j)))
    return pl.BlockSpec((None, k_rows, tn), lambda j, i: (layer, 0, col_block(j)))


def matmul_bias(x, w, b, *, tm, tn, layer=0, out_dtype=F32, name="matmul_bias"):
    M, K = x.shape
    N = w.shape[-1]
    tn = min(tn, N)
    tm = min(tm, M)
    return pl.pallas_call(
        _mm_body,
        grid=(pl.cdiv(N, tn), pl.cdiv(M, tm)),
        in_specs=[pl.BlockSpec((tm, K), lambda j, i: (i, 0)),
                  _weight_spec(w, layer, K, tn, lambda j: j),
                  pl.BlockSpec((1, tn), lambda j, i: (0, j))],
        out_specs=pl.BlockSpec((tm, tn), lambda j, i: (i, j)),
        out_shape=jax.ShapeDtypeStruct((M, N), out_dtype),
        scratch_shapes=[pltpu.VMEM((K, tn), BF16)],
        compiler_params=_cparams(2),
        name=name,
    )(x, w, b)


def _proj_body(x_ref, w_ref, b_ref, flag_ref, cos_ref, sin_ref, *rest, with_f32, use_rope):
    if with_f32:
        of_ref, ob_ref, wbf_ref = rest
    else:
        ob_ref, wbf_ref = rest

    @pl.when(pl.program_id(1) == 0)
    def _():
        _cast_weight_tile(w_ref, wbf_ref)

    acc = jnp.dot(x_ref[...], wbf_ref[...], preferred_element_type=F32) + b_ref[...]
    for u in range(acc.shape[1] // LANES):
        cols = slice(u * LANES, (u + 1) * LANES)
        z = acc[:, cols]
        if use_rope:
            f = flag_ref[:, cols]
            cos = 1.0 + f * (cos_ref[...] - 1.0)
            sin = f * sin_ref[...]
            z = z * cos + pltpu.roll(z, HEAD_DIM // 2, axis=1) * sin
        if with_f32:
            of_ref[:, cols] = z
        ob_ref[:, cols] = z.astype(BF16)


def project(x, w, b, col0, n_cols, rope_flag, cos, sin, *, with_f32, name, layer=0, tm=512, tn=512):
    M, K = x.shape
    assert col0 % tn == 0 and n_cols % tn == 0
    j0 = col0 // tn
    use_rope = rope_flag is not None
    if not use_rope:
        rope_flag = jnp.zeros((1, n_cols), F32)
    kern = functools.partial(_proj_body, with_f32=with_f32, use_rope=use_rope)
    out_block = pl.BlockSpec((tm, tn), lambda j, i: (i, j))
    outs = pl.pallas_call(
        kern,
        grid=(n_cols // tn, pl.cdiv(M, tm)),
        in_specs=[pl.BlockSpec((tm, K), lambda j, i: (i, 0)),
                  _weight_spec(w, layer, K, tn, lambda j: j + j0),
                  pl.BlockSpec((1, tn), lambda j, i: (0, j + j0)),
                  pl.BlockSpec((1, tn), lambda j, i: (0, j)),
                  pl.BlockSpec((tm, LANES), lambda j, i: (i, 0)),
                  pl.BlockSpec((tm, LANES), lambda j, i: (i, 0))],
        out_specs=[out_block, out_block] if with_f32 else [out_block],
        out_shape=([jax.ShapeDtypeStruct((M, n_cols), F32)] if with_f32 else [])
        + [jax.ShapeDtypeStruct((M, n_cols), BF16)],
        scratch_shapes=[pltpu.VMEM((K, tn), BF16)],
        compiler_params=_cparams(2),
        name=name,
    )(x, w, b, rope_flag, cos, sin)
    return (outs[0], outs[1]) if with_f32 else (None, outs[0])


def _realign_body(a_ref, b_ref, o_ref, *, shift):
    w = jnp.concatenate([a_ref[...], b_ref[...]], axis=1)
    width = w.shape[1]
    o_ref[...] = pltpu.roll(w, width - shift, axis=1)[:, :o_ref.shape[1]].astype(o_ref.dtype)


def realign_columns(w, layer, col0, n_cols, out_dtype, *, tr=512, tn=512):
    _, K, _ = w.shape
    tr = min(tr, K)
    aligned = col0 // tn * tn
    shift = col0 - aligned
    assert shift < LANES and n_cols % tn == 0 and K % tr == 0
    ja, jb, units = aligned // tn, aligned // LANES, tn // LANES
    return pl.pallas_call(
        functools.partial(_realign_body, shift=shift),
        grid=(n_cols // tn, K // tr),
        in_specs=[pl.BlockSpec((None, tr, tn), lambda j, r: (layer, r, ja + j)),
                  pl.BlockSpec((None, tr, LANES), lambda j, r: (layer, r, jb + (j + 1) * units))],
        out_specs=pl.BlockSpec((tr, tn), lambda j, r: (r, j)),
        out_shape=jax.ShapeDtypeStruct((K, n_cols), out_dtype),
        compiler_params=_cparams(2),
        name="realign_columns",
    )(w, w)


def _merge_body(x_ref, od_ref, on_ref, of_ref, wg0_ref, wg1_ref, wg2_ref, bg_ref, wd_ref, wn_ref, wf_ref, o_ref,
                wdb, wnb, wfb):
    @pl.when(pl.program_id(1) == 0)
    def _():
        _cast_weight_tile(wd_ref, wdb)
        _cast_weight_tile(wn_ref, wnb)
        _cast_weight_tile(wf_ref, wfb)

    x = x_ref[...]
    acc = None
    for k, (wg_k, o_k, w_k) in enumerate(((wg0_ref, od_ref, wdb), (wg1_ref, on_ref, wnb), (wg2_ref, of_ref, wfb))):
        gate = jax.nn.sigmoid(jnp.dot(x, wg_k[...], preferred_element_type=F32) + bg_ref[k])
        term = gate * jnp.dot(o_k[...], w_k[...], preferred_element_type=F32)
        acc = term if acc is None else acc + term
    o_ref[...] = acc.astype(o_ref.dtype)


def merge_branches(x, o_d, o_n, o_f, w_gate, b_gate, w_d, w_n, w_f, layer, *, tm=256, tn=256):
    M, D = x.shape
    tn = min(tn, D)
    nj = D // tn
    gate_spec = lambda k: pl.BlockSpec((D, tn), lambda j, i: (0, k * nj + j))
    return pl.pallas_call(
        _merge_body,
        grid=(nj, pl.cdiv(M, tm)),
        in_specs=[pl.BlockSpec((tm, D), lambda j, i: (i, 0)),
                  pl.BlockSpec((tm, W_DIFF), lambda j, i: (i, 0)),
                  pl.BlockSpec((tm, W_NSA), lambda j, i: (i, 0)),
                  pl.BlockSpec((tm, W_FOX), lambda j, i: (i, 0)),
                  gate_spec(0), gate_spec(1), gate_spec(2),
                  pl.BlockSpec((3, 1, tn), lambda j, i: (0, 0, j)),
                  _weight_spec(w_d, layer, W_DIFF, tn, lambda j: j),
                  _weight_spec(w_n, layer, W_NSA, tn, lambda j: j),
                  _weight_spec(w_f, layer, W_FOX, tn, lambda j: j)],
        out_specs=pl.BlockSpec((tm, tn), lambda j, i: (i, j)),
        out_shape=jax.ShapeDtypeStruct((M, D), BF16),
        scratch_shapes=[pltpu.VMEM((W_DIFF, tn), BF16), pltpu.VMEM((W_NSA, tn), BF16),
                        pltpu.VMEM((W_FOX, tn), BF16)],
        compiler_params=_cparams(2),
        name="merge_branches",
    )(x, o_d, o_n, o_f, w_gate, w_gate, w_gate, b_gate, w_d, w_n, w_f)


def _ln_body(x_ref, y_ref, g_ref, b_ref, o_ref, obf_ref):
    v = DEEPNORM_ALPHA * x_ref[...] + y_ref[...]
    mu = jnp.mean(v, -1, keepdims=True)
    c = v - mu
    var = jnp.mean(c * c, -1, keepdims=True)
    out = c * lax.rsqrt(var + LN_EPS) * g_ref[...] + b_ref[...]
    o_ref[...] = out
    obf_ref[...] = out.astype(BF16)


def residual_layer_norm(x, y, g, b, *, tm=256):
    M, D = x.shape
    return pl.pallas_call(
        _ln_body,
        grid=(pl.cdiv(M, tm),),
        in_specs=[pl.BlockSpec((tm, D), lambda i: (i, 0)),
                  pl.BlockSpec((tm, D), lambda i: (i, 0)),
                  pl.BlockSpec((1, D), lambda i: (0, 0)),
                  pl.BlockSpec((1, D), lambda i: (0, 0))],
        out_specs=[pl.BlockSpec((tm, D), lambda i: (i, 0)),
                   pl.BlockSpec((tm, D), lambda i: (i, 0))],
        out_shape=[jax.ShapeDtypeStruct((M, D), F32), jax.ShapeDtypeStruct((M, D), BF16)],
        compiler_params=_cparams(1),
        name="residual_layer_norm",
    )(x, y, g, b)


def _combine_ln_body(x_ref, *refs):
    y_refs = refs[:TOP_K]
    gate_ref, g_ref, b_ref, o_ref, obf_ref = refs[TOP_K:]
    gate = gate_ref[...]
    f = None
    for k in range(TOP_K):
        term = gate[:, k:k + 1] * y_refs[k][...].astype(F32)
        f = term if f is None else f + term
    v = DEEPNORM_ALPHA * x_ref[...] + f
    mu = jnp.mean(v, -1, keepdims=True)
    c = v - mu
    var = jnp.mean(c * c, -1, keepdims=True)
    out = c * lax.rsqrt(var + LN_EPS) * g_ref[...] + b_ref[...]
    o_ref[...] = out
    obf_ref[...] = out.astype(BF16)


COMBINE_TILE = 256


def combine_layer_norm(x, y_picks, gates, g, b):
    M, D = x.shape
    tm = COMBINE_TILE
    pick_spec = lambda k: pl.BlockSpec((None, tm, D), lambda i: (k, i, 0))
    return pl.pallas_call(
        _combine_ln_body,
        grid=(pl.cdiv(M, tm),),
        in_specs=[pl.BlockSpec((tm, D), lambda i: (i, 0))] + [pick_spec(k) for k in range(TOP_K)] + [
                  pl.BlockSpec((tm, TOP_K), lambda i: (i, 0)),
                  pl.BlockSpec((1, D), lambda i: (0, 0)),
                  pl.BlockSpec((1, D), lambda i: (0, 0))],
        out_specs=[pl.BlockSpec((tm, D), lambda i: (i, 0)),
                   pl.BlockSpec((tm, D), lambda i: (i, 0))],
        out_shape=[jax.ShapeDtypeStruct((M, D), F32), jax.ShapeDtypeStruct((M, D), BF16)],
        compiler_params=_cparams(1),
        name="moe_combine_layer_norm",
    )(x, *([y_picks] * TOP_K), gates, g, b)


def _softmax_rows(s, mask):
    s = jnp.where(mask, s, NEG_BIG)
    m = jnp.max(s, -1, keepdims=True)
    p = jnp.where(mask, jnp.exp(s - m), 0.0)
    d = jnp.sum(p, -1, keepdims=True)
    return p * (1.0 / jnp.where(d > 0, d, 1.0))


def _exp_rows(s, mask):
    s = jnp.where(mask, s, NEG_BIG)
    p = jnp.exp(s - jnp.max(s, -1, keepdims=True))
    return p, 1.0 / jnp.sum(p, -1, keepdims=True)


def _dot_nt(a, b, precision=None):
    return lax.dot_general(a, b, (((1,), (1,)), ((), ())), preferred_element_type=F32, precision=precision)


def _scores(q, k):
    return _dot_nt(q, k) * ATTN_SCALE


def _stack_heads(q, units):
    return jnp.concatenate([q[:, u * LANES:(u + 1) * LANES] for u in units], axis=0)


def _online_step(state, s, mask, v):
    m, l, acc = state
    if mask is not None:
        s = jnp.where(mask, s, NEG_BIG)
    m_new = jnp.maximum(m, jnp.max(s, -1, keepdims=True))
    p = jnp.exp(s - m_new)
    if mask is not None:
        p = jnp.where(mask, p, 0.0)
    alpha = jnp.exp(m - m_new)
    l_new = alpha * l + jnp.sum(p, -1, keepdims=True)
    acc_new = alpha * acc + jnp.dot(p.astype(BF16), v, preferred_element_type=F32)
    return m_new, l_new, acc_new


def _online_finish(state):
    _, l, acc = state
    return acc * (1.0 / jnp.where(l > 0, l, 1.0))


def _load_state(m_ref, l_ref, acc_ref, idx):
    return m_ref[idx][:, :1], l_ref[idx][:, :1], acc_ref[idx]


def _store_state(m_ref, l_ref, acc_ref, idx, state):
    m, l, acc = state
    m_ref[idx] = jnp.broadcast_to(m, m_ref.shape[1:])
    l_ref[idx] = jnp.broadcast_to(l, l_ref.shape[1:])
    acc_ref[idx] = acc


def _init_state(m_ref, l_ref, acc_ref):
    m_ref[...] = jnp.full(m_ref.shape, NEG_BIG, F32)
    l_ref[...] = jnp.zeros(l_ref.shape, F32)
    acc_ref[...] = jnp.zeros(acc_ref.shape, F32)


def _page_unit(page_refs, unit):
    return jnp.concatenate([r[pl.ds(unit, PAGE_SIZE, stride=ROW_UNITS), :] for r in page_refs], axis=0).astype(BF16)


def _token_of_row(shape):
    return lax.broadcasted_iota(jnp.int32, shape, 0) % TOK_PAD


def _new_key_mask(n_rows):
    t = _token_of_row((n_rows, NEW_PAD))
    j = lax.broadcasted_iota(jnp.int32, (n_rows, NEW_PAD), 1)
    return (j <= t) & (j < DEC_SEQ)


def _subln(o, g_ref, post):
    return o * lax.rsqrt(jnp.mean(o * o, -1, keepdims=True) + LN_EPS) * g_ref[...] * post


def _diff_prompt_body(sc_ref, q_ref, k_ref, v_ref, g_ref, o_ref, *, hg, tq):
    qi = pl.program_id(2)
    lam = sc_ref[0]
    post = sc_ref[1]
    q = q_ref[...]

    def attend(T):
        qpos = qi * tq + lax.broadcasted_iota(jnp.int32, (tq, T), 0)
        kpos = lax.broadcasted_iota(jnp.int32, (tq, T), 1)
        mask = (kpos <= qpos)[None]

        def attend_component(c):
            qc = _stack_heads(q, [h * 2 + c for h in range(hg)])
            s = _scores(qc, k_ref[:T, c * LANES:(c + 1) * LANES]).reshape(hg, tq, T)
            p, inv = _exp_rows(s, mask)
            pv = jnp.dot(p.astype(BF16).reshape(hg * tq, T), v_ref[:T, :], preferred_element_type=F32)
            return pv * inv.reshape(hg * tq, 1)

        o = _subln(attend_component(0) - lam * attend_component(1), g_ref, post)
        for h in range(hg):
            o_ref[:, h * 2 * HEAD_DIM:(h + 1) * 2 * HEAD_DIM] = o[h * tq:(h + 1) * tq].astype(o_ref.dtype)

    _for_causal_extent(qi, tq, k_ref.shape[0], attend)


def diff_attention_prompt(q, kv, scalars, subln_g, n_batch, T, *, tq=Q_TILE):
    hg = H_DIFF // KV_DIFF
    d2 = 2 * HEAD_DIM
    nq = T // tq
    kern = functools.partial(_diff_prompt_body, hg=hg, tq=tq)
    return pl.pallas_call(
        kern,
        grid=(n_batch, KV_DIFF, nq),
        in_specs=[pl.BlockSpec(memory_space=pltpu.SMEM),
                  pl.BlockSpec((tq, hg * d2), lambda b, g, i: (b * nq + i, g)),
                  pl.BlockSpec((T, d2), lambda b, g, i: (b, g)),
                  pl.BlockSpec((T, d2), lambda b, g, i: (b, KV_DIFF + g)),
                  pl.BlockSpec((1, d2), lambda b, g, i: (0, 0))],
        out_specs=pl.BlockSpec((tq, hg * d2), lambda b, g, i: (b * nq + i, g)),
        out_shape=jax.ShapeDtypeStruct((n_batch * T, W_DIFF), BF16),
        compiler_params=_cparams(3),
        name="diff_attention_prompt",
    )(scalars, q, kv, kv, subln_g)


def _diff_sample_body(pt_ref, *refs, hg):
    pages = refs[:PAGES_PER_STEP]
    sc_ref, q_ref, new_ref, g_ref, o_ref, m_ref, l_ref, acc_ref = refs[PAGES_PER_STEP:]
    s_id = pl.program_id(1)
    n_rows = hg * TOK_PAD

    @pl.when(s_id == 0)
    def _():
        _init_state(m_ref, l_ref, acc_ref)

    for g in range(KV_DIFF):
        v = jnp.concatenate([_page_unit(pages, 2 * KV_DIFF + g), _page_unit(pages, 3 * KV_DIFF + g)], axis=1)
        for c in range(2):
            idx = g * 2 + c
            s = _scores(q_ref[g, c], _page_unit(pages, c * KV_DIFF + g))
            state = _online_step(_load_state(m_ref, l_ref, acc_ref, idx), s, None, v)
            _store_state(m_ref, l_ref, acc_ref, idx, state)

    @pl.when(s_id == pl.num_programs(1) - 1)
    def _():
        lam = sc_ref[0]
        post = sc_ref[1]
        mask = _new_key_mask(n_rows)
        new = new_ref[...]
        for g in range(KV_DIFF):
            v = new[:, (KV_DIFF + g) * 2 * LANES:(KV_DIFF + g + 1) * 2 * LANES]
            outs = []
            for c in range(2):
                idx = g * 2 + c
                k = new[:, idx * LANES:(idx + 1) * LANES]
                state = _online_step(_load_state(m_ref, l_ref, acc_ref, idx), _scores(q_ref[g, c], k), mask, v)
                outs.append(_online_finish(state))
            o_ref[g] = _subln(outs[0] - lam * outs[1], g_ref, post)


def _cache_rows(cache):
    assert math.prod(cache.shape[3:]) == ROW_UNITS * LANES and cache.shape[2] == PAGE_SIZE
    return cache.reshape(-1, LANES)


def _page_specs(page0, n_pages_per_step):
    def spec(k):
        return pl.BlockSpec((PAGE_SIZE * ROW_UNITS, LANES),
                            lambda b, s, pt, *_: (page0 + pt[b * N_PAGES + s * n_pages_per_step + k], 0))
    return [spec(k) for k in range(n_pages_per_step)]


def diff_attention_sample(pt_flat, scalars, cache, page0, q, new_kv, subln_g):
    Bs = q.shape[0]
    hg = H_DIFF // KV_DIFF
    n_rows = hg * TOK_PAD
    width = new_kv.shape[2]
    d2 = 2 * HEAD_DIM
    grid_spec = pltpu.PrefetchScalarGridSpec(
        num_scalar_prefetch=1,
        grid=(Bs, N_PAGES // PAGES_PER_STEP),
        in_specs=_page_specs(page0, PAGES_PER_STEP) + [
            pl.BlockSpec(memory_space=pltpu.SMEM),
            pl.BlockSpec((None, KV_DIFF, 2, n_rows, HEAD_DIM), lambda b, s, *_: (b, 0, 0, 0, 0)),
            pl.BlockSpec((None, NEW_PAD, width), lambda b, s, *_: (b, 0, 0)),
            pl.BlockSpec((1, d2), lambda b, s, *_: (0, 0))],
        out_specs=pl.BlockSpec((None, KV_DIFF, n_rows, d2), lambda b, s, *_: (b, 0, 0, 0)),
        scratch_shapes=[pltpu.VMEM((KV_DIFF * 2, n_rows, LANES), F32), pltpu.VMEM((KV_DIFF * 2, n_rows, LANES), F32),
                        pltpu.VMEM((KV_DIFF * 2, n_rows, d2), F32)],
    )
    return pl.pallas_call(
        functools.partial(_diff_sample_body, hg=hg),
        grid_spec=grid_spec,
        out_shape=jax.ShapeDtypeStruct((Bs, KV_DIFF, n_rows, d2), F32),
        compiler_params=_cparams(2),
        name="diff_attention_sample",
    )(pt_flat, *([cache] * PAGES_PER_STEP), scalars, q, new_kv, subln_g)


def _fox_prompt_body(q_ref, k_ref, v_ref, cq_ref, ck_ref, o_ref, *, hg, tq):
    qi = pl.program_id(2)
    q = _stack_heads(q_ref[...], range(hg))
    cq = cq_ref[...]

    def attend(T):
        qpos = qi * tq + lax.broadcasted_iota(jnp.int32, (tq, T), 0)
        kpos = lax.broadcasted_iota(jnp.int32, (tq, T), 1)
        mask = (kpos <= qpos)[None]
        bias = jnp.concatenate([cq[:, h:h + 1] - ck_ref[h:h + 1, :T] for h in range(hg)], axis=0)
        s = (_scores(q, k_ref[:T, :]) + bias).reshape(hg, tq, T)
        p, inv = _exp_rows(s, mask)
        o = jnp.dot(p.astype(BF16).reshape(hg * tq, T), v_ref[:T, :], preferred_element_type=F32)
        o = o * inv.reshape(hg * tq, 1)
        for h in range(hg):
            o_ref[:, h * HEAD_DIM:(h + 1) * HEAD_DIM] = o[h * tq:(h + 1) * tq].astype(o_ref.dtype)

    _for_causal_extent(qi, tq, k_ref.shape[0], attend)


def fox_attention_prompt(q, kv, c_col, c_row, n_batch, T, *, tq=Q_TILE):
    hg = H_FOX // KV_FOX
    nq = T // tq
    kern = functools.partial(_fox_prompt_body, hg=hg, tq=tq)
    return pl.pallas_call(
        kern,
        grid=(n_batch, KV_FOX, nq),
        in_specs=[pl.BlockSpec((tq, hg * HEAD_DIM), lambda b, g, i: (b * nq + i, g)),
                  pl.BlockSpec((T, HEAD_DIM), lambda b, g, i: (b, g)),
                  pl.BlockSpec((T, HEAD_DIM), lambda b, g, i: (b, KV_FOX + g)),
                  pl.BlockSpec((None, tq, hg), lambda b, g, i: (g, b * nq + i, 0)),
                  pl.BlockSpec((None, None, hg, T), lambda b, g, i: (b, g, 0, 0))],
        out_specs=pl.BlockSpec((tq, hg * HEAD_DIM), lambda b, g, i: (b * nq + i, g)),
        out_shape=jax.ShapeDtypeStruct((n_batch * T, W_FOX), BF16),
        compiler_params=_cparams(3),
        name="fox_attention_prompt",
    )(q, kv, kv, c_col, c_row)


def _head_rows(c, hg):
    return jnp.concatenate([jnp.broadcast_to(c[h:h + 1, :], (TOK_PAD, c.shape[1])) for h in range(hg)], axis=0)


def _fox_sample_body(pt_ref, *refs, hg):
    pages = refs[:PAGES_PER_STEP]
    q_ref, ckp_ref, cq_ref, ckn_ref, new_ref, o_ref, m_ref, l_ref, acc_ref = refs[PAGES_PER_STEP:]
    s_id = pl.program_id(1)
    n_rows = hg * TOK_PAD

    @pl.when(s_id == 0)
    def _():
        _init_state(m_ref, l_ref, acc_ref)

    for g in range(KV_FOX):
        bias = cq_ref[g] - _head_rows(ckp_ref[g], hg)
        s = _scores(q_ref[g], _page_unit(pages, g)) + bias
        state = _online_step(_load_state(m_ref, l_ref, acc_ref, g), s, None, _page_unit(pages, KV_FOX + g))
        _store_state(m_ref, l_ref, acc_ref, g, state)

    @pl.when(s_id == pl.num_programs(1) - 1)
    def _():
        mask = _new_key_mask(n_rows)
        new = new_ref[...]
        for g in range(KV_FOX):
            bias = cq_ref[g] - _head_rows(ckn_ref[g], hg)
            s = _scores(q_ref[g], new[:, g * LANES:(g + 1) * LANES]) + bias
            v = new[:, (KV_FOX + g) * LANES:(KV_FOX + g + 1) * LANES]
            o_ref[g] = _online_finish(_online_step(_load_state(m_ref, l_ref, acc_ref, g), s, mask, v))


def fox_attention_sample(pt_flat, cache, page0, q, ck_past, cq_col, ck_new, new_kv):
    Bs = q.shape[0]
    hg = H_FOX // KV_FOX
    n_rows = hg * TOK_PAD
    width = new_kv.shape[2]
    keys_per_step = PAGES_PER_STEP * PAGE_SIZE
    grid_spec = pltpu.PrefetchScalarGridSpec(
        num_scalar_prefetch=1,
        grid=(Bs, N_PAGES // PAGES_PER_STEP),
        in_specs=_page_specs(page0, PAGES_PER_STEP) + [
            pl.BlockSpec((None, KV_FOX, n_rows, HEAD_DIM), lambda b, s, *_: (b, 0, 0, 0)),
            pl.BlockSpec((None, KV_FOX, hg, keys_per_step), lambda b, s, *_: (b, 0, 0, s)),
            pl.BlockSpec((None, KV_FOX, n_rows, 1), lambda b, s, *_: (b, 0, 0, 0)),
            pl.BlockSpec((None, KV_FOX, hg, NEW_PAD), lambda b, s, *_: (b, 0, 0, 0)),
            pl.BlockSpec((None, NEW_PAD, width), lambda b, s, *_: (b, 0, 0))],
        out_specs=pl.BlockSpec((None, KV_FOX, n_rows, HEAD_DIM), lambda b, s, *_: (b, 0, 0, 0)),
        scratch_shapes=[pltpu.VMEM((KV_FOX, n_rows, LANES), F32), pltpu.VMEM((KV_FOX, n_rows, LANES), F32),
                        pltpu.VMEM((KV_FOX, n_rows, HEAD_DIM), F32)],
    )
    return pl.pallas_call(
        functools.partial(_fox_sample_body, hg=hg),
        grid_spec=grid_spec,
        out_shape=jax.ShapeDtypeStruct((Bs, KV_FOX, n_rows, HEAD_DIM), F32),
        compiler_params=_cparams(2),
        name="fox_attention_sample",
    )(pt_flat, *([cache] * PAGES_PER_STEP), q, ck_past, cq_col, ck_new, new_kv)


def _gelu_tanh(x):
    return 0.5 * x * (1.0 + jnp.tanh(math.sqrt(2.0 / math.pi) * (x + 0.044715 * x * x * x)))


def _compress_body(*refs, n_src, paged):
    if paged:
        refs = refs[1:]
    srcs = refs[:n_src]
    pe_ref, w1_ref, w2_ref, o_ref, a_ref, b_ref = refs[n_src:]
    s_id = pl.program_id(1)
    n_units = 2 * KV_NSA
    col = pl.multiple_of(s_id * LANES, LANES)

    def chunk_rows(i, u):
        if paged:
            return jnp.concatenate(
                [src[pl.ds(i * ROW_UNITS + u, PAGE_SIZE // CMP_STRIDE, stride=CMP_STRIDE * ROW_UNITS), :]
                 for src in srcs], axis=0)
        return srcs[u][pl.ds(i, LANES, stride=CMP_STRIDE), :]

    for u in range(n_units):
        kv = u // KV_NSA
        for half, dst in ((0, a_ref), (1, b_ref)):
            pieces = [chunk_rows(i, u) + pe_ref[kv, half * CMP_STRIDE + i:half * CMP_STRIDE + i + 1, :]
                      for i in range(CMP_STRIDE)]
            x = jnp.concatenate(pieces, axis=1).astype(BF16)
            dst[u, :, pl.ds(col, LANES)] = _dot_nt(w1_ref[kv, half], x)

    @pl.when(s_id == pl.num_programs(1) - 1)
    def _():
        n_chunks = a_ref.shape[2]
        for u in range(n_units):
            kv = u // KV_NSA
            y = a_ref[u] + pltpu.roll(b_ref[u], n_chunks - 1, axis=1)
            o_ref[u] = jnp.dot(w2_ref[kv], _gelu_tanh(y).astype(BF16), preferred_element_type=F32).astype(BF16)


def _compress_weights(pe, w1, w2):
    half = CMP_STRIDE * HEAD_DIM
    w1t = jnp.stack([jnp.stack([w1[kv, :half].T, w1[kv, half:].T]) for kv in range(2)]).astype(BF16)
    w2t = jnp.transpose(w2, (0, 2, 1)).astype(BF16)
    return pe, w1t, w2t


def compress_prompt(src, pe, w1t, w2t, n_batch, T):
    n_chunks = T // CMP_STRIDE
    assert n_chunks == LANES
    n_units = 2 * KV_NSA
    kern = functools.partial(_compress_body, n_src=n_units, paged=False)
    full3 = lambda b, s: (0, 0, 0)
    unit_spec = lambda u: pl.BlockSpec((T, HEAD_DIM), lambda b, s: (b, u))
    return pl.pallas_call(
        kern,
        grid=(n_batch, 1),
        in_specs=[unit_spec(u) for u in range(n_units)] + [
                  pl.BlockSpec(pe.shape, full3),
                  pl.BlockSpec(w1t.shape, lambda b, s: (0, 0, 0, 0)),
                  pl.BlockSpec(w2t.shape, full3)],
        out_specs=pl.BlockSpec((None, 2 * KV_NSA, HEAD_DIM, n_chunks), lambda b, s: (b, 0, 0, 0)),
        out_shape=jax.ShapeDtypeStruct((n_batch, 2 * KV_NSA, HEAD_DIM, n_chunks), BF16),
        scratch_shapes=[pltpu.VMEM((2 * KV_NSA, HEAD_DIM, n_chunks), F32),
                        pltpu.VMEM((2 * KV_NSA, HEAD_DIM, n_chunks), F32)],
        compiler_params=_cparams(2),
        name="nsa_compress_prompt",
    )(*([src] * n_units), pe, w1t, w2t)


def compress_paged(pt_flat, cache, page0, pe, w1t, w2t, n_batch):
    chunks_per_page = PAGE_SIZE // CMP_STRIDE
    n_chunks = PAST_LEN // CMP_STRIDE
    assert chunks_per_page * CMP_PAGES_PER_STEP == LANES
    kern = functools.partial(_compress_body, n_src=CMP_PAGES_PER_STEP, paged=True)
    grid_spec = pltpu.PrefetchScalarGridSpec(
        num_scalar_prefetch=1,
        grid=(n_batch, N_PAGES // CMP_PAGES_PER_STEP),
        in_specs=_page_specs(page0, CMP_PAGES_PER_STEP) + [
            pl.BlockSpec(pe.shape, lambda b, s, pt: (0, 0, 0)),
            pl.BlockSpec(w1t.shape, lambda b, s, pt: (0, 0, 0, 0)),
            pl.BlockSpec(w2t.shape, lambda b, s, pt: (0, 0, 0))],
        out_specs=pl.BlockSpec((None, 2 * KV_NSA, HEAD_DIM, n_chunks), lambda b, s, pt: (b, 0, 0, 0)),
        scratch_shapes=[pltpu.VMEM((2 * KV_NSA, HEAD_DIM, n_chunks), F32),
                        pltpu.VMEM((2 * KV_NSA, HEAD_DIM, n_chunks), F32)],
    )
    return pl.pallas_call(
        kern,
        grid_spec=grid_spec,
        out_shape=jax.ShapeDtypeStruct((n_batch, 2 * KV_NSA, HEAD_DIM, n_chunks), BF16),
        compiler_params=_cparams(2),
        name="nsa_compress_paged",
    )(pt_flat, *([cache] * CMP_PAGES_PER_STEP), pe, w1t, w2t)


def _block_keys(score_t, cur):
    jj = lax.broadcasted_iota(jnp.int32, score_t.shape, 0)
    valid = jj <= cur
    forced = (jj == 0) | (jj == cur) | (jj == cur - 1)
    return jnp.where(valid, jnp.where(forced, jnp.inf, score_t), -jnp.inf), valid


def _select_blocks(score_t, cur, n_sel):
    key, valid = _block_keys(score_t, cur)
    jj = lax.broadcasted_iota(jnp.int32, score_t.shape, 0)
    rank = jnp.zeros(score_t.shape, F32)
    for i in range(n_sel):
        ki = key[i:i + 1, :]
        beats = (ki > key) | ((ki == key) & (jj > i))
        rank = rank + jnp.where(beats, 1.0, 0.0)
    return jnp.where(valid & (rank < float(min(SEL_TOPK, n_sel))), 1.0, 0.0)


def _cover_matrix(n_sel_rows, n_cmp_cols, n_cmp, n_sel):
    cs = np.arange(n_cmp_cols) * CMP_STRIDE
    ss = np.arange(n_sel_rows) * SEL_BLOCK
    cover = (cs[None, :] < ss[:, None] + SEL_BLOCK) & (cs[None, :] + CMP_BLOCK > ss[:, None])
    cover &= (np.arange(n_cmp_cols)[None, :] < n_cmp) & (np.arange(n_sel_rows)[:, None] < n_sel)
    return jnp.asarray(cover.astype(np.float32))


def _nsa_prompt_body(q_ref, cmp_ref, ks_ref, vs_ref, kw_ref, vw_ref, gate_ref, cover_ref, expand_ref, o_ref,
                     *, hg, tq, n_win, n_cmp):
    qi = pl.program_id(2)
    T = ks_ref.shape[0]
    n_cmp_pad = cmp_ref.shape[2]
    n_sel = cover_ref.shape[0]
    q = _stack_heads(q_ref[...], range(hg))
    q0 = qi * tq

    s = (jnp.dot(q, cmp_ref[0], preferred_element_type=F32) * ATTN_SCALE).reshape(hg, tq, n_cmp_pad)
    qpos_c = q0 + lax.broadcasted_iota(jnp.int32, (tq, n_cmp_pad), 0)
    nn = lax.broadcasted_iota(jnp.int32, (tq, n_cmp_pad), 1)
    p_cmp = _softmax_rows(s, ((nn * CMP_STRIDE + (CMP_BLOCK - 1) <= qpos_c) & (nn < n_cmp))[None])
    o_cmp = _dot_nt(p_cmp.astype(BF16).reshape(hg * tq, n_cmp_pad), cmp_ref[1])

    p_sum = jnp.sum(p_cmp, axis=0)
    score_t = _dot_nt(cover_ref[...], p_sum, precision=lax.Precision.HIGHEST)
    cur = (q0 + lax.broadcasted_iota(jnp.int32, (n_sel, tq), 1)) // SEL_BLOCK
    sel_q = _select_blocks(score_t, cur, n_sel).T.astype(BF16)

    start = pl.multiple_of(jnp.maximum(q0 + tq - n_win, 0), tq)
    kw = kw_ref[pl.ds(start, n_win), :]
    vw = vw_ref[pl.ds(start, n_win), :]
    dlt = (q0 + lax.broadcasted_iota(jnp.int32, (tq, n_win), 0)) - (
        start + lax.broadcasted_iota(jnp.int32, (tq, n_win), 1))
    mask_win = ((dlt >= 0) & (dlt <= WINDOW))[None]
    s = _scores(q, kw).reshape(hg, tq, n_win)
    p_win, inv_win = _exp_rows(s, mask_win)
    o_win = jnp.dot(p_win.astype(BF16).reshape(hg * tq, n_win), vw, preferred_element_type=F32)
    o_win = o_win * inv_win.reshape(hg * tq, 1)

    gate = jax.nn.sigmoid(gate_ref[...])

    def selected_and_store(n_keys):
        sel_keys = jnp.dot(sel_q, expand_ref[:, :n_keys], preferred_element_type=F32)
        qpos = q0 + lax.broadcasted_iota(jnp.int32, (tq, n_keys), 0)
        kpos = lax.broadcasted_iota(jnp.int32, (tq, n_keys), 1)
        mask_slc = ((sel_keys > 0.5) & (kpos <= qpos))[None]
        s_slc = _scores(q, ks_ref[:n_keys, :]).reshape(hg, tq, n_keys)
        p_slc, inv = _exp_rows(s_slc, mask_slc)
        o_slc = jnp.dot(p_slc.astype(BF16).reshape(hg * tq, n_keys), vs_ref[:n_keys, :], preferred_element_type=F32)
        o_slc = o_slc * inv.reshape(hg * tq, 1)
        for h in range(hg):
            rows = slice(h * tq, (h + 1) * tq)
            o = (gate[:, 3 * h:3 * h + 1] * o_cmp[rows] + gate[:, 3 * h + 1:3 * h + 2] * o_slc[rows]
                 + gate[:, 3 * h + 2:3 * h + 3] * o_win[rows])
            o_ref[:, h * HEAD_DIM:(h + 1) * HEAD_DIM] = o.astype(o_ref.dtype)

    _for_causal_extent(qi, tq, T, selected_and_store)


def nsa_attention_prompt(q, kv, cmp_t, gates, n_batch, T, *, tq=Q_TILE):
    hg = H_NSA // KV_NSA
    G = KV_NSA
    nq = T // tq
    n_cmp_pad = cmp_t.shape[3]
    n_cmp = (T - CMP_BLOCK) // CMP_STRIDE + 1
    n_sel = -(-T // SEL_BLOCK)
    cover_t = _cover_matrix(n_sel, n_cmp_pad, n_cmp, n_sel)
    expand = jnp.asarray((np.arange(T)[None, :] // SEL_BLOCK == np.arange(n_sel)[:, None]).astype(np.float32),
                         dtype=BF16)
    n_win = min(WINDOW + tq, T)
    kern = functools.partial(_nsa_prompt_body, hg=hg, tq=tq, n_win=n_win, n_cmp=n_cmp)
    unit = lambda j: (lambda b, g, i: (b, j * G + g))
    return pl.pallas_call(
        kern,
        grid=(n_batch, G, nq),
        in_specs=[pl.BlockSpec((tq, hg * HEAD_DIM), lambda b, g, i: (b * nq + i, g)),
                  pl.BlockSpec((None, 2, None, HEAD_DIM, n_cmp_pad), lambda b, g, i: (b, 0, g, 0, 0)),
                  pl.BlockSpec((T, HEAD_DIM), unit(2)),
                  pl.BlockSpec((T, HEAD_DIM), unit(3)),
                  pl.BlockSpec((T, HEAD_DIM), unit(4)),
                  pl.BlockSpec((T, HEAD_DIM), unit(5)),
                  pl.BlockSpec((None, tq, hg * 3), lambda b, g, i: (g, b * nq + i, 0)),
                  pl.BlockSpec((n_sel, n_cmp_pad), lambda b, g, i: (0, 0)),
                  pl.BlockSpec((n_sel, T), lambda b, g, i: (0, 0))],
        out_specs=pl.BlockSpec((tq, hg * HEAD_DIM), lambda b, g, i: (b * nq + i, g)),
        out_shape=jax.ShapeDtypeStruct((n_batch * T, W_NSA), BF16),
        compiler_params=_cparams(3),
        name="nsa_attention_prompt",
    )(q, cmp_t.reshape(n_batch, 2, G, HEAD_DIM, n_cmp_pad), kv, kv, kv, kv, gates, cover_t, expand)


def _nsa_sample_body(pt_ref, *refs, hg, n_cmp, n_sel):
    pages = refs[:PAGES_PER_STEP]
    (q_ref, cmp_ref, new_ref, win_ref, gate_ref, cover_ref, expand_ref, o_ref,
     sel_ref, key_ref, ocmp_ref, m_ref, l_ref, acc_ref) = refs[PAGES_PER_STEP:]
    s_id = pl.program_id(1)
    G = KV_NSA
    n_rows = hg * TOK_PAD
    n_cmp_pad = cmp_ref.shape[3]
    n_sel_pad = cover_ref.shape[0]
    cur = PAST_LEN // SEL_BLOCK
    blocks_per_step = PAGES_PER_STEP * PAGE_SIZE // SEL_BLOCK

    @pl.when(s_id == 0)
    def _():
        _init_state(m_ref, l_ref, acc_ref)
        nn = lax.broadcasted_iota(jnp.int32, (n_rows, n_cmp_pad), 1)
        for g in range(G):
            s = jnp.dot(q_ref[g], cmp_ref[0, g], preferred_element_type=F32) * ATTN_SCALE
            p = _softmax_rows(s, nn < n_cmp)
            ocmp_ref[g] = _dot_nt(p.astype(BF16), cmp_ref[1, g])
            p_sum = jnp.sum(p.reshape(hg, TOK_PAD, n_cmp_pad), axis=0)
            p_sum = jnp.concatenate([p_sum, jnp.zeros((LANES - TOK_PAD, n_cmp_pad), F32)], axis=0)
            score_t = _dot_nt(cover_ref[...], p_sum, precision=lax.Precision.HIGHEST)
            key, valid = _block_keys(score_t, cur)
            key_ref[...] = key
            jj = lax.broadcasted_iota(jnp.int32, key.shape, 0)

            def rank_step(i, rank):
                ki = key_ref[pl.ds(i, 1), :]
                kk = key_ref[...]
                beats = (ki > kk) | ((ki == kk) & (jj > i))
                return rank + jnp.where(beats, 1.0, 0.0)

            rank = lax.fori_loop(0, n_sel, rank_step, jnp.zeros(key.shape, F32))
            sel_ref[g] = jnp.where(valid & (jj < n_sel) & (rank < float(min(SEL_TOPK, n_sel))), 1.0, 0.0)

    row0 = pl.multiple_of(s_id * blocks_per_step, blocks_per_step)
    for g in range(G):
        sel_blk = sel_ref[g, pl.ds(row0, blocks_per_step), :]
        tok_keys = jnp.dot(sel_blk.T[:TOK_PAD].astype(BF16), expand_ref[...], preferred_element_type=F32)
        mask = jnp.concatenate([tok_keys] * hg, axis=0) > 0.5
        s = _scores(q_ref[g], _page_unit(pages, 2 * G + g))
        state = _online_step(_load_state(m_ref, l_ref, acc_ref, g), s, mask, _page_unit(pages, 3 * G + g))
        _store_state(m_ref, l_ref, acc_ref, g, state)

    @pl.when(s_id == pl.num_programs(1) - 1)
    def _():
        new = new_ref[...]
        mask_new = _new_key_mask(n_rows)
        t_row = _token_of_row((n_rows, WINDOW))
        i_key = lax.broadcasted_iota(jnp.int32, (n_rows, WINDOW), 1)
        mask_win = i_key >= t_row
        unit = lambda j, g: new[:, (j * G + g) * LANES:(j * G + g + 1) * LANES]
        for g in range(G):
            q = q_ref[g]
            o_slc = _online_finish(_online_step(_load_state(m_ref, l_ref, acc_ref, g), _scores(q, unit(2, g)),
                                                mask_new, unit(3, g)))
            kw = win_ref[pl.ds(g, WINDOW, stride=2 * G), :].astype(BF16)
            vw = win_ref[pl.ds(G + g, WINDOW, stride=2 * G), :].astype(BF16)
            init = (jnp.full((n_rows, 1), NEG_BIG, F32), jnp.zeros((n_rows, 1), F32),
                    jnp.zeros((n_rows, HEAD_DIM), F32))
            state = _online_step(init, _scores(q, kw), mask_win, vw)
            o_win = _online_finish(_online_step(state, _scores(q, unit(4, g)), mask_new, unit(5, g)))
            gate = jax.nn.sigmoid(gate_ref[g])
            o_ref[g] = gate[:, 0:1] * ocmp_ref[g] + gate[:, 1:2] * o_slc + gate[:, 2:3] * o_win


def nsa_attention_sample(pt_flat, cache, page0, q, cmp_t, new_kv, win, win0, gates):
    Bs = q.shape[0]
    G = KV_NSA
    hg = H_NSA // KV_NSA
    n_rows = hg * TOK_PAD
    n_k = PAST_LEN + DEC_SEQ
    n_cmp = (n_k - CMP_BLOCK) // CMP_STRIDE + 1
    n_cmp_pad = cmp_t.shape[3]
    n_sel = -(-n_k // SEL_BLOCK)
    blocks_per_step = PAGES_PER_STEP * PAGE_SIZE // SEL_BLOCK
    n_sel_pad = -(-n_sel // blocks_per_step) * blocks_per_step
    cover_t = _cover_matrix(n_sel_pad, n_cmp_pad, n_cmp, n_sel)
    keys_per_step = PAGES_PER_STEP * PAGE_SIZE
    expand = jnp.asarray((np.arange(keys_per_step)[None, :] // SEL_BLOCK
                          == np.arange(blocks_per_step)[:, None]).astype(np.float32), dtype=BF16)
    kern = functools.partial(_nsa_sample_body, hg=hg, n_cmp=n_cmp, n_sel=n_sel)
    grid_spec = pltpu.PrefetchScalarGridSpec(
        num_scalar_prefetch=1,
        grid=(Bs, N_PAGES // PAGES_PER_STEP),
        in_specs=_page_specs(page0, PAGES_PER_STEP) + [
            pl.BlockSpec((None, G, n_rows, HEAD_DIM), lambda b, s, *_: (b, 0, 0, 0)),
            pl.BlockSpec((None, 2, G, HEAD_DIM, n_cmp_pad), lambda b, s, *_: (b, 0, 0, 0, 0)),
            pl.BlockSpec((None, NEW_PAD, new_kv.shape[2]), lambda b, s, *_: (b, 0, 0)),
            pl.BlockSpec((WINDOW * 2 * G, LANES), lambda b, s, *_: (win0 + b, 0)),
            pl.BlockSpec((None, G, n_rows, 3), lambda b, s, *_: (b, 0, 0, 0)),
            pl.BlockSpec((n_sel_pad, n_cmp_pad), lambda b, s, *_: (0, 0)),
            pl.BlockSpec((blocks_per_step, keys_per_step), lambda b, s, *_: (0, 0))],
        out_specs=pl.BlockSpec((None, G, n_rows, HEAD_DIM), lambda b, s, *_: (b, 0, 0, 0)),
        scratch_shapes=[pltpu.VMEM((G, n_sel_pad, LANES), F32), pltpu.VMEM((n_sel_pad, LANES), F32),
                        pltpu.VMEM((G, n_rows, HEAD_DIM), F32),
                        pltpu.VMEM((G, n_rows, LANES), F32), pltpu.VMEM((G, n_rows, LANES), F32),
                        pltpu.VMEM((G, n_rows, HEAD_DIM), F32)],
    )
    return pl.pallas_call(
        kern,
        grid_spec=grid_spec,
        out_shape=jax.ShapeDtypeStruct((Bs, G, n_rows, HEAD_DIM), F32),
        compiler_params=_cparams(2),
        name="nsa_attention_sample",
    )(pt_flat, *([cache] * PAGES_PER_STEP), q, cmp_t.reshape(Bs, 2, G, HEAD_DIM, n_cmp_pad), new_kv, win, gates,
      cover_t, expand)


def _expert_changed(te_ref, i):
    prev = te_ref[jnp.maximum(i - 1, 0)]
    return (i == 0) | (te_ref[i] != prev)


def _moe_up_body(te_ref, nu_ref, x_ref, wg_ref, wu_ref, bg_ref, bu_ref, o_ref, wgb, wub):
    i = pl.program_id(1)

    @pl.when(_expert_changed(te_ref, i))
    def _():
        _cast_weight_tile(wg_ref, wgb)
        _cast_weight_tile(wu_ref, wub)

    @pl.when(i < nu_ref[0])
    def _():
        x = x_ref[...]
        g = jnp.dot(x, wgb[...], preferred_element_type=F32) + bg_ref[...]
        u = jnp.dot(x, wub[...], preferred_element_type=F32) + bu_ref[...]
        g = jnp.minimum(g, SWIGLU_LIMIT)
        u = jnp.clip(u, -SWIGLU_LIMIT, SWIGLU_LIMIT)
        o_ref[...] = ((u + 1.0) * (g * jax.nn.sigmoid(SWIGLU_ALPHA * g))).astype(o_ref.dtype)


def moe_up(tile_expert, n_used, x_sorted, w_gate_up, b_gate_up, layer, *, tn=512):
    R, D = x_sorted.shape
    de = w_gate_up.shape[3] // 2
    tn = min(tn, de)
    nj = de // tn
    tm = MOE_TILE
    grid_spec = pltpu.PrefetchScalarGridSpec(
        num_scalar_prefetch=2,
        grid=(nj, R // tm),
        in_specs=[pl.BlockSpec((tm, D), lambda j, i, te, nu: (i, 0)),
                  pl.BlockSpec((None, None, D, tn), lambda j, i, te, nu: (layer, te[i], 0, j)),
                  pl.BlockSpec((None, None, D, tn), lambda j, i, te, nu: (layer, te[i], 0, j + nj)),
                  pl.BlockSpec((None, None, 1, tn), lambda j, i, te, nu: (layer, te[i], 0, j)),
                  pl.BlockSpec((None, None, 1, tn), lambda j, i, te, nu: (layer, te[i], 0, j + nj))],
        out_specs=pl.BlockSpec((tm, tn), lambda j, i, te, nu: (i, j)),
        scratch_shapes=[pltpu.VMEM((D, tn), BF16), pltpu.VMEM((D, tn), BF16)],
    )
    return pl.pallas_call(
        _moe_up_body,
        grid_spec=grid_spec,
        out_shape=jax.ShapeDtypeStruct((R, de), BF16),
        compiler_params=_cparams(2, V7X_VMEM_LIMIT_LARGE_BYTES),
        name="moe_up",
    )(tile_expert, n_used, x_sorted, w_gate_up, w_gate_up, b_gate_up, b_gate_up)


def _moe_down_body(te_ref, nu_ref, a_ref, w_ref, b_ref, o_ref, wb):
    i = pl.program_id(1)

    @pl.when(_expert_changed(te_ref, i))
    def _():
        _cast_weight_tile(w_ref, wb)

    @pl.when(i < nu_ref[0])
    def _():
        o_ref[...] = (jnp.dot(a_ref[...], wb[...], preferred_element_type=F32) + b_ref[...]).astype(o_ref.dtype)


def moe_down(tile_expert, n_used, act, w_down, b_down, layer, *, tn=1024):
    R, de = act.shape
    D = w_down.shape[3]
    tn = min(tn, D)
    tm = MOE_TILE
    grid_spec = pltpu.PrefetchScalarGridSpec(
        num_scalar_prefetch=2,
        grid=(D // tn, R // tm),
        in_specs=[pl.BlockSpec((tm, de), lambda j, i, te, nu: (i, 0)),
                  pl.BlockSpec((None, None, de, tn), lambda j, i, te, nu: (layer, te[i], 0, j)),
                  pl.BlockSpec((None, None, 1, tn), lambda j, i, te, nu: (layer, te[i], 0, j))],
        out_specs=pl.BlockSpec((tm, tn), lambda j, i, te, nu: (i, j)),
        scratch_shapes=[pltpu.VMEM((de, tn), BF16)],
    )
    return pl.pallas_call(
        _moe_down_body,
        grid_spec=grid_spec,
        out_shape=jax.ShapeDtypeStruct((R, D), BF16),
        compiler_params=_cparams(2),
        name="moe_down",
    )(tile_expert, n_used, act, w_down, b_down)


RANK_TILE = 256


def _rank_body(e_ref, tri_ref, within_ref, counts_ref, carry_ref):
    @pl.when(pl.program_id(0) == 0)
    def _():
        carry_ref[...] = jnp.zeros(carry_ref.shape, F32)

    e = e_ref[0]
    onehot = jnp.where(e == lax.broadcasted_iota(jnp.int32, (N_EXPERTS, RANK_TILE), 0), 1.0, 0.0)
    incl = jnp.dot(onehot.astype(BF16), tri_ref[...], preferred_element_type=F32)
    carry = carry_ref[...]
    pos = jnp.sum(onehot * (incl - 1.0 + carry[:, :1]), axis=0, keepdims=True)
    within_ref[0] = pos.astype(jnp.int32)
    carry_ref[...] = carry + jnp.sum(onehot, axis=1, keepdims=True)
    counts_ref[...] = carry_ref[...]


def expert_ranks(e_flat):
    n = e_flat.shape[0]
    n_blocks = -(-n // RANK_TILE)
    e_pad = jnp.pad(e_flat.astype(jnp.int32), (0, n_blocks * RANK_TILE - n), constant_values=-1)
    tri = jnp.asarray(np.triu(np.ones((RANK_TILE, RANK_TILE), np.float32)), dtype=BF16)
    within, counts = pl.pallas_call(
        _rank_body,
        grid=(n_blocks,),
        in_specs=[pl.BlockSpec((1, 1, RANK_TILE), lambda i: (i, 0, 0)),
                  pl.BlockSpec((RANK_TILE, RANK_TILE), lambda i: (0, 0))],
        out_specs=[pl.BlockSpec((1, 1, RANK_TILE), lambda i: (i, 0, 0)),
                   pl.BlockSpec((N_EXPERTS, LANES), lambda i: (0, 0))],
        out_shape=[jax.ShapeDtypeStruct((n_blocks, 1, RANK_TILE), jnp.int32),
                   jax.ShapeDtypeStruct((N_EXPERTS, LANES), F32)],
        scratch_shapes=[pltpu.VMEM((N_EXPERTS, LANES), F32)],
        compiler_params=_cparams(1),
        name="moe_expert_ranks",
    )(e_pad.reshape(n_blocks, 1, RANK_TILE), tri)
    return within.reshape(-1)[:n], counts[:, 0].astype(jnp.int32)


def moe_ffn(h_bf, w_router, b_router, w_gate_up, b_gate_up, w_down, b_down, layer):
    n_tok, D = h_bf.shape
    E = w_router.shape[2]
    tm = MOE_TILE
    logits = matmul_bias(h_bf, w_router, b_router[layer][None, :], tm=512, tn=E, layer=layer, name="router")
    top_v, top_i = lax.top_k(logits, TOP_K)
    gates = jax.nn.softmax(top_v, axis=-1)
    n_assign = n_tok * TOP_K
    e_flat = top_i.reshape(-1)
    within, counts = expert_ranks(e_flat)
    padded = (counts + tm - 1) // tm * tm
    pad_end = jnp.cumsum(padded)
    dest = (pad_end - padded)[e_flat] + within
    n_tiles = -(-(n_assign + E * (tm - 1)) // tm)
    tok_flat = jnp.arange(n_assign, dtype=jnp.int32) // TOP_K
    row_tok = (jnp.arange(n_tiles * tm, dtype=jnp.int32) % n_tok).at[dest].set(tok_flat)
    tile_expert = jnp.minimum(jnp.searchsorted(pad_end, jnp.arange(n_tiles) * tm, side='right'),
                              E - 1).astype(jnp.int32)
    n_used = (pad_end[-1] // tm).astype(jnp.int32).reshape(1)
    x_sorted = h_bf[row_tok]
    act = moe_up(tile_expert, n_used, x_sorted, w_gate_up, b_gate_up[:, :, None, :], layer)
    y = moe_down(tile_expert, n_used, act, w_down, b_down[:, :, None, :], layer)
    n_pad = -(-n_tok // COMBINE_TILE) * COMBINE_TILE
    dest_k = jnp.pad(dest.reshape(n_tok, TOP_K).T, ((0, 0), (0, n_pad - n_tok)))
    return y[dest_k], gates


def _rope_tables(pos):
    half = HEAD_DIM // 2
    inv = ROPE_THETA ** (-jnp.arange(half, dtype=F32) / half)
    ang = pos.astype(F32)[:, None] * inv[None, :]
    cos, sin = jnp.cos(ang), jnp.sin(ang)
    return jnp.concatenate([cos, cos], -1), jnp.concatenate([-sin, sin], -1)


def _unit_flags(pattern):
    return jnp.asarray(np.repeat(np.asarray(pattern, np.float32), LANES)[None, :])


def _diff_scalars(diff_lambda_l, l):
    lam_init = 0.8 - 0.6 * math.exp(-0.3 * l)
    dl = diff_lambda_l.astype(F32)
    lam = jnp.exp(jnp.sum(dl[0] * dl[1])) - jnp.exp(jnp.sum(dl[2] * dl[3])) + lam_init
    return jnp.stack([lam, jnp.asarray(1.0 - lam_init, F32)]).astype(F32)


def _sample_q_rows(q, n_groups, hg, n_comp):
    Bs, Ts, _ = q.shape
    q = q.reshape(Bs, Ts, n_groups, hg, n_comp, HEAD_DIM)
    q = jnp.pad(q, ((0, 0), (0, TOK_PAD - Ts), (0, 0), (0, 0), (0, 0), (0, 0)))
    q = q.transpose(0, 2, 4, 3, 1, 5).reshape(Bs, n_groups, n_comp, hg * TOK_PAD, HEAD_DIM)
    return q[:, :, 0] if n_comp == 1 else q


def _sample_out_rows(o, hg, Ts):
    Bs, G, _, dv = o.shape
    o = o.reshape(Bs, G, hg, TOK_PAD, dv)[:, :, :, :Ts]
    return o.transpose(0, 3, 1, 2, 4).reshape(Bs * Ts, G * hg * dv)


def _pad_new_rows(a):
    return jnp.pad(a, ((0, 0), (0, NEW_PAD - a.shape[1]), (0, 0)))


def kernel(x_prompt, x_sample, cache_diff_kv, cache_nsa_kv, cache_nsa_win, cache_fox_kv, cache_fox_logf, page_table, w_in, b_in, diff_lambda, diff_subln, nsa_cmp_pe, nsa_cmp_w1, nsa_cmp_w2, w_branch_diff, w_branch_nsa, w_branch_fox, w_out, ln1_g, ln1_b, w_router, b_router, w_gate_up, b_gate_up, w_down, b_down, ln2_g, ln2_b):
    B, T, D = x_prompt.shape
    Bs, Ts, _ = x_sample.shape
    n_p = B * T
    n_s = Bs * Ts
    n_pool = cache_diff_kv.shape[1]
    x = jnp.concatenate([x_prompt.reshape(n_p, D), x_sample.reshape(n_s, D)], axis=0)
    x_bf = x.astype(BF16)
    pos = jnp.concatenate([jnp.tile(jnp.arange(T, dtype=jnp.int32), B),
                           jnp.tile(PAST_LEN + jnp.arange(Ts, dtype=jnp.int32), Bs)])
    cos, sin = _rope_tables(pos)
    pt_flat = page_table.reshape(-1).astype(jnp.int32)
    zero_bias = jnp.zeros((1, D), F32)
    flag_all = lambda n: _unit_flags([1] * n)
    flag_diff_kv = _unit_flags([1] * (KV_DIFF * 2) + [0] * (KV_DIFF * 2))
    flag_nsa_kv = _unit_flags(([1] * KV_NSA + [0] * KV_NSA) * 3)
    hg_d, hg_n, hg_f = H_DIFF // KV_DIFF, H_NSA // KV_NSA, H_FOX // KV_FOX
    o_qd, o_kvd, o_qn, o_kvn, o_gn, o_qf, o_kvf, o_ff, o_gm = IN_OFFS[:9]
    nsa_rows, fox_rows = _cache_rows(cache_nsa_kv), _cache_rows(cache_fox_kv)
    diff_units = cache_diff_kv.reshape(cache_diff_kv.shape[:4] + (KV_DIFF, 2, LANES))
    diff_rows = _cache_rows(jnp.swapaxes(diff_units, 4, 5))
    win_rows = cache_nsa_win.reshape(-1, LANES)
    assert cache_nsa_win.shape[2] == WINDOW

    st_p, st_s = [], []
    for l in range(DEPTH):
        bl = b_in[l][None, :]
        _, qd = project(x_bf, w_in, bl, o_qd, W_DIFF, flag_all(H_DIFF * 2), cos, sin, with_f32=False, layer=l,
                        name="proj_diff_q")
        kvd_f, kvd = project(x_bf, w_in, bl, o_kvd, IN_SIZES[1], flag_diff_kv, cos, sin, with_f32=True, layer=l,
                             name="proj_diff_kv")
        _, qn = project(x_bf, w_in, bl, o_qn, W_NSA, flag_all(H_NSA), cos, sin, with_f32=False, layer=l,
                        name="proj_nsa_q")
        kvn_f, kvn = project(x_bf, w_in, bl, o_kvn, IN_SIZES[3], flag_nsa_kv, cos, sin, with_f32=True, layer=l,
                             name="proj_nsa_kv")
        w_fox = realign_columns(w_in, l, o_qf, o_ff - o_qf, F32)
        b_fox = bl[:, o_qf:o_ff]
        _, qf = project(x_bf, w_fox, b_fox, 0, W_FOX, None, cos, sin, with_f32=False, name="proj_fox_q")
        kvf_f, kvf = project(x_bf, w_fox, b_fox, W_FOX, IN_SIZES[6], None, cos, sin, with_f32=True,
                             name="proj_fox_kv")
        u_gn, u_ff = o_gn // LANES * LANES, o_ff // LANES * LANES
        z_gn, _ = project(x_bf, w_in, bl, u_gn, LANES, None, cos, sin, with_f32=True, layer=l, tn=LANES,
                          name="proj_nsa_gate")
        z_ff, _ = project(x_bf, w_in, bl, u_ff, LANES, None, cos, sin, with_f32=True, layer=l, tn=LANES,
                          name="proj_fox_forget")
        zg_n = z_gn[:, o_gn - u_gn:o_gn - u_gn + IN_SIZES[4]]
        logf = jax.nn.log_sigmoid(z_ff[:, o_ff - u_ff:o_ff - u_ff + IN_SIZES[7]])
        scalars = _diff_scalars(diff_lambda[l], l)
        subln_g = diff_subln[l][None, :]
        pe, w1t, w2t = _compress_weights(nsa_cmp_pe[l], nsa_cmp_w1[l], nsa_cmp_w2[l])

        o_d_p = diff_attention_prompt(qd, kvd, scalars, subln_g, B, T)
        cmp_p = compress_prompt(kvn_f, pe, w1t, w2t, B, T)
        gates_n = zg_n.reshape(-1, KV_NSA, hg_n * 3).transpose(1, 0, 2)
        o_n_p = nsa_attention_prompt(qn, kvn, cmp_p, gates_n, B, T)
        c_p = jnp.cumsum(logf[:n_p].reshape(B, T, H_FOX), axis=1)
        c_col = jnp.pad(c_p.reshape(n_p, KV_FOX, hg_f), ((0, n_s), (0, 0), (0, 0))).transpose(1, 0, 2)
        c_row = c_p.reshape(B, T, KV_FOX, hg_f).transpose(0, 2, 3, 1)
        o_f_p = fox_attention_prompt(qf, kvf, c_col, c_row, B, T)

        rows_s = lambda a: a[n_p:].reshape(Bs, Ts, a.shape[1])
        page0 = l * n_pool
        o = diff_attention_sample(pt_flat, scalars, diff_rows, page0,
                                  _sample_q_rows(rows_s(qd), KV_DIFF, hg_d, 2), _pad_new_rows(rows_s(kvd)), subln_g)
        o_d_s = _sample_out_rows(o, hg_d, Ts)
        cmp_s = compress_paged(pt_flat, nsa_rows, page0, pe, w1t, w2t, Bs)
        gates_s = jnp.pad(rows_s(zg_n).reshape(Bs, Ts, KV_NSA, hg_n, 3), ((0, 0), (0, TOK_PAD - Ts), (0, 0), (0, 0), (0, 0)))
        gates_s = gates_s.transpose(0, 2, 3, 1, 4).reshape(Bs, KV_NSA, hg_n * TOK_PAD, 3)
        o = nsa_attention_sample(pt_flat, nsa_rows, page0,
                                 _sample_q_rows(rows_s(qn), KV_NSA, hg_n, 1), cmp_s, _pad_new_rows(rows_s(kvn)),
                                 win_rows, l * Bs, gates_s)
        o_n_s = _sample_out_rows(o, hg_n, Ts)
        lf_past = cache_fox_logf[l][page_table].astype(F32)
        later = lambda n: jnp.asarray(np.triu(np.ones((n, n), np.float32), 1))
        in_page = jnp.einsum('ji,bpih->bpjh', later(PAGE_SIZE), lf_past, precision=lax.Precision.HIGHEST)
        pages_after = jnp.einsum('qp,bph->bqh', later(N_PAGES), jnp.sum(lf_past, axis=2),
                                 precision=lax.Precision.HIGHEST)
        ck_past = -(in_page + pages_after[:, :, None, :]).reshape(Bs, PAST_LEN, H_FOX)
        ck_past = ck_past.reshape(Bs, PAST_LEN, KV_FOX, hg_f).transpose(0, 2, 3, 1)
        c_new = jnp.cumsum(rows_s(logf), axis=1)
        c_new_g = c_new.reshape(Bs, Ts, KV_FOX, hg_f).transpose(0, 2, 3, 1)
        cq_col = jnp.pad(c_new_g, ((0, 0), (0, 0), (0, 0), (0, TOK_PAD - Ts))).reshape(Bs, KV_FOX, hg_f * TOK_PAD, 1)
        ck_new = jnp.pad(c_new_g, ((0, 0), (0, 0), (0, 0), (0, NEW_PAD - Ts)))
        o = fox_attention_sample(pt_flat, fox_rows, page0,
                                 _sample_q_rows(rows_s(qf), KV_FOX, hg_f, 1), ck_past, cq_col, ck_new,
                                 _pad_new_rows(rows_s(kvf)))
        o_f_s = _sample_out_rows(o, hg_f, Ts)

        new_d, new_n, new_w = kvd_f, kvn_f[:, :4 * KV_NSA * HEAD_DIM], kvn_f[:, 4 * KV_NSA * HEAD_DIM:]
        win_p = new_w[:n_p].reshape(B, T, 2, KV_NSA, HEAD_DIM)[:, T - min(WINDOW, T):]
        win_s = jnp.concatenate([cache_nsa_win[l], new_w[n_p:].reshape(Bs, Ts, 2, KV_NSA, HEAD_DIM)],
                                axis=1)[:, -cache_nsa_win.shape[2]:]
        st_p.append((new_d[:n_p].reshape(B, T, 2, KV_DIFF, 2 * HEAD_DIM), new_n[:n_p].reshape(B, T, 4, KV_NSA, HEAD_DIM),
                     win_p, kvf_f[:n_p].reshape(B, T, 2, KV_FOX, HEAD_DIM), logf[:n_p].reshape(B, T, H_FOX)))
        st_s.append((new_d[n_p:].reshape(Bs, Ts, 2, KV_DIFF, 2 * HEAD_DIM), new_n[n_p:].reshape(Bs, Ts, 4, KV_NSA, HEAD_DIM),
                     win_s, kvf_f[n_p:].reshape(Bs, Ts, 2, KV_FOX, HEAD_DIM), logf[n_p:].reshape(Bs, Ts, H_FOX)))

        o_d = jnp.concatenate([o_d_p, o_d_s.astype(BF16)], axis=0)
        o_n = jnp.concatenate([o_n_p, o_n_s.astype(BF16)], axis=0)
        o_f = jnp.concatenate([o_f_p, o_f_s.astype(BF16)], axis=0)
        w_gate = realign_columns(w_in, l, o_gm, 3 * D, BF16)
        b_gate = bl[:, o_gm:].reshape(3, 1, D)
        merged = merge_branches(x_bf, o_d, o_n, o_f, w_gate, b_gate, w_branch_diff, w_branch_nsa, w_branch_fox, l)
        y = matmul_bias(merged, w_out, zero_bias, tm=512, tn=512, layer=l, name="out_proj")
        h, h_bf = residual_layer_norm(x, y, ln1_g[l][None, :], ln1_b[l][None, :])
        y_rows, gates = moe_ffn(h_bf, w_router, b_router, w_gate_up, b_gate_up, w_down, b_down, l)
        x, x_bf = combine_layer_norm(h, y_rows, gates, ln2_g[l][None, :], ln2_b[l][None, :])
    y_prompt = x[:n_p].reshape(B, T, D)
    y_sample = x[n_p:].reshape(Bs, Ts, D)
    stack = lambda sts, k: jnp.stack([s[k] for s in sts], 0)
    return (y_prompt, y_sample,
            stack(st_p, 0), stack(st_p, 1), stack(st_p, 2), stack(st_p, 3), stack(st_p, 4),
            stack(st_s, 0), stack(st_s, 1), stack(st_s, 2), stack(st_s, 3), stack(st_s, 4))
```
